```python
import jax, jax.numpy as jnp
from jax import lax
import numpy as np

D_MODEL = 1024
BATCH = 4
SEQ = 8192
DEPTH = 2

HEAD_DIM = 64
RWKV_DIM = 3 * D_MODEL // 8
CONV_DIM = D_MODEL // 4
ATTN_DIM = D_MODEL - RWKV_DIM - CONV_DIM
RWKV_HEADS = RWKV_DIM // HEAD_DIM
CONV_GROUPS = CONV_DIM // HEAD_DIM
ATTN_HEADS = ATTN_DIM // HEAD_DIM
DECAY_LORA = 32
ICLR_LORA = 32
VRES_LORA = 16
GATE_LORA = 64
RWKV_IN = 3 * RWKV_DIM + DECAY_LORA + ICLR_LORA + GATE_LORA
CONV_IN = 3 * CONV_DIM
ATTN_IN = 3 * ATTN_DIM
N_IN = RWKV_IN + CONV_IN + ATTN_IN
CONV_WIDTH = 3
DILATED_PATTERNS = ((128, 1), (512, 4), (2048, 16))
ATTN_BLOCK = 128
DIL_SPAN = ATTN_BLOCK * 16
ROPE_THETA = 10000.0
D_FF = 3584
N_EXPERTS = 8
TOP_K = 2
MOE_BLOCK = 128
N_DENSE = (DEPTH + 1) // 2
N_MOE = DEPTH // 2
NORM_EPS = 1e-6
GN_EPS = 64e-5

kernel_name = 'hybrid_rwkv7_shortconv_dilatedattn_moe'


def rms_norm(x, g):
    xf = x.astype(jnp.float32)
    y = xf * lax.rsqrt(jnp.mean(xf * xf, axis=-1, keepdims=True) + NORM_EPS)
    return (y * g.astype(jnp.float32)).astype(x.dtype)


def token_shift(p, mu):
    prev = jnp.pad(p[:, :-1], ((0, 0), (1, 0), (0, 0)))
    return p + (prev - p) * mu


def swiglu(h, wg, wu, wd):
    return (jax.nn.silu(h @ wg) * (h @ wu)) @ wd


def wkv7_scan(r, w, k, v, a, b):
    bsz, _, h, n = r.shape
    xs = tuple(jnp.moveaxis(t, 1, 0) for t in (r, w, k, v, a, b))

    def step(state, inp):
        r_t, w_t, k_t, v_t, a_t, b_t = inp
        sa = jnp.einsum('bhvk,bhk->bhv', state, a_t)
        state = (state * w_t[:, :, None, :] + sa[..., None] * b_t[:, :, None, :]
                 + v_t[..., None] * k_t[:, :, None, :])
        return state, jnp.einsum('bhvk,bhk->bhv', state, r_t)

    state0 = jnp.zeros((bsz, h, n, n), jnp.float32)
    _, ys = lax.scan(step, state0, xs)
    return jnp.moveaxis(ys, 0, 1)


def rwkv7_mix(p, vres, v_first, w0, w2, a0, a2, g2, k_k, k_a, r_k, ln_w, ln_b):
    bsz, seq, _ = p.shape
    p = p.astype(jnp.float32)
    o1, o2, o3 = RWKV_DIM, 2 * RWKV_DIM, 3 * RWKV_DIM
    o4, o5 = o3 + DECAY_LORA, o3 + DECAY_LORA + ICLR_LORA
    r, k, v = p[..., :o1], p[..., o1:o2], p[..., o2:o3]
    wd, ad, gd = p[..., o3:o4], p[..., o4:o5], p[..., o5:]
    log_w = -jax.nn.softplus(-(w0 + jnp.tanh(wd) @ w2)) - 0.5
    decay = jnp.exp(-jnp.exp(log_w))
    a = jax.nn.sigmoid(a0 + ad @ a2)
    g = jax.nn.sigmoid(gd) @ g2
    if vres is None:
        v_first = v
    else:
        pv, v0, v2 = vres
        v = v + (v_first - v) * jax.nn.sigmoid(v0 + pv @ v2)
    heads = lambda t: t.reshape(bsz, seq, RWKV_HEADS, HEAD_DIM)
    kk = heads(k * k_k)
    kk = kk / jnp.maximum(jnp.linalg.norm(kk, axis=-1, keepdims=True), 1e-12)
    k = heads(k * (1.0 + (a - 1.0) * k_a))
    r, v_h, a_h, decay = heads(r), heads(v), heads(a), heads(decay)
    y = wkv7_scan(r, decay, k, v_h, -kk, kk * a_h)
    mean = jnp.mean(y, axis=-1, keepdims=True)
    var = jnp.mean(jnp.square(y - mean), axis=-1, keepdims=True)
    y = ((y - mean) * lax.rsqrt(var + GN_EPS)).reshape(bsz, seq, RWKV_DIM) * ln_w + ln_b
    bonus = jnp.sum(r * k * r_k, axis=-1, keepdims=True) * v_h
    out = (y + bonus.reshape(bsz, seq, RWKV_DIM)) * g
    return out, v_first


def short_conv_mix(p, conv_w, out_gain):
    bsz, seq, _ = p.shape
    b_gate, c_gate, xc = p[..., :CONV_DIM], p[..., CONV_DIM:2 * CONV_DIM], p[..., 2 * CONV_DIM:]
    u = c_gate * xc
    filt = conv_w.astype(u.dtype)[:, None, :]
    y = lax.conv_general_dilated(u, filt, window_strides=(1,), padding=[(CONV_WIDTH - 1, 0)],
                                 dimension_numbers=('NWC', 'WIO', 'NWC'),
                                 feature_group_count=CONV_DIM)
    y = (b_gate * y).reshape(bsz, seq, CONV_GROUPS, HEAD_DIM)
    return rms_norm(y, out_gain).reshape(bsz, seq, CONV_DIM)


def rope(x):
    seq, dim = x.shape[1], x.shape[-1]
    half = dim // 2
    inv_freq = ROPE_THETA ** (-jnp.arange(half, dtype=jnp.float32) * 2.0 / dim)
    ang = jnp.arange(seq, dtype=jnp.float32)[:, None] * inv_freq[None, :]
    cos = jnp.cos(ang)[None, :, None, :]
    sin = jnp.sin(ang)[None, :, None, :]
    xf = x.astype(jnp.float32)
    x1, x2 = xf[..., :half], xf[..., half:]
    return jnp.concatenate([x1 * cos - x2 * sin, x2 * cos + x1 * sin], axis=-1).astype(x.dtype)


def to_dilated(x, dil):
    bsz, sp, h, d = x.shape
    return x.reshape(bsz, sp // dil, dil, h, d).transpose(0, 2, 3, 1, 4).reshape(bsz * dil, h, sp // dil, d)


def from_dilated(y, dil, bsz):
    _, h, l, d = y.shape
    return y.reshape(bsz, dil, h, l, d).transpose(0, 3, 1, 2, 4).reshape(bsz, l * dil, h, d)


def banded_attention(q, k, v, n_back):
    n, h, l, d = q.shape
    nb, blk = l // ATTN_BLOCK, ATTN_BLOCK
    f32 = jnp.float32
    to_blocks = lambda t: t.astype(f32).reshape(n, h, nb, blk, d)
    qb, kb, vb = to_blocks(q), to_blocks(k), to_blocks(v)
    prev = lambda t: jnp.pad(t[:, :, :-1], ((0, 0), (0, 0), (1, 0), (0, 0), (0, 0)))
    kw = jnp.concatenate([prev(kb), kb], axis=3)
    vw = jnp.concatenate([prev(vb), vb], axis=3)
    s = jnp.einsum('nhbqd,nhbkd->nhbqk', qb, kw) * (d ** -0.5)
    qi = jnp.arange(blk)[:, None]
    kj = jnp.arange(2 * blk)[None, :]
    dist = qi + blk - kj
    band = (dist >= 0) & (dist <= n_back)
    has_prev = (jnp.arange(nb)[:, None, None] > 0) | (kj >= blk)[None]
    mask = band[None] & has_prev
    s = jnp.where(mask, s, -jnp.inf)
    m = jnp.max(s, axis=-1, keepdims=True)
    pexp = jnp.exp(s - m)
    den = jnp.sum(pexp, axis=-1, keepdims=True)
    o = jnp.einsum('nhbqk,nhbkd->nhbqd', pexp, vw) / den
    lse = (m + jnp.log(den))[..., 0]
    return o.reshape(n, h, l, d), lse.reshape(n, h, l)


def dilated_attention_mix(p, q_gain, k_gain, out_gain):
    bsz, seq, _ = p.shape
    part = lambda i: p[..., i * ATTN_DIM:(i + 1) * ATTN_DIM].reshape(bsz, seq, ATTN_HEADS, HEAD_DIM)
    q = rope(rms_norm(part(0), q_gain))
    k = rope(rms_norm(part(1), k_gain))
    v = part(2)
    s_pad = -(-seq // DIL_SPAN) * DIL_SPAN
    padw = ((0, 0), (0, s_pad - seq), (0, 0), (0, 0))
    q, k, v = jnp.pad(q, padw), jnp.pad(k, padw), jnp.pad(v, padw)
    outs, lses = [], []
    for window, dil in DILATED_PATTERNS:
        o, lse = banded_attention(to_dilated(q, dil), to_dilated(k, dil), to_dilated(v, dil), window // dil)
        outs.append(from_dilated(o, dil, bsz)[:, :seq])
        lses.append(from_dilated(lse[..., None], dil, bsz)[:, :seq, :, 0])
    wts = jax.nn.softmax(jnp.stack(lses), axis=0)
    o = jnp.einsum('pbsh,pbshd->bshd', wts, jnp.stack(outs))
    o = rms_norm(o.astype(p.dtype), out_gain)
    return o.reshape(bsz, seq, ATTN_DIM)


def moe_swiglu(h, router, wg, wu, wd):
    bsz, seq, d = h.shape
    t = bsz * seq
    xt = h.reshape(t, d)
    logits = (xt @ router).astype(jnp.float32)
    topv, topi = lax.top_k(logits, TOP_K)
    gates = jax.nn.softmax(topv, axis=-1)
    flat_e = topi.reshape(-1)
    flat_tok = jnp.repeat(jnp.arange(t, dtype=jnp.int32), TOP_K)
    flat_g = gates.reshape(-1)
    order = jnp.argsort(flat_e)
    se = flat_e[order]
    counts = jnp.bincount(flat_e, length=N_EXPERTS)
    padded = (counts + MOE_BLOCK - 1) // MOE_BLOCK * MOE_BLOCK
    start = jnp.cumsum(counts) - counts
    ends_p = jnp.cumsum(padded)
    pstart = ends_p - padded
    dest = pstart[se] + (jnp.arange(t * TOP_K) - start[se])
    n_rows = t * TOP_K + N_EXPERTS * MOE_BLOCK
    row_tok = jnp.full((n_rows,), t, jnp.int32).at[dest].set(flat_tok[order])
    row_gate = jnp.zeros((n_rows,), jnp.float32).at[dest].set(flat_g[order])
    n_blocks = n_rows // MOE_BLOCK
    block_expert = jnp.minimum(
        jnp.searchsorted(ends_p, jnp.arange(n_blocks) * MOE_BLOCK, side='right'), N_EXPERTS - 1)
    xpad = jnp.concatenate([xt, jnp.zeros((1, d), xt.dtype)], axis=0)
    xb = xpad[row_tok].reshape(n_blocks, MOE_BLOCK, d)

    def expert_block(args):
        xblk, e = args
        return swiglu(xblk, wg[e], wu[e], wd[e])

    yb = lax.map(expert_block, (xb, block_expert))
    y = (yb.reshape(n_rows, d) * row_gate[:, None]).astype(h.dtype)
    out = jnp.zeros((t + 1, d), h.dtype).at[row_tok].add(y)[:t]
    return out.reshape(bsz, seq, d)


def setup_inputs(seed: int = 0) -> dict:
    key = jax.random.key(seed)
    ks = jax.random.split(key, 32)
    f32 = jnp.float32
    L = DEPTH
    nrm = lambda i, shape, scale: scale * jax.random.normal(ks[i], shape, f32)
    gain = lambda i, shape: 1.0 + nrm(i, shape, 0.02)
    unif = lambda i, shape, lo, hi: jax.random.uniform(ks[i], shape, f32, lo, hi)
    return {
        'x': nrm(0, (BATCH, SEQ, D_MODEL), 1.0),
        'mix_norm': gain(1, (L, D_MODEL)),
        'w_in': nrm(2, (L, D_MODEL, N_IN), D_MODEL ** -0.5),
        'tshift_mu': unif(3, (L, RWKV_IN), 0.0, 1.0),
        'vres_w_in': nrm(4, (L - 1, D_MODEL, VRES_LORA), D_MODEL ** -0.5),
        'vres_mu': unif(5, (L - 1, VRES_LORA), 0.0, 1.0),
        'vres_v0': 1.0 + nrm(6, (L - 1, RWKV_DIM), 0.3),
        'vres_v2': nrm(7, (L - 1, VRES_LORA, RWKV_DIM), VRES_LORA ** -0.5),
        'decay_w0': unif(8, (L, RWKV_DIM), -5.0, -0.5),
        'decay_w2': nrm(9, (L, DECAY_LORA, RWKV_DIM), 0.5 * DECAY_LORA ** -0.5),
        'iclr_a0': nrm(10, (L, RWKV_DIM), 0.5),
        'iclr_a2': nrm(11, (L, ICLR_LORA, RWKV_DIM), 0.5 * ICLR_LORA ** -0.5),
        'gate_g2': nrm(12, (L, GATE_LORA, RWKV_DIM), GATE_LORA ** -0.5),
        'k_k': 0.85 + nrm(13, (L, RWKV_DIM), 0.1),
        'k_a': 1.0 + nrm(14, (L, RWKV_DIM), 0.1),
        'r_k': nrm(15, (L, RWKV_HEADS, HEAD_DIM), 0.1),
        'ln_x_w': gain(16, (L, RWKV_DIM)),
        'ln_x_b': nrm(17, (L, RWKV_DIM), 0.02),
        'conv_w': nrm(18, (L, CONV_WIDTH, CONV_DIM), CONV_WIDTH ** -0.5),
        'conv_out_norm': gain(19, (L, CONV_GROUPS, HEAD_DIM)),
        'q_norm': gain(20, (L, HEAD_DIM)),
        'k_norm': gain(21, (L, HEAD_DIM)),
        'attn_out_norm': gain(22, (L, ATTN_HEADS, HEAD_DIM)),
        'w_o': nrm(23, (L, D_MODEL, D_MODEL), D_MODEL ** -0.5),
        'ffn_norm': gain(24, (L, D_MODEL)),
        'dense_wg': nrm(25, (N_DENSE, D_MODEL, D_FF), D_MODEL ** -0.5),
        'dense_wu': nrm(26, (N_DENSE, D_MODEL, D_FF), D_MODEL ** -0.5),
        'dense_wd': nrm(27, (N_DENSE, D_FF, D_MODEL), D_FF ** -0.5),
        'router': nrm(28, (N_MOE, D_MODEL, N_EXPERTS), D_MODEL ** -0.5),
        'moe_wg': nrm(29, (N_MOE, N_EXPERTS, D_MODEL, D_FF), D_MODEL ** -0.5),
        'moe_wu': nrm(30, (N_MOE, N_EXPERTS, D_MODEL, D_FF), D_MODEL ** -0.5),
        'moe_wd': nrm(31, (N_MOE, N_EXPERTS, D_FF, D_MODEL), D_FF ** -0.5),
    }


def reference(x, mix_norm, w_in, tshift_mu, vres_w_in, vres_mu, vres_v0, vres_v2,
              decay_w0, decay_w2, iclr_a0, iclr_a2, gate_g2, k_k, k_a, r_k, ln_x_w, ln_x_b,
              conv_w, conv_out_norm, q_norm, k_norm, attn_out_norm, w_o, ffn_norm,
              dense_wg, dense_wu, dense_wd, router, moe_wg, moe_wu, moe_wd):
    v_first = None
    for l in range(DEPTH):
        h = rms_norm(x, mix_norm[l])
        if l == 0:
            w_comb = w_in[l]
        else:
            w_comb = jnp.concatenate([w_in[l], vres_w_in[l - 1]], axis=1)
        p = h @ w_comb
        p_rwkv = token_shift(p[..., :RWKV_IN], tshift_mu[l])
        p_conv = p[..., RWKV_IN:RWKV_IN + CONV_IN]
        p_attn = p[..., RWKV_IN + CONV_IN:N_IN]
        if l == 0:
            vres = None
        else:
            vres = (token_shift(p[..., N_IN:], vres_mu[l - 1]), vres_v0[l - 1], vres_v2[l - 1])
        y_rwkv, v_first = rwkv7_mix(p_rwkv, vres, v_first, decay_w0[l], decay_w2[l], iclr_a0[l],
                                    iclr_a2[l], gate_g2[l], k_k[l], k_a[l], r_k[l],
                                    ln_x_w[l], ln_x_b[l])
        y_conv = short_conv_mix(p_conv, conv_w[l], conv_out_norm[l])
        y_attn = dilated_attention_mix(p_attn, q_norm[l], k_norm[l], attn_out_norm[l])
        y = jnp.concatenate([y_rwkv.astype(x.dtype), y_conv, y_attn], axis=-1)
        x = x + y @ w_o[l]
        h = rms_norm(x, ffn_norm[l])
        if l % 2 == 0:
            i = l // 2
            x = x + swiglu(h, dense_wg[i], dense_wu[i], dense_wd[i])
        else:
            i = l // 2
            x = x + moe_swiglu(h, router[i], moe_wg[i], moe_wu[i], moe_wd[i])
    return x
```

```python
import functools

import jax
import jax.numpy as jnp
from jax import lax
from jax.experimental import pallas as pl
from jax.experimental.pallas import tpu as pltpu

F32 = jnp.float32
BF16 = jnp.bfloat16

D_MODEL = 1024
HEAD_DIM = 64
RWKV_DIM = 384
CONV_DIM = 256
ATTN_DIM = 384
LORA_W = 128
RWKV_IN = 3 * RWKV_DIM + LORA_W
CONV_IN = 3 * CONV_DIM
ATTN_IN = 3 * ATTN_DIM
VRES_LORA = 16
VRES_PAD = 128
CONV_WIDTH = 3
DILATIONS = (1, 4, 16)
ATTN_BLOCK = 128
DIL_SPAN = ATTN_BLOCK * 16
ROPE_THETA = 10000.0
D_FF = 3584
N_EXPERTS = 8
NORM_EPS = 1e-6
GN_EPS = 64e-5

LANES = 128
RWKV_CHUNK = 64
PAIR = 2 * HEAD_DIM
VMEM_LIMIT = 56 * 1024 * 1024


def _params(*sem):
    return pltpu.CompilerParams(dimension_semantics=sem, vmem_limit_bytes=VMEM_LIMIT)


def _bdot(a, b):
    return jnp.dot(a.astype(BF16), b.astype(BF16), preferred_element_type=F32)


def _bdot_nt(a, b):
    return lax.dot_general(a.astype(BF16), b.astype(BF16), (((1,), (1,)), ((), ())),
                           preferred_element_type=F32)


def _shift_rows(x, carry_row, n=1):
    row = lax.broadcasted_iota(jnp.int32, (x.shape[0], 1), 0)
    out = pltpu.roll(x, n, 0)
    for i in range(n):
        out = jnp.where(row == i, carry_row[i:i + 1, :], out)
    return out


def _inproj_body(x_ref, g_ref, w_ref, *o_refs):
    x = x_ref[...]
    h = x * lax.rsqrt(jnp.mean(x * x, axis=-1, keepdims=True) + NORM_EPS) * g_ref[...]
    p = jnp.dot(h.astype(BF16), w_ref[...], preferred_element_type=F32)
    off = 0
    for o_ref in o_refs:
        n = o_ref.shape[-1]
        o_ref[...] = p[:, off:off + n].astype(o_ref.dtype)
        off += n


def _inproj(x2d, gain, w, widths, tm):
    t, d = x2d.shape
    n = w.shape[1]
    return pl.pallas_call(
        _inproj_body,
        grid=(t // tm,),
        in_specs=[pl.BlockSpec((tm, d), lambda i: (i, 0)),
                  pl.BlockSpec((1, d), lambda i: (0, 0)),
                  pl.BlockSpec((d, n), lambda i: (0, 0))],
        out_specs=[pl.BlockSpec((tm, wd), lambda i: (i, 0)) for wd in widths],
        out_shape=[jax.ShapeDtypeStruct((t, wd), F32) for wd in widths],
        compiler_params=_params("arbitrary"),
        name="inproj",
    )(x2d, gain, w)


def _rwkv_body(*refs, has_vres):
    if has_vres:
        (p_ref, pv_ref, vf_ref, mu_ref, vmu_ref, wl_ref, v2_ref, vec_ref, seg_ref, tri_ref,
         cm_ref, y_ref, prev_ref, pprev_ref, h_ref) = refs
    else:
        (p_ref, mu_ref, wl_ref, vec_ref, seg_ref, tri_ref,
         cm_ref, y_ref, vout_ref, prev_ref, h_ref) = refs
    c = pl.program_id(1)

    @pl.when(c == 0)
    def _():
        prev_ref[...] = jnp.zeros_like(prev_ref)
        h_ref[...] = jnp.zeros_like(h_ref)
        if has_vres:
            pprev_ref[...] = jnp.zeros_like(pprev_ref)

    p = p_ref[...]
    nt = p.shape[0]
    prev = _shift_rows(p, prev_ref[...])
    prev_ref[...] = p[nt - 1:nt, :]
    x = p + (prev - p) * mu_ref[...]
    r = x[:, 0:RWKV_DIM]
    k = x[:, RWKV_DIM:2 * RWKV_DIM]
    v = x[:, 2 * RWKV_DIM:3 * RWKV_DIM]
    z = x[:, 3 * RWKV_DIM:]
    lane = lax.broadcasted_iota(jnp.int32, (1, LANES), 1)
    zt = jnp.where(lane < 32, jnp.tanh(z), jnp.where(lane < 64, z, jax.nn.sigmoid(z)))
    lo = _bdot(zt, wl_ref[...])
    w0, a0, k_k, k_a, r_k, ln_w, ln_b, v0 = (vec_ref[i:i + 1, :] for i in range(8))
    seg = seg_ref[...]

    zz = -(w0 + lo[:, 0:RWKV_DIM])
    softplus = jnp.maximum(zz, 0.0) + jnp.log(1.0 + jnp.exp(-jnp.abs(zz)))
    lw = -jnp.exp(-softplus - 0.5)
    a = jax.nn.sigmoid(a0 + lo[:, RWKV_DIM:2 * RWKV_DIM])
    g = lo[:, 2 * RWKV_DIM:]
    if has_vres:
        pv = pv_ref[...]
        pvprev = _shift_rows(pv, pprev_ref[...])
        pprev_ref[...] = pv[nt - 1:nt, :]
        xv = pv + (pvprev - pv) * vmu_ref[...]
        v = v + (vf_ref[...] - v) * jax.nn.sigmoid(v0 + _bdot(xv, v2_ref[...]))
    else:
        vout_ref[...] = v
    kk = k * k_k
    kk = kk / jnp.maximum(jnp.sqrt(_bdot(kk * kk, seg)), 1e-12)
    kmod = k * (1.0 + (a - 1.0) * k_a)
    bonus = _bdot(r * kmod * r_k, seg) * v

    tri = tri_ref[...]
    lw_hi = lw.astype(BF16)
    lw_lo = (lw - lw_hi.astype(F32)).astype(BF16)
    cum = (jnp.dot(tri, lw_hi, preferred_element_type=F32)
           + jnp.dot(tri, lw_lo, preferred_element_type=F32))
    pinv = jnp.exp(-cum)
    r_t = r * jnp.exp(cum)
    a_t = -kk * jnp.exp(cum - lw)
    b_t = kk * a * pinv
    k_t = kmod * pinv
    p_end = jnp.exp(cum[nt - 1:nt, :])
    bh_t = b_t * p_end
    kh_t = k_t * p_end

    strict, incl, blk8, eye = (cm_ref[i] for i in range(4))
    m0 = (lane < HEAD_DIM).astype(F32)
    m1 = 1.0 - m0

    def stack(t, sl):
        ts = t[:, sl]
        return jnp.concatenate([ts * m0, ts * m1], axis=0)

    ys = []
    for j in range(RWKV_DIM // PAIR):
        sl = slice(j * PAIR, (j + 1) * PAIR)
        a_st, r_st, b_st, k_st, v_st = (stack(t, sl) for t in (a_t, r_t, b_t, k_t, v))
        gram = _bdot_nt(jnp.concatenate([a_st, r_st], axis=0),
                        jnp.concatenate([b_st, k_st], axis=0))
        n2 = 2 * nt
        a_ab = gram[:n2, :n2] * strict
        a_ak = gram[:n2, n2:] * strict
        a_rb = gram[n2:, :n2] * incl
        a_rk = gram[n2:, n2:] * incl
        dg = a_ab * blk8
        off = a_ab - dg
        d2 = _bdot(dg, dg)
        tdiag = eye + dg
        tdiag = tdiag + _bdot(tdiag, d2)
        tdiag = tdiag + _bdot(tdiag, _bdot(d2, d2))
        e1 = _bdot(tdiag, off)
        e2 = _bdot(e1, e1)
        yb = eye + e1
        yb = yb + _bdot(yb, e2)
        yb = yb + _bdot(yb, _bdot(e2, e2))
        tinv = _bdot(yb, tdiag)

        h = h_ref[j]
        xs = _bdot(jnp.concatenate([a_st, a_ak], axis=1), jnp.concatenate([h, v_st], axis=0))
        u = _bdot(tinv, xs)
        y_st = _bdot(jnp.concatenate([r_st, a_rb, a_rk], axis=1),
                     jnp.concatenate([h, u, v_st], axis=0))
        ys.append(y_st[:nt] + y_st[nt:])
        pe = p_end[:, sl]
        pe_col = jnp.sum(eye * pe, axis=1, keepdims=True)
        kb = jnp.concatenate([stack(bh_t, sl), stack(kh_t, sl)], axis=0)
        h_ref[j] = pe_col * h + _bdot(kb.T, jnp.concatenate([u, v_st], axis=0))

    y = jnp.concatenate(ys, axis=1)
    inv_n = 1.0 / HEAD_DIM
    mean = _bdot(y, seg) * inv_n
    dy = y - mean
    var = _bdot(dy * dy, seg) * inv_n
    yn = dy * lax.rsqrt(var + GN_EPS) * ln_w + ln_b
    y_ref[...] = ((yn + bonus) * g).astype(y_ref.dtype)


def _rwkv(p_rwkv, mu, wl, vecs, consts, vres=None):
    b, s, _ = p_rwkv.shape
    c = RWKV_CHUNK
    seg, tri, cm = consts
    tile = lambda wd: pl.BlockSpec((None, c, wd), lambda i, j: (i, j, 0))
    full = lambda arr: pl.BlockSpec(arr.shape, lambda i, j: (0,) * arr.ndim)
    y_shape = jax.ShapeDtypeStruct((b, s, RWKV_DIM), BF16)
    scratch = [pltpu.VMEM((1, RWKV_IN), F32)]
    if vres is None:
        ins = [p_rwkv, mu, wl, vecs, seg, tri, cm]
        in_specs = [tile(RWKV_IN)] + [full(t) for t in ins[1:]]
        out_shape = [y_shape, jax.ShapeDtypeStruct((b, s, RWKV_DIM), F32)]
        out_specs = [tile(RWKV_DIM), tile(RWKV_DIM)]
    else:
        p_vres, v_first, vmu, v2 = vres
        ins = [p_rwkv, p_vres, v_first, mu, vmu, wl, v2, vecs, seg, tri, cm]
        in_specs = [tile(RWKV_IN), tile(VRES_PAD), tile(RWKV_DIM)] + [full(t) for t in ins[3:]]
        out_shape = [y_shape]
        out_specs = [tile(RWKV_DIM)]
        scratch.append(pltpu.VMEM((1, VRES_PAD), F32))
    scratch.append(pltpu.VMEM((RWKV_DIM // PAIR, PAIR, PAIR), F32))
    return pl.pallas_call(
        functools.partial(_rwkv_body, has_vres=vres is not None),
        grid=(b, s // c),
        in_specs=in_specs, out_specs=out_specs, out_shape=out_shape,
        scratch_shapes=scratch,
        compiler_params=_params("arbitrary", "arbitrary"),
        name="rwkv7",
    )(*ins)


def _conv_body(p_ref, w_ref, gain_ref, seg_ref, o_ref, carry_ref):
    @pl.when(pl.program_id(1) == 0)
    def _():
        carry_ref[...] = jnp.zeros_like(carry_ref)

    p = p_ref[...]
    nt = p.shape[0]
    u = p[:, CONV_DIM:2 * CONV_DIM] * p[:, 2 * CONV_DIM:]
    carry = carry_ref[...]
    u1 = _shift_rows(u, carry[1:2, :], 1)
    u2 = _shift_rows(u, carry, 2)
    carry_ref[...] = u[nt - 2:nt, :]
    w = w_ref[...]
    y = p[:, 0:CONV_DIM] * (w[0:1, :] * u2 + w[1:2, :] * u1 + w[2:3, :] * u)
    ms = _bdot(y * y, seg_ref[...]) * (1.0 / HEAD_DIM)
    o_ref[...] = (y * lax.rsqrt(ms + NORM_EPS) * gain_ref[...]).astype(o_ref.dtype)


def _short_conv(p_conv, w, gain, seg, tm):
    b, s, _ = p_conv.shape
    return pl.pallas_call(
        _conv_body,
        grid=(b, s // tm),
        in_specs=[pl.BlockSpec((None, tm, CONV_IN), lambda i, j: (i, j, 0)),
                  pl.BlockSpec(w.shape, lambda i, j: (0, 0)),
                  pl.BlockSpec(gain.shape, lambda i, j: (0, 0)),
                  pl.BlockSpec(seg.shape, lambda i, j: (0, 0))],
        out_specs=pl.BlockSpec((None, tm, CONV_DIM), lambda i, j: (i, j, 0)),
        out_shape=jax.ShapeDtypeStruct((b, s, CONV_DIM), BF16),
        scratch_shapes=[pltpu.VMEM((CONV_WIDTH - 1, CONV_DIM), F32)],
        compiler_params=_params("arbitrary", "arbitrary"),
        name="short_conv",
    )(p_conv, w, gain, seg)


def _attn_prep_body(p_ref, cos_ref, sin_ref, qg_ref, kg_ref, seg_ref, q_ref, k_ref, v_ref):
    p = p_ref[...]
    seg = seg_ref[...]
    lane = lax.broadcasted_iota(jnp.int32, (1, LANES), 1)
    first = (lane & (HEAD_DIM - 1)) < HEAD_DIM // 2
    for part, g_ref, o_ref, scale in ((0, qg_ref, q_ref, HEAD_DIM ** -0.5), (1, kg_ref, k_ref, 1.0)):
        x = p[:, part * ATTN_DIM:(part + 1) * ATTN_DIM]
        ms = _bdot(x * x, seg) * (1.0 / HEAD_DIM)
        xn = x * lax.rsqrt(ms + NORM_EPS) * g_ref[...]
        rot = []
        for j in range(ATTN_DIM // LANES):
            xs = xn[:, j * LANES:(j + 1) * LANES]
            rot.append(jnp.where(first, pltpu.roll(xs, LANES - HEAD_DIM // 2, 1),
                                 pltpu.roll(xs, HEAD_DIM // 2, 1)))
        out = xn * cos_ref[...] + jnp.concatenate(rot, axis=1) * sin_ref[...]
        o_ref[...] = (out * scale).astype(o_ref.dtype)
    v_ref[...] = p[:, 2 * ATTN_DIM:].astype(v_ref.dtype)


def _attn_prep(p_attn, cos, sin, qg, kg, seg, tm):
    b, s, _ = p_attn.shape
    tile = lambda wd: pl.BlockSpec((None, tm, wd), lambda i, j: (i, j, 0))
    tab = pl.BlockSpec((tm, ATTN_DIM), lambda i, j: (j, 0))
    full = lambda arr: pl.BlockSpec(arr.shape, lambda i, j: (0, 0))
    shp = jax.ShapeDtypeStruct((b, s, ATTN_DIM), BF16)
    return pl.pallas_call(
        _attn_prep_body,
        grid=(b, s // tm),
        in_specs=[tile(ATTN_IN), tab, tab, full(qg), full(kg), full(seg)],
        out_specs=[tile(ATTN_DIM)] * 3,
        out_shape=[shp] * 3,
        compiler_params=_params("arbitrary", "arbitrary"),
        name="attn_prep",
    )(p_attn, cos, sin, qg, kg, seg)


def _attn_body(q_ref, k_ref, v_ref, kprev_ref, vprev_ref, o_ref, l_ref):
    mb = pl.program_id(2)
    q = q_ref[...]
    kcat = jnp.concatenate([kprev_ref[...], k_ref[...]], axis=0)
    vcat = jnp.concatenate([vprev_ref[...], v_ref[...]], axis=0)
    blk = q.shape[0]
    qi = lax.broadcasted_iota(jnp.int32, (blk, 2 * blk), 0)
    kj = lax.broadcasted_iota(jnp.int32, (blk, 2 * blk), 1)
    dist = qi + blk - kj
    mask = (dist >= 0) & (dist <= ATTN_BLOCK) & (kj + mb * (2 * blk) >= blk)
    lane = lax.broadcasted_iota(jnp.int32, (1, LANES), 1)
    head0 = lane < HEAD_DIM
    m0 = head0.astype(F32)
    for j in range(ATTN_DIM // PAIR):
        sl = slice(j * PAIR, (j + 1) * PAIR)
        qp, kp, vp = q[:, sl].astype(F32), kcat[:, sl], vcat[:, sl]
        outs, lses = [], []
        for hm in (m0, 1.0 - m0):
            s = _bdot_nt(qp * hm, kp)
            s = jnp.where(mask, s, -jnp.inf)
            m = jnp.max(s, axis=-1, keepdims=True)
            pe = jnp.exp(s - m)
            den = jnp.sum(pe, axis=-1, keepdims=True)
            outs.append(_bdot(pe, vp) / den)
            lses.append(m + jnp.log(den))
        o_ref[:, sl] = jnp.where(head0, outs[0], outs[1])
        l_ref[:, sl] = jnp.where(head0, lses[0], lses[1])


def _attn(q, k, v, dil):
    b, s, _ = q.shape
    ld = s // dil
    view = lambda t: t.reshape(b, ld, dil * ATTN_DIM)
    blk = (None, ATTN_BLOCK, ATTN_DIM)
    spec = pl.BlockSpec(blk, lambda i, r, m: (i, m, r))
    prev = pl.BlockSpec(blk, lambda i, r, m: (i, jnp.maximum(m - 1, 0), r))
    shp = jax.ShapeDtypeStruct((b, ld, dil * ATTN_DIM), F32)
    o, lse = pl.pallas_call(
        _attn_body,
        grid=(b, dil, ld // ATTN_BLOCK),
        in_specs=[spec, spec, spec, prev, prev],
        out_specs=[spec, spec],
        out_shape=[shp, shp],
        compiler_params=_params("arbitrary", "arbitrary", "arbitrary"),
        name=f"attn_d{dil}",
    )(view(q), view(k), view(v), view(k), view(v))
    return o.reshape(b, s, ATTN_DIM), lse.reshape(b, s, ATTN_DIM)


def _attn_combine_body(*refs):
    n = len(DILATIONS)
    o_refs, l_refs = refs[:n], refs[n:2 * n]
    gain_ref, seg_ref, y_ref = refs[2 * n:]
    ls = [r[...] for r in l_refs]
    m = functools.reduce(jnp.maximum, ls)
    es = [jnp.exp(l - m) for l in ls]
    den = functools.reduce(jnp.add, es)
    o = functools.reduce(jnp.add, [e * r[...] for e, r in zip(es, o_refs)]) / den
    ms = _bdot(o * o, seg_ref[...]) * (1.0 / HEAD_DIM)
    y_ref[...] = (o * lax.rsqrt(ms + NORM_EPS) * gain_ref[...]).astype(y_ref.dtype)


def _attn_combine(os_, ls_, gain, seg, tm):
    b, s, _ = os_[0].shape
    tile = pl.BlockSpec((None, tm, ATTN_DIM), lambda i, j: (i, j, 0))
    full = lambda arr: pl.BlockSpec(arr.shape, lambda i, j: (0, 0))
    return pl.pallas_call(
        _attn_combine_body,
        grid=(b, s // tm),
        in_specs=[tile] * (2 * len(os_)) + [full(gain), full(seg)],
        out_specs=tile,
        out_shape=jax.ShapeDtypeStruct((b, s, ATTN_DIM), BF16),
        compiler_params=_params("arbitrary", "arbitrary"),
        name="attn_combine",
    )(*os_, *ls_, gain, seg)


INFO_E0, INFO_E1, INFO_R0, INFO_R1, INFO_G0, INFO_G1 = range(6)


def _outproj_body(*refs, has_router):
    if has_router:
        (x_ref, yr_ref, yc_ref, ya_ref, wo_ref, g_ref, rwh_ref, rwl_ref, tri_ref,
         xn_ref, h_ref, info_ref, cnt_ref, run_ref) = refs
    else:
        x_ref, yr_ref, yc_ref, ya_ref, wo_ref, g_ref, xn_ref, h_ref = refs
    o1, o2 = RWKV_DIM, RWKV_DIM + CONV_DIM
    acc = jnp.dot(yr_ref[...], wo_ref[0:o1, :], preferred_element_type=F32)
    acc += jnp.dot(yc_ref[...], wo_ref[o1:o2, :], preferred_element_type=F32)
    acc += jnp.dot(ya_ref[...], wo_ref[o2:, :], preferred_element_type=F32)
    xn = x_ref[...] + acc
    xn_ref[...] = xn
    h = xn * lax.rsqrt(jnp.mean(xn * xn, axis=-1, keepdims=True) + NORM_EPS) * g_ref[...]
    h_ref[...] = h.astype(h_ref.dtype)
    if not has_router:
        return

    @pl.when(pl.program_id(0) == 0)
    def _():
        run_ref[...] = jnp.zeros_like(run_ref)

    h_hi = h.astype(BF16)
    h_lo = (h - h_hi.astype(F32)).astype(BF16)
    rwh = rwh_ref[...]
    logits = (jnp.dot(h_hi, rwh, preferred_element_type=F32)
              + jnp.dot(h_lo, rwh, preferred_element_type=F32)
              + jnp.dot(h_hi, rwl_ref[...], preferred_element_type=F32))
    lane = lax.broadcasted_iota(jnp.int32, (1, LANES), 1)
    logits = jnp.where(lane < N_EXPERTS, logits, -jnp.inf)

    def top(lg):
        m = jnp.max(lg, axis=-1, keepdims=True)
        idx = jnp.min(jnp.where(lg == m, lane, LANES), axis=-1, keepdims=True)
        return m, idx, lane == idx

    m1, i1, oh1 = top(logits)
    m2, i2, oh2 = top(jnp.where(oh1, -jnp.inf, logits))
    zexp = jnp.exp(m2 - m1)
    g0 = 1.0 / (1.0 + zexp)
    g1 = zexp * g0
    oh = oh1.astype(F32) + oh2.astype(F32)
    tot = jnp.dot(tri_ref[...], oh.astype(BF16), preferred_element_type=F32) + run_ref[...]
    r0 = jnp.sum(jnp.where(oh1, tot, 0.0), axis=-1, keepdims=True)
    r1 = jnp.sum(jnp.where(oh2, tot, 0.0), axis=-1, keepdims=True)
    run_ref[...] += jnp.sum(oh, axis=0, keepdims=True)
    cnt_ref[...] = run_ref[...]
    info = jnp.zeros(logits.shape, F32)
    for ln, val in ((INFO_E0, i1.astype(F32)), (INFO_E1, i2.astype(F32)), (INFO_R0, r0),
                    (INFO_R1, r1), (INFO_G0, g0), (INFO_G1, g1)):
        info = jnp.where(lane == ln, val, info)
    info_ref[...] = info


def _outproj(x2d, yr, yc, ya, wo, gain, tm, router=None):
    t, d = x2d.shape
    row = lambda wd: pl.BlockSpec((tm, wd), lambda i: (i, 0))
    full = lambda arr: pl.BlockSpec(arr.shape, lambda i: (0, 0))
    ins = [x2d, yr, yc, ya, wo, gain]
    in_specs = [row(d), row(RWKV_DIM), row(CONV_DIM), row(ATTN_DIM), full(wo), full(gain)]
    out_shape = [jax.ShapeDtypeStruct((t, d), F32)]
    out_specs = [row(d)]
    scratch = []
    if router is None:
        out_shape.append(jax.ShapeDtypeStruct((t, d), BF16))
        out_specs.append(row(d))
    else:
        ins += list(router)
        in_specs += [full(a) for a in router]
        out_shape += [jax.ShapeDtypeStruct((t, d), F32), jax.ShapeDtypeStruct((t, LANES), F32),
                      jax.ShapeDtypeStruct((1, LANES), F32)]
        out_specs += [row(d), row(LANES), pl.BlockSpec((1, LANES), lambda i: (0, 0))]
        scratch = [pltpu.VMEM((1, LANES), F32)]
    return pl.pallas_call(
        functools.partial(_outproj_body, has_router=router is not None),
        grid=(t // tm,),
        in_specs=in_specs, out_specs=out_specs, out_shape=out_shape,
        scratch_shapes=scratch,
        compiler_params=_params("arbitrary"),
        name="outproj",
    )(*ins)


def _ffn_body(te_ref, nu_ref, x_ref, wg_ref, wu_ref, wd_ref, *rest, has_res):
    if has_res:
        res_ref, o_ref, acc_ref = rest
    else:
        o_ref, acc_ref = rest
    i, f = pl.program_id(0), pl.program_id(1)

    @pl.when(f == 0)
    def _():
        acc_ref[...] = jnp.zeros_like(acc_ref)

    @pl.when(i < nu_ref[0])
    def _():
        x = x_ref[...].astype(BF16)
        hg = jnp.dot(x, wg_ref[...], preferred_element_type=F32)
        hu = jnp.dot(x, wu_ref[...], preferred_element_type=F32)
        act = hg * jax.nn.sigmoid(hg) * hu
        acc_ref[...] += jnp.dot(act.astype(BF16), wd_ref[...], preferred_element_type=F32)

    @pl.when(f == pl.num_programs(1) - 1)
    def _():
        out = acc_ref[...]
        if has_res:
            out = out + res_ref[...]
        o_ref[...] = out.astype(o_ref.dtype)


def _ffn(x, wg, wu, wd, tile_expert, n_used, tm, tf, res=None):
    n, d = x.shape
    ff = wg.shape[-1]
    row = pl.BlockSpec((tm, d), lambda i, f, te, nu: (i, 0))
    ins = [x, wg, wu, wd]
    in_specs = [row,
                pl.BlockSpec((None, d, tf), lambda i, f, te, nu: (te[i], 0, f)),
                pl.BlockSpec((None, d, tf), lambda i, f, te, nu: (te[i], 0, f)),
                pl.BlockSpec((None, tf, d), lambda i, f, te, nu: (te[i], f, 0))]
    if res is not None:
        ins.append(res)
        in_specs.append(row)
    return pl.pallas_call(
        functools.partial(_ffn_body, has_res=res is not None),
        grid_spec=pltpu.PrefetchScalarGridSpec(
            num_scalar_prefetch=2,
            grid=(n // tm, ff // tf),
            in_specs=in_specs,
            out_specs=row,
            scratch_shapes=[pltpu.VMEM((tm, d), F32)]),
        out_shape=jax.ShapeDtypeStruct((n, d), F32),
        compiler_params=_params("arbitrary", "arbitrary"),
        name="swiglu",
    )(tile_expert, n_used, *ins)


def _row_copy(src, s, dst, d, sem):
    return pltpu.make_async_copy(src.at[pl.ds(s, 1)], dst.at[pl.ds(d, 1)], sem)


def _dispatch_body(dest_ref, h_hbm, xs_in, xs_out, sem):
    del xs_in
    ntok = dest_ref.shape[-1] // 2
    base = pl.program_id(0) * ntok

    def issue(t, carry):
        for kq in range(2):
            _row_copy(h_hbm, base + t, xs_out, dest_ref[0, 0, 2 * t + kq], sem).start()
        return carry

    lax.fori_loop(0, ntok, issue, 0)

    def drain(t, carry):
        for kq in range(2):
            _row_copy(h_hbm, base + t, xs_out, dest_ref[0, 0, 2 * t + kq], sem).wait()
        return carry

    lax.fori_loop(0, ntok, drain, 0)


def _dispatch(h2, dest, n_rows, tm):
    t, d = h2.shape
    any_spec = pl.BlockSpec(memory_space=pl.ANY)
    return pl.pallas_call(
        _dispatch_body,
        grid=(t // tm,),
        in_specs=[pl.BlockSpec((1, 1, 2 * tm), lambda i: (i, 0, 0), memory_space=pltpu.SMEM),
                  any_spec, any_spec],
        out_specs=any_spec,
        out_shape=jax.ShapeDtypeStruct((n_rows, d), F32),
        scratch_shapes=[pltpu.SemaphoreType.DMA],
        input_output_aliases={2: 0},
        compiler_params=_params("arbitrary"),
        name="moe_dispatch",
    )(dest.reshape(t // tm, 1, 2 * tm), h2, jnp.zeros((n_rows, d), F32))


def _combine_body(dest_ref, x_ref, info_ref, ys_hbm, o_ref, buf0, buf1, sem):
    ntok = x_ref.shape[0]
    bufs = (buf0, buf1)

    def issue(t, carry):
        for kq in range(2):
            _row_copy(ys_hbm, dest_ref[0, 0, 2 * t + kq], bufs[kq], t, sem).start()
        return carry

    lax.fori_loop(0, ntok, issue, 0)

    def drain(t, carry):
        for kq in range(2):
            _row_copy(ys_hbm, dest_ref[0, 0, 2 * t + kq], bufs[kq], t, sem).wait()
        return carry

    lax.fori_loop(0, ntok, drain, 0)
    info = info_ref[...]
    g0 = info[:, INFO_G0:INFO_G0 + 1]
    g1 = info[:, INFO_G1:INFO_G1 + 1]
    o_ref[...] = x_ref[...] + g0 * buf0[...] + g1 * buf1[...]


def _combine(xn, info, ys, dest, tm):
    t, d = xn.shape
    row = lambda wd: pl.BlockSpec((tm, wd), lambda i: (i, 0))
    return pl.pallas_call(
        _combine_body,
        grid=(t // tm,),
        in_specs=[pl.BlockSpec((1, 1, 2 * tm), lambda i: (i, 0, 0), memory_space=pltpu.SMEM),
                  row(d), row(LANES), pl.BlockSpec(memory_space=pl.ANY)],
        out_specs=row(d),
        out_shape=jax.ShapeDtypeStruct((t, d), F32),
        scratch_shapes=[pltpu.VMEM((tm, d), F32), pltpu.VMEM((tm, d), F32),
                        pltpu.SemaphoreType.DMA],
        compiler_params=_params("arbitrary"),
        name="moe_combine",
    )(dest.reshape(t // tm, 1, 2 * tm), xn, info, ys)


def _seg_matrix(n):
    i = jnp.arange(n) // HEAD_DIM
    return (i[:, None] == i[None, :]).astype(BF16)


def _rwkv_consts():
    c = RWKV_CHUNK
    t = jnp.arange(c)
    tri = (t[:, None] >= t[None, :]).astype(BF16)
    i = jnp.arange(2 * c)
    same = (i[:, None] // c) == (i[None, :] // c)
    ti, tj = (i % c)[:, None], (i % c)[None, :]
    strict = same & (ti > tj)
    incl = same & (ti >= tj)
    blk8 = (i[:, None] // 8) == (i[None, :] // 8)
    eye = i[:, None] == i[None, :]
    cm = jnp.stack([strict, incl, blk8, eye]).astype(F32)
    return _seg_matrix(RWKV_DIM), tri, cm


def _rope_tables(s):
    half = HEAD_DIM // 2
    inv_freq = ROPE_THETA ** (-jnp.arange(half, dtype=F32) * 2.0 / HEAD_DIM)
    ang = jnp.arange(s, dtype=F32)[:, None] * inv_freq[None, :]
    cos, sin = jnp.cos(ang), jnp.sin(ang)
    reps = ATTN_DIM // HEAD_DIM
    cos_t = jnp.tile(jnp.concatenate([cos, cos], axis=1), (1, reps))
    sin_t = jnp.tile(jnp.concatenate([-sin, sin], axis=1), (1, reps))
    return cos_t, sin_t


def _row_tile(n, want):
    return want if n % want == 0 else n


def kernel(x, mix_norm, w_in, tshift_mu, vres_w_in, vres_mu, vres_v0, vres_v2, decay_w0, decay_w2, iclr_a0, iclr_a2, gate_g2, k_k, k_a, r_k, ln_x_w, ln_x_b, conv_w, conv_out_norm, q_norm, k_norm, attn_out_norm, w_o, ffn_norm, dense_wg, dense_wu, dense_wd, router, moe_wg, moe_wu, moe_wd):
    b, s, d = x.shape
    assert d == D_MODEL and s % DIL_SPAN == 0
    t = b * s
    depth = w_in.shape[0]
    tm = _row_tile(s, 512)
    seg384 = _seg_matrix(RWKV_DIM)
    seg256 = _seg_matrix(CONV_DIM)
    rwkv_consts = _rwkv_consts()
    cos_t, sin_t = _rope_tables(s)
    zeros_r = jnp.zeros((RWKV_DIM,), F32)

    x2d = x.reshape(t, d)
    v_first = None
    for l in range(depth):
        w_comb = w_in[l]
        widths = [RWKV_IN, CONV_IN, ATTN_IN]
        if l > 0:
            w_comb = jnp.concatenate(
                [w_comb, jnp.pad(vres_w_in[l - 1], ((0, 0), (0, VRES_PAD - VRES_LORA)))], axis=1)
            widths.append(VRES_PAD)
        outs = _inproj(x2d, mix_norm[l][None, :], w_comb.astype(BF16), widths, tm)
        p_rwkv, p_conv, p_attn = (o.reshape(b, s, -1) for o in outs[:3])

        wl = jnp.zeros((LORA_W, 3 * RWKV_DIM), F32)
        wl = wl.at[0:32, 0:RWKV_DIM].set(decay_w2[l])
        wl = wl.at[32:64, RWKV_DIM:2 * RWKV_DIM].set(iclr_a2[l])
        wl = wl.at[64:128, 2 * RWKV_DIM:].set(gate_g2[l]).astype(BF16)
        vecs = jnp.stack([decay_w0[l], iclr_a0[l], k_k[l], k_a[l], r_k[l].reshape(-1), ln_x_w[l],
                          ln_x_b[l], vres_v0[l - 1] if l > 0 else zeros_r])
        mu = tshift_mu[l][None, :]
        if l == 0:
            y_rwkv, v_first = _rwkv(p_rwkv, mu, wl, vecs, rwkv_consts)
        else:
            vmu = jnp.pad(vres_mu[l - 1], (0, VRES_PAD - VRES_LORA))[None, :]
            v2 = jnp.pad(vres_v2[l - 1], ((0, VRES_PAD - VRES_LORA), (0, 0))).astype(BF16)
            (y_rwkv,) = _rwkv(p_rwkv, mu, wl, vecs, rwkv_consts,
                              vres=(outs[3].reshape(b, s, VRES_PAD), v_first, vmu, v2))

        y_conv = _short_conv(p_conv, conv_w[l], conv_out_norm[l].reshape(1, -1), seg256, tm)

        reps = ATTN_DIM // HEAD_DIM
        q, k, v = _attn_prep(p_attn, cos_t, sin_t, jnp.tile(q_norm[l], reps)[None, :],
                             jnp.tile(k_norm[l], reps)[None, :], seg384, tm)
        res = [_attn(q, k, v, dil) for dil in DILATIONS]
        y_attn = _attn_combine([r[0] for r in res], [r[1] for r in res],
                               attn_out_norm[l].reshape(1, -1), seg384, tm)

        flat = lambda a: a.reshape(t, -1)
        wo = w_o[l].astype(BF16)
        gain = ffn_norm[l][None, :]
        i = l // 2
        if l % 2 == 0:
            xn, h2 = _outproj(x2d, flat(y_rwkv), flat(y_conv), flat(y_attn), wo, gain, tm)
            tmf = _row_tile(t, 1024)
            n_tiles = t // tmf
            x2d = _ffn(h2, dense_wg[i][None].astype(BF16), dense_wu[i][None].astype(BF16),
                       dense_wd[i][None].astype(BF16), jnp.zeros((n_tiles,), jnp.int32),
                       jnp.full((1,), n_tiles, jnp.int32), tmf, 512, res=xn)
        else:
            rw = jnp.pad(router[i], ((0, 0), (0, LANES - N_EXPERTS)))
            rw_hi = rw.astype(BF16)
            rw_lo = (rw - rw_hi.astype(F32)).astype(BF16)
            rt = jnp.arange(tm)
            tri = (rt[:, None] > rt[None, :]).astype(BF16)
            xn, h2, info, cnt = _outproj(x2d, flat(y_rwkv), flat(y_conv), flat(y_attn), wo, gain,
                                         tm, router=(rw_hi, rw_lo, tri))
            tme = 512
            counts = cnt[0, :N_EXPERTS].astype(jnp.int32)
            padded = (counts + tme - 1) // tme * tme
            ends = jnp.cumsum(padded)
            pstart = ends - padded
            n_rows = 2 * t + N_EXPERTS * tme
            n_tiles = n_rows // tme
            tile_expert = jnp.minimum(
                jnp.searchsorted(ends, jnp.arange(n_tiles) * tme, side='right'),
                N_EXPERTS - 1).astype(jnp.int32)
            n_used = (ends[-1] // tme).astype(jnp.int32).reshape(1)
            e01 = info[:, INFO_E0:INFO_E1 + 1].astype(jnp.int32)
            dest = pstart[e01] + info[:, INFO_R0:INFO_R1 + 1].astype(jnp.int32)
            xs = _dispatch(h2, dest, n_rows, tm)
            ys = _ffn(xs, moe_wg[i].astype(BF16), moe_wu[i].astype(BF16), moe_wd[i].astype(BF16),
                      tile_expert, n_used, tme, 512)
            x2d = _combine(xn, info, ys, dest, _row_tile(s, 256))
    return x2d.reshape(b, s, d)
```

```python
import functools

import jax
import jax.numpy as jnp
from jax import lax
from jax.experimental import pallas as pl
from jax.experimental.pallas import tpu as pltpu

F32 = jnp.float32
BF16 = jnp.bfloat16

D_MODEL = 1024
HEAD_DIM = 64
RWKV_DIM = 384
CONV_DIM = 256
ATTN_DIM = 384
LORA_W = 128
RWKV_IN = 3 * RWKV_DIM + LORA_W
CONV_IN = 3 * CONV_DIM
ATTN_IN = 3 * ATTN_DIM
VRES_LORA = 16
VRES_PAD = 128
CONV_WIDTH = 3
DILATIONS = (1, 4, 16)
ATTN_BLOCK = 128
DIL_SPAN = ATTN_BLOCK * 16
ROPE_THETA = 10000.0
D_FF = 3584
N_EXPERTS = 8
NORM_EPS = 1e-6
GN_EPS = 64e-5

LANES = 128
RWKV_CHUNK = 64
PAIR = 2 * HEAD_DIM
VMEM_LIMIT = 56 * 1024 * 1024


def _params(*sem):
    return pltpu.CompilerParams(dimension_semantics=sem, vmem_limit_bytes=VMEM_LIMIT)


def _bdot(a, b):
    return jnp.dot(a.astype(BF16), b.astype(BF16), preferred_element_type=F32)


def _bdot_nt(a, b):
    return lax.dot_general(a.astype(BF16), b.astype(BF16), (((1,), (1,)), ((), ())),
                           preferred_element_type=F32)


def _shift_rows(x, carry_row, n=1):
    row = lax.broadcasted_iota(jnp.int32, (x.shape[0], 1), 0)
    out = pltpu.roll(x, n, 0)
    for i in range(n):
        out = jnp.where(row == i, carry_row[i:i + 1, :], out)
    return out


def _inproj_body(x_ref, g_ref, w_ref, *o_refs):
    x = x_ref[...]
    h = x * lax.rsqrt(jnp.mean(x * x, axis=-1, keepdims=True) + NORM_EPS) * g_ref[...]
    p = jnp.dot(h.astype(BF16), w_ref[...], preferred_element_type=F32)
    off = 0
    for o_ref in o_refs:
        n = o_ref.shape[-1]
        o_ref[...] = p[:, off:off + n].astype(o_ref.dtype)
        off += n


def _inproj(x2d, gain, w, widths, tm):
    t, d = x2d.shape
    n = w.shape[1]
    return pl.pallas_call(
        _inproj_body,
        grid=(t // tm,),
        in_specs=[pl.BlockSpec((tm, d), lambda i: (i, 0)),
                  pl.BlockSpec((1, d), lambda i: (0, 0)),
                  pl.BlockSpec((d, n), lambda i: (0, 0))],
        out_specs=[pl.BlockSpec((tm, wd), lambda i: (i, 0)) for wd in widths],
        out_shape=[jax.ShapeDtypeStruct((t, wd), F32) for wd in widths],
        compiler_params=_params("arbitrary"),
        name="inproj",
    )(x2d, gain, w)


def _rwkv_body(*refs, has_vres):
    if has_vres:
        (p_ref, pv_ref, vf_ref, mu_ref, vmu_ref, wl_ref, v2_ref, vec_ref, seg_ref, tri_ref,
         cm_ref, y_ref, prev_ref, pprev_ref, h_ref) = refs
    else:
        (p_ref, mu_ref, wl_ref, vec_ref, seg_ref, tri_ref,
         cm_ref, y_ref, vout_ref, prev_ref, h_ref) = refs
    @pl.when(pl.program_id(0) == 0)
    def _():
        prev_ref[...] = jnp.zeros_like(prev_ref)
        h_ref[...] = jnp.zeros_like(h_ref)
        if has_vres:
            pprev_ref[...] = jnp.zeros_like(pprev_ref)

    nb, nt = p_ref.shape[0], p_ref.shape[1]

    def token_shift(src_ref, carry_ref, mix):
        out = []
        for bi in range(nb):
            cur = src_ref[bi]
            prev = _shift_rows(cur, carry_ref[bi:bi + 1, :])
            carry_ref[bi:bi + 1, :] = cur[nt - 1:nt, :]
            out.append(cur + (prev - cur) * mix)
        return jnp.concatenate(out, axis=0)

    x = token_shift(p_ref, prev_ref, mu_ref[...])
    r = x[:, 0:RWKV_DIM]
    k = x[:, RWKV_DIM:2 * RWKV_DIM]
    v = x[:, 2 * RWKV_DIM:3 * RWKV_DIM]
    z = x[:, 3 * RWKV_DIM:]
    lane = lax.broadcasted_iota(jnp.int32, (1, LANES), 1)
    zt = jnp.where(lane < 32, jnp.tanh(z), jnp.where(lane < 64, z, jax.nn.sigmoid(z)))
    lo = _bdot(zt, wl_ref[...])
    w0, a0, k_k, k_a, r_k, ln_w, ln_b, v0 = (vec_ref[i:i + 1, :] for i in range(8))
    seg = seg_ref[...]

    zz = -(w0 + lo[:, 0:RWKV_DIM])
    softplus = jnp.maximum(zz, 0.0) + jnp.log(1.0 + jnp.exp(-jnp.abs(zz)))
    lw = -jnp.exp(-softplus - 0.5)
    a = jax.nn.sigmoid(a0 + lo[:, RWKV_DIM:2 * RWKV_DIM])
    g = lo[:, 2 * RWKV_DIM:]
    if has_vres:
        xv = token_shift(pv_ref, pprev_ref, vmu_ref[...])
        v_first = vf_ref[...].reshape(nb * nt, RWKV_DIM)
        v = v + (v_first - v) * jax.nn.sigmoid(v0 + _bdot(xv, v2_ref[...]))
    else:
        vout_ref[...] = v.reshape(nb, nt, RWKV_DIM)
    kk = k * k_k
    kk = kk / jnp.maximum(jnp.sqrt(_bdot(kk * kk, seg)), 1e-12)
    kmod = k * (1.0 + (a - 1.0) * k_a)
    bonus = _bdot(r * kmod * r_k, seg) * v

    tri = tri_ref[...]
    lw_hi = lw.astype(BF16)
    lw_lo = (lw - lw_hi.astype(F32)).astype(BF16)
    cum = (jnp.dot(tri, lw_hi, preferred_element_type=F32)
           + jnp.dot(tri, lw_lo, preferred_element_type=F32))
    pinv = jnp.exp(-cum)
    r_t = r * jnp.exp(cum)
    a_t = -kk * jnp.exp(cum - lw)
    b_t = kk * a * pinv
    k_t = kmod * pinv
    p_ends = [jnp.exp(cum[bi * nt + nt - 1:(bi + 1) * nt, :]) for bi in range(nb)]
    p_end_rows = jnp.concatenate([jnp.broadcast_to(pe, (nt, RWKV_DIM)) for pe in p_ends], axis=0)
    bh_t = b_t * p_end_rows
    kh_t = k_t * p_end_rows

    strict, incl, blk8, eye = (cm_ref[i] for i in range(4))
    m0 = (lane < HEAD_DIM).astype(F32)
    m1 = 1.0 - m0

    def stack(t, rows, sl):
        ts = t[rows, sl]
        return jnp.concatenate([ts * m0, ts * m1], axis=0)

    npair = RWKV_DIM // PAIR
    chains = [(bi, j) for bi in range(nb) for j in range(npair)]
    each = lambda fn, *cols: [fn(*args) for args in zip(*cols)]
    cat0 = lambda *ts: jnp.concatenate(ts, axis=0)
    cat1 = lambda *ts: jnp.concatenate(ts, axis=1)

    def stacks(t):
        return [stack(t, slice(bi * nt, (bi + 1) * nt), slice(j * PAIR, (j + 1) * PAIR))
                for bi, j in chains]

    a_st, r_st, b_st, k_st, v_st, bh_st, kh_st = (stacks(t) for t in (a_t, r_t, b_t, k_t, v, bh_t, kh_t))
    gram = each(lambda a_, r_, b_, k_: _bdot_nt(cat0(a_, r_), cat0(b_, k_)), a_st, r_st, b_st, k_st)
    n2 = 2 * nt
    a_ab = [gm[:n2, :n2] * strict for gm in gram]
    a_ak = [gm[:n2, n2:] * strict for gm in gram]
    a_rb = [gm[n2:, :n2] * incl for gm in gram]
    a_rk = [gm[n2:, n2:] * incl for gm in gram]
    dg = [m * blk8 for m in a_ab]
    off = each(lambda m, d_: m - d_, a_ab, dg)
    d2 = each(_bdot, dg, dg)
    tdiag = [eye + d_ for d_ in dg]
    tdiag = each(lambda t_, d_: t_ + _bdot(t_, d_), tdiag, d2)
    d4 = each(_bdot, d2, d2)
    tdiag = each(lambda t_, d_: t_ + _bdot(t_, d_), tdiag, d4)
    e1 = each(_bdot, tdiag, off)
    e2 = each(_bdot, e1, e1)
    yb = [eye + e_ for e_ in e1]
    yb = each(lambda y_, e_: y_ + _bdot(y_, e_), yb, e2)
    e4 = each(_bdot, e2, e2)
    yb = each(lambda y_, e_: y_ + _bdot(y_, e_), yb, e4)
    tinv = each(_bdot, yb, tdiag)

    hs = [h_ref[bi, j] for bi, j in chains]
    xs = each(lambda a_, ak_, h_, v_: _bdot(cat1(a_, ak_), cat0(h_, v_)), a_st, a_ak, hs, v_st)
    us = each(_bdot, tinv, xs)
    y_st = each(lambda r_, rb_, rk_, h_, u_, v_: _bdot(cat1(r_, rb_, rk_), cat0(h_, u_, v_)),
                r_st, a_rb, a_rk, hs, us, v_st)
    pe_col = [jnp.sum(eye * p_ends[bi][:, j * PAIR:(j + 1) * PAIR], axis=1, keepdims=True)
              for bi, j in chains]
    h_new = each(lambda pc_, h_, bh_, kh_, u_, v_: pc_ * h_ + _bdot(cat0(bh_, kh_).T, cat0(u_, v_)),
                 pe_col, hs, bh_st, kh_st, us, v_st)
    for (bi, j), hn in zip(chains, h_new):
        h_ref[bi, j] = hn
    y_pair = [ys_[:nt] + ys_[nt:] for ys_ in y_st]
    y = cat0(*[cat1(*y_pair[bi * npair:(bi + 1) * npair]) for bi in range(nb)])
    inv_n = 1.0 / HEAD_DIM
    mean = _bdot(y, seg) * inv_n
    dy = y - mean
    var = _bdot(dy * dy, seg) * inv_n
    yn = dy * lax.rsqrt(var + GN_EPS) * ln_w + ln_b
    y_ref[...] = ((yn + bonus) * g).reshape(nb, nt, RWKV_DIM).astype(y_ref.dtype)


def _rwkv(p_rwkv, mu, wl, vecs, consts, vres=None):
    b, s, _ = p_rwkv.shape
    c = RWKV_CHUNK
    seg, tri, cm = consts
    tile = lambda wd: pl.BlockSpec((b, c, wd), lambda j: (0, j, 0))
    full = lambda arr: pl.BlockSpec(arr.shape, lambda j: (0,) * arr.ndim)
    y_shape = jax.ShapeDtypeStruct((b, s, RWKV_DIM), BF16)
    scratch = [pltpu.VMEM((b, RWKV_IN), F32)]
    if vres is None:
        ins = [p_rwkv, mu, wl, vecs, seg, tri, cm]
        in_specs = [tile(RWKV_IN)] + [full(t) for t in ins[1:]]
        out_shape = [y_shape, jax.ShapeDtypeStruct((b, s, RWKV_DIM), F32)]
        out_specs = [tile(RWKV_DIM), tile(RWKV_DIM)]
    else:
        p_vres, v_first, vmu, v2 = vres
        ins = [p_rwkv, p_vres, v_first, mu, vmu, wl, v2, vecs, seg, tri, cm]
        in_specs = [tile(RWKV_IN), tile(VRES_PAD), tile(RWKV_DIM)] + [full(t) for t in ins[3:]]
        out_shape = [y_shape]
        out_specs = [tile(RWKV_DIM)]
        scratch.append(pltpu.VMEM((b, VRES_PAD), F32))
    scratch.append(pltpu.VMEM((b, RWKV_DIM // PAIR, PAIR, PAIR), F32))
    return pl.pallas_call(
        functools.partial(_rwkv_body, has_vres=vres is not None),
        grid=(s // c,),
        in_specs=in_specs, out_specs=out_specs, out_shape=out_shape,
        scratch_shapes=scratch,
        compiler_params=_params("arbitrary"),
        name="rwkv7",
    )(*ins)


def _conv_body(p_ref, w_ref, gain_ref, seg_ref, o_ref, carry_ref):
    @pl.when(pl.program_id(1) == 0)
    def _():
        carry_ref[...] = jnp.zeros_like(carry_ref)

    p = p_ref[...]
    nt = p.shape[0]
    u = p[:, CONV_DIM:2 * CONV_DIM] * p[:, 2 * CONV_DIM:]
    carry = carry_ref[...]
    u1 = _shift_rows(u, carry[1:2, :], 1)
    u2 = _shift_rows(u, carry, 2)
    carry_ref[...] = u[nt - 2:nt, :]
    w = w_ref[...]
    y = p[:, 0:CONV_DIM] * (w[0:1, :] * u2 + w[1:2, :] * u1 + w[2:3, :] * u)
    ms = _bdot(y * y, seg_ref[...]) * (1.0 / HEAD_DIM)
    o_ref[...] = (y * lax.rsqrt(ms + NORM_EPS) * gain_ref[...]).astype(o_ref.dtype)


def _short_conv(p_conv, w, gain, seg, tm):
    b, s, _ = p_conv.shape
    return pl.pallas_call(
        _conv_body,
        grid=(b, s // tm),
        in_specs=[pl.BlockSpec((None, tm, CONV_IN), lambda i, j: (i, j, 0)),
                  pl.BlockSpec(w.shape, lambda i, j: (0, 0)),
                  pl.BlockSpec(gain.shape, lambda i, j: (0, 0)),
                  pl.BlockSpec(seg.shape, lambda i, j: (0, 0))],
        out_specs=pl.BlockSpec((None, tm, CONV_DIM), lambda i, j: (i, j, 0)),
        out_shape=jax.ShapeDtypeStruct((b, s, CONV_DIM), BF16),
        scratch_shapes=[pltpu.VMEM((CONV_WIDTH - 1, CONV_DIM), F32)],
        compiler_params=_params("arbitrary", "arbitrary"),
        name="short_conv",
    )(p_conv, w, gain, seg)


def _attn_prep_body(p_ref, cos_ref, sin_ref, qg_ref, kg_ref, seg_ref, q_ref, k_ref, v_ref):
    p = p_ref[...]
    seg = seg_ref[...]
    lane = lax.broadcasted_iota(jnp.int32, (1, LANES), 1)
    first = (lane & (HEAD_DIM - 1)) < HEAD_DIM // 2
    for part, g_ref, o_ref, scale in ((0, qg_ref, q_ref, HEAD_DIM ** -0.5), (1, kg_ref, k_ref, 1.0)):
        x = p[:, part * ATTN_DIM:(part + 1) * ATTN_DIM]
        ms = _bdot(x * x, seg) * (1.0 / HEAD_DIM)
        xn = x * lax.rsqrt(ms + NORM_EPS) * g_ref[...]
        rot = []
        for j in range(ATTN_DIM // LANES):
            xs = xn[:, j * LANES:(j + 1) * LANES]
            rot.append(jnp.where(first, pltpu.roll(xs, LANES - HEAD_DIM // 2, 1),
                                 pltpu.roll(xs, HEAD_DIM // 2, 1)))
        out = xn * cos_ref[...] + jnp.concatenate(rot, axis=1) * sin_ref[...]
        o_ref[...] = (out * scale).astype(o_ref.dtype)
    v_ref[...] = p[:, 2 * ATTN_DIM:].astype(v_ref.dtype)


def _attn_prep(p_attn, cos, sin, qg, kg, seg, tm):
    b, s, _ = p_attn.shape
    tile = lambda wd: pl.BlockSpec((None, tm, wd), lambda i, j: (i, j, 0))
    tab = pl.BlockSpec((tm, ATTN_DIM), lambda i, j: (j, 0))
    full = lambda arr: pl.BlockSpec(arr.shape, lambda i, j: (0, 0))
    shp = jax.ShapeDtypeStruct((b, s, ATTN_DIM), BF16)
    return pl.pallas_call(
        _attn_prep_body,
        grid=(b, s // tm),
        in_specs=[tile(ATTN_IN), tab, tab, full(qg), full(kg), full(seg)],
        out_specs=[tile(ATTN_DIM)] * 3,
        out_shape=[shp] * 3,
        compiler_params=_params("arbitrary", "arbitrary"),
        name="attn_prep",
    )(p_attn, cos, sin, qg, kg, seg)


def _attn_body(q_ref, k_ref, v_ref, kprev_ref, vprev_ref, o_ref, l_ref):
    mb = pl.program_id(2)
    q = q_ref[...]
    kcat = jnp.concatenate([kprev_ref[...], k_ref[...]], axis=0)
    vcat = jnp.concatenate([vprev_ref[...], v_ref[...]], axis=0)
    blk = q.shape[0]
    qi = lax.broadcasted_iota(jnp.int32, (blk, 2 * blk), 0)
    kj = lax.broadcasted_iota(jnp.int32, (blk, 2 * blk), 1)
    dist = qi + blk - kj
    mask = (dist >= 0) & (dist <= ATTN_BLOCK) & (kj + mb * (2 * blk) >= blk)
    lane = lax.broadcasted_iota(jnp.int32, (1, LANES), 1)
    head0 = lane < HEAD_DIM
    m0 = head0.astype(F32)
    npair = ATTN_DIM // PAIR
    sls = [slice(j * PAIR, (j + 1) * PAIR) for j in range(npair)]
    heads = [(sl, hm) for sl in sls for hm in (m0, 1.0 - m0)]
    s_all = [_bdot_nt(q[:, sl].astype(F32) * hm, kcat[:, sl]) for sl, hm in heads]
    s_all = [jnp.where(mask, s, -jnp.inf) for s in s_all]
    m_all = [jnp.max(s, axis=-1, keepdims=True) for s in s_all]
    pe_all = [jnp.exp(s - m) for s, m in zip(s_all, m_all)]
    den_all = [jnp.sum(pe, axis=-1, keepdims=True) for pe in pe_all]
    o_all = [_bdot(pe, vcat[:, sl]) / den for pe, den, (sl, _) in zip(pe_all, den_all, heads)]
    lse_all = [m + jnp.log(den) for m, den in zip(m_all, den_all)]
    for j, sl in enumerate(sls):
        o_ref[:, sl] = jnp.where(head0, o_all[2 * j], o_all[2 * j + 1])
        l_ref[:, sl] = jnp.where(head0, lse_all[2 * j], lse_all[2 * j + 1])


def _attn(q, k, v, dil):
    b, s, _ = q.shape
    ld = s // dil
    view = lambda t: t.reshape(b, ld, dil * ATTN_DIM)
    blk = (None, ATTN_BLOCK, ATTN_DIM)
    spec = pl.BlockSpec(blk, lambda i, r, m: (i, m, r))
    prev = pl.BlockSpec(blk, lambda i, r, m: (i, jnp.maximum(m - 1, 0), r))
    shp = jax.ShapeDtypeStruct((b, ld, dil * ATTN_DIM), F32)
    o, lse = pl.pallas_call(
        _attn_body,
        grid=(b, dil, ld // ATTN_BLOCK),
        in_specs=[spec, spec, spec, prev, prev],
        out_specs=[spec, spec],
        out_shape=[shp, shp],
        compiler_params=_params("arbitrary", "arbitrary", "arbitrary"),
        name=f"attn_d{dil}",
    )(view(q), view(k), view(v), view(k), view(v))
    return o.reshape(b, s, ATTN_DIM), lse.reshape(b, s, ATTN_DIM)


def _attn_combine_body(*refs):
    n = len(DILATIONS)
    o_refs, l_refs = refs[:n], refs[n:2 * n]
    gain_ref, seg_ref, y_ref = refs[2 * n:]
    ls = [r[...] for r in l_refs]
    m = functools.reduce(jnp.maximum, ls)
    es = [jnp.exp(l - m) for l in ls]
    den = functools.reduce(jnp.add, es)
    o = functools.reduce(jnp.add, [e * r[...] for e, r in zip(es, o_refs)]) / den
    ms = _bdot(o * o, seg_ref[...]) * (1.0 / HEAD_DIM)
    y_ref[...] = (o * lax.rsqrt(ms + NORM_EPS) * gain_ref[...]).astype(y_ref.dtype)


def _attn_combine(os_, ls_, gain, seg, tm):
    b, s, _ = os_[0].shape
    tile = pl.BlockSpec((None, tm, ATTN_DIM), lambda i, j: (i, j, 0))
    full = lambda arr: pl.BlockSpec(arr.shape, lambda i, j: (0, 0))
    return pl.pallas_call(
        _attn_combine_body,
        grid=(b, s // tm),
        in_specs=[tile] * (2 * len(os_)) + [full(gain), full(seg)],
        out_specs=tile,
        out_shape=jax.ShapeDtypeStruct((b, s, ATTN_DIM), BF16),
        compiler_params=_params("arbitrary", "arbitrary"),
        name="attn_combine",
    )(*os_, *ls_, gain, seg)


INFO_E0, INFO_E1, INFO_R0, INFO_R1, INFO_G0, INFO_G1 = range(6)


def _outproj_body(*refs, has_router):
    if has_router:
        (x_ref, yr_ref, yc_ref, ya_ref, wo_ref, g_ref, rwh_ref, rwl_ref, tri_ref,
         xn_ref, h_ref, info_ref, cnt_ref, run_ref) = refs
    else:
        x_ref, yr_ref, yc_ref, ya_ref, wo_ref, g_ref, xn_ref, h_ref = refs
    o1, o2 = RWKV_DIM, RWKV_DIM + CONV_DIM
    acc = jnp.dot(yr_ref[...], wo_ref[0:o1, :], preferred_element_type=F32)
    acc += jnp.dot(yc_ref[...], wo_ref[o1:o2, :], preferred_element_type=F32)
    acc += jnp.dot(ya_ref[...], wo_ref[o2:, :], preferred_element_type=F32)
    xn = x_ref[...] + acc
    xn_ref[...] = xn
    h = xn * lax.rsqrt(jnp.mean(xn * xn, axis=-1, keepdims=True) + NORM_EPS) * g_ref[...]
    h_ref[...] = h.astype(h_ref.dtype)
    if not has_router:
        return

    @pl.when(pl.program_id(0) == 0)
    def _():
        run_ref[...] = jnp.zeros_like(run_ref)

    h_hi = h.astype(BF16)
    h_lo = (h - h_hi.astype(F32)).astype(BF16)
    rwh = rwh_ref[...]
    logits = (jnp.dot(h_hi, rwh, preferred_element_type=F32)
              + jnp.dot(h_lo, rwh, preferred_element_type=F32)
              + jnp.dot(h_hi, rwl_ref[...], preferred_element_type=F32))
    lane = lax.broadcasted_iota(jnp.int32, (1, LANES), 1)
    logits = jnp.where(lane < N_EXPERTS, logits, -jnp.inf)

    def top(lg):
        m = jnp.max(lg, axis=-1, keepdims=True)
        idx = jnp.min(jnp.where(lg == m, lane, LANES), axis=-1, keepdims=True)
        return m, idx, lane == idx

    m1, i1, oh1 = top(logits)
    m2, i2, oh2 = top(jnp.where(oh1, -jnp.inf, logits))
    zexp = jnp.exp(m2 - m1)
    g0 = 1.0 / (1.0 + zexp)
    g1 = zexp * g0
    oh = oh1.astype(F32) + oh2.astype(F32)
    tot = jnp.dot(tri_ref[...], oh.astype(BF16), preferred_element_type=F32) + run_ref[...]
    r0 = jnp.sum(jnp.where(oh1, tot, 0.0), axis=-1, keepdims=True)
    r1 = jnp.sum(jnp.where(oh2, tot, 0.0), axis=-1, keepdims=True)
    run_ref[...] += jnp.sum(oh, axis=0, keepdims=True)
    cnt_ref[...] = run_ref[...]
    info = jnp.zeros(logits.shape, F32)
    for ln, val in ((INFO_E0, i1.astype(F32)), (INFO_E1, i2.astype(F32)), (INFO_R0, r0),
                    (INFO_R1, r1), (INFO_G0, g0), (INFO_G1, g1)):
        info = jnp.where(lane == ln, val, info)
    info_ref[...] = info


def _outproj(x2d, yr, yc, ya, wo, gain, tm, router=None):
    t, d = x2d.shape
    row = lambda wd: pl.BlockSpec((tm, wd), lambda i: (i, 0))
    full = lambda arr: pl.BlockSpec(arr.shape, lambda i: (0, 0))
    ins = [x2d, yr, yc, ya, wo, gain]
    in_specs = [row(d), row(RWKV_DIM), row(CONV_DIM), row(ATTN_DIM), full(wo), full(gain)]
    out_shape = [jax.ShapeDtypeStruct((t, d), F32)]
    out_specs = [row(d)]
    scratch = []
    if router is None:
        out_shape.append(jax.ShapeDtypeStruct((t, d), BF16))
        out_specs.append(row(d))
    else:
        ins += list(router)
        in_specs += [full(a) for a in router]
        out_shape += [jax.ShapeDtypeStruct((t, d), F32), jax.ShapeDtypeStruct((t, LANES), F32),
                      jax.ShapeDtypeStruct((1, LANES), F32)]
        out_specs += [row(d), row(LANES), pl.BlockSpec((1, LANES), lambda i: (0, 0))]
        scratch = [pltpu.VMEM((1, LANES), F32)]
    return pl.pallas_call(
        functools.partial(_outproj_body, has_router=router is not None),
        grid=(t // tm,),
        in_specs=in_specs, out_specs=out_specs, out_shape=out_shape,
        scratch_shapes=scratch,
        compiler_params=_params("arbitrary"),
        name="outproj",
    )(*ins)


def _ffn_body(te_ref, nu_ref, x_ref, wg_ref, wu_ref, wd_ref, *rest, has_res):
    if has_res:
        res_ref, o_ref, acc_ref = rest
    else:
        o_ref, acc_ref = rest
    i, f = pl.program_id(0), pl.program_id(1)

    @pl.when(f == 0)
    def _():
        acc_ref[...] = jnp.zeros_like(acc_ref)

    @pl.when(i < nu_ref[0])
    def _():
        x = x_ref[...].astype(BF16)
        hg = jnp.dot(x, wg_ref[...], preferred_element_type=F32)
        hu = jnp.dot(x, wu_ref[...], preferred_element_type=F32)
        act = hg * jax.nn.sigmoid(hg) * hu
        acc_ref[...] += jnp.dot(act.astype(BF16), wd_ref[...], preferred_element_type=F32)

    @pl.when(f == pl.num_programs(1) - 1)
    def _():
        out = acc_ref[...]
        if has_res:
            out = out + res_ref[...]
        o_ref[...] = out.astype(o_ref.dtype)


def _ffn(x, wg, wu, wd, tile_expert, n_used, tm, tf, res=None):
    n, d = x.shape
    ff = wg.shape[-1]
    row = pl.BlockSpec((tm, d), lambda i, f, te, nu: (i, 0))
    ins = [x, wg, wu, wd]
    in_specs = [row,
                pl.BlockSpec((None, d, tf), lambda i, f, te, nu: (te[i], 0, f)),
                pl.BlockSpec((None, d, tf), lambda i, f, te, nu: (te[i], 0, f)),
                pl.BlockSpec((None, tf, d), lambda i, f, te, nu: (te[i], f, 0))]
    if res is not None:
        ins.append(res)
        in_specs.append(row)
    return pl.pallas_call(
        functools.partial(_ffn_body, has_res=res is not None),
        grid_spec=pltpu.PrefetchScalarGridSpec(
            num_scalar_prefetch=2,
            grid=(n // tm, ff // tf),
            in_specs=in_specs,
            out_specs=row,
            scratch_shapes=[pltpu.VMEM((tm, d), F32)]),
        out_shape=jax.ShapeDtypeStruct((n, d), F32),
        compiler_params=_params("arbitrary", "arbitrary"),
        name="swiglu",
    )(tile_expert, n_used, *ins)


def _row_copy(src, s, dst, d, sem):
    return pltpu.make_async_copy(src.at[pl.ds(s, 1)], dst.at[pl.ds(d, 1)], sem)


def _moe_ffn_body(te_ref, nu_ref, tok_ref, tokn_ref, h_hbm, wg_ref, wu_ref, wd_ref, o_ref,
                  acc_ref, xbuf, sem):
    i, f = pl.program_id(0), pl.program_id(1)
    n_used = nu_ref[0]
    tm = xbuf.shape[1]
    slot = lax.rem(i, 2)

    def tile_copy(sl):
        return pltpu.make_async_copy(h_hbm.at[pl.ds(0, tm)], xbuf.at[sl], sem.at[sl])

    def gather(idx_ref, sl):
        def issue(r, carry):
            pltpu.make_async_copy(h_hbm.at[pl.ds(idx_ref[0, 0, r], 1)],
                                  xbuf.at[sl, pl.ds(r, 1)], sem.at[sl]).start()
            return carry
        lax.fori_loop(0, tm, issue, 0, unroll=8)

    @pl.when(f == 0)
    def _():
        acc_ref[...] = jnp.zeros_like(acc_ref)

    @pl.when((f == 0) & (i == 0) & (n_used > 0))
    def _():
        gather(tok_ref, 0)

    @pl.when((f == 0) & (i < n_used))
    def _():
        tile_copy(slot).wait()

    @pl.when((f == 0) & (i + 1 < n_used))
    def _():
        gather(tokn_ref, 1 - slot)

    @pl.when(i < n_used)
    def _():
        x = xbuf[slot].astype(BF16)
        hg = jnp.dot(x, wg_ref[...], preferred_element_type=F32)
        hu = jnp.dot(x, wu_ref[...], preferred_element_type=F32)
        act = hg * jax.nn.sigmoid(hg) * hu
        acc_ref[...] += jnp.dot(act.astype(BF16), wd_ref[...], preferred_element_type=F32)

    @pl.when(f == pl.num_programs(1) - 1)
    def _():
        o_ref[...] = acc_ref[...]


def _moe_ffn(h2, row_tok, wg, wu, wd, tile_expert, n_used, tm, tf):
    d = h2.shape[1]
    n = row_tok.shape[0]
    ff = wg.shape[-1]
    n_tiles = n // tm
    idx = row_tok.reshape(n_tiles, 1, tm)
    smem = lambda fn: pl.BlockSpec((1, 1, tm), fn, memory_space=pltpu.SMEM)
    return pl.pallas_call(
        _moe_ffn_body,
        grid_spec=pltpu.PrefetchScalarGridSpec(
            num_scalar_prefetch=2,
            grid=(n_tiles, ff // tf),
            in_specs=[smem(lambda i, f, te, nu: (i, 0, 0)),
                      smem(lambda i, f, te, nu: (jnp.minimum(i + 1, n_tiles - 1), 0, 0)),
                      pl.BlockSpec(memory_space=pl.ANY),
                      pl.BlockSpec((None, d, tf), lambda i, f, te, nu: (te[i], 0, f)),
                      pl.BlockSpec((None, d, tf), lambda i, f, te, nu: (te[i], 0, f)),
                      pl.BlockSpec((None, tf, d), lambda i, f, te, nu: (te[i], f, 0))],
            out_specs=pl.BlockSpec((tm, d), lambda i, f, te, nu: (i, 0)),
            scratch_shapes=[pltpu.VMEM((tm, d), F32), pltpu.VMEM((2, tm, d), F32),
                            pltpu.SemaphoreType.DMA((2,))]),
        out_shape=jax.ShapeDtypeStruct((n, d), F32),
        compiler_params=_params("arbitrary", "arbitrary"),
        name="moe_swiglu",
    )(tile_expert, n_used, idx, idx, h2, wg, wu, wd)


def _combine_body(dest_ref, x_ref, info_ref, ys_hbm, o_ref, buf0, buf1, sem):
    ntok = x_ref.shape[0]
    bufs = (buf0, buf1)

    def issue(t, carry):
        for kq in range(2):
            _row_copy(ys_hbm, dest_ref[0, 0, 2 * t + kq], bufs[kq], t, sem).start(priority=kq)
        return carry

    lax.fori_loop(0, ntok, issue, 0, unroll=8)
    for kq in range(2):
        pltpu.make_async_copy(ys_hbm.at[pl.ds(0, ntok)], bufs[kq], sem).wait()
    info = info_ref[...]
    g0 = info[:, INFO_G0:INFO_G0 + 1]
    g1 = info[:, INFO_G1:INFO_G1 + 1]
    o_ref[...] = x_ref[...] + g0 * buf0[...] + g1 * buf1[...]


def _combine(xn, info, ys, dest, tm):
    t, d = xn.shape
    row = lambda wd: pl.BlockSpec((tm, wd), lambda i: (i, 0))
    return pl.pallas_call(
        _combine_body,
        grid=(t // tm,),
        in_specs=[pl.BlockSpec((1, 1, 2 * tm), lambda i: (i, 0, 0), memory_space=pltpu.SMEM),
                  row(d), row(LANES), pl.BlockSpec(memory_space=pl.ANY)],
        out_specs=row(d),
        out_shape=jax.ShapeDtypeStruct((t, d), F32),
        scratch_shapes=[pltpu.VMEM((tm, d), F32), pltpu.VMEM((tm, d), F32),
                        pltpu.SemaphoreType.DMA],
        compiler_params=_params("arbitrary"),
        name="moe_combine",
    )(dest.reshape(t // tm, 1, 2 * tm), xn, info, ys)


def _seg_matrix(n):
    i = jnp.arange(n) // HEAD_DIM
    return (i[:, None] == i[None, :]).astype(BF16)


def _rwkv_consts(nb):
    c = RWKV_CHUNK
    t = jnp.arange(nb * c)
    tri = ((t[:, None] >= t[None, :]) & (t[:, None] // c == t[None, :] // c)).astype(BF16)
    i = jnp.arange(2 * c)
    same = (i[:, None] // c) == (i[None, :] // c)
    ti, tj = (i % c)[:, None], (i % c)[None, :]
    strict = same & (ti > tj)
    incl = same & (ti >= tj)
    blk8 = (i[:, None] // 8) == (i[None, :] // 8)
    eye = i[:, None] == i[None, :]
    cm = jnp.stack([strict, incl, blk8, eye]).astype(F32)
    return _seg_matrix(RWKV_DIM), tri, cm


def _rope_tables(s):
    half = HEAD_DIM // 2
    inv_freq = ROPE_THETA ** (-jnp.arange(half, dtype=F32) * 2.0 / HEAD_DIM)
    ang = jnp.arange(s, dtype=F32)[:, None] * inv_freq[None, :]
    cos, sin = jnp.cos(ang), jnp.sin(ang)
    reps = ATTN_DIM // HEAD_DIM
    cos_t = jnp.tile(jnp.concatenate([cos, cos], axis=1), (1, reps))
    sin_t = jnp.tile(jnp.concatenate([-sin, sin], axis=1), (1, reps))
    return cos_t, sin_t


def _row_tile(n, want):
    return want if n % want == 0 else n


def kernel(x, mix_norm, w_in, tshift_mu, vres_w_in, vres_mu, vres_v0, vres_v2, decay_w0, decay_w2, iclr_a0, iclr_a2, gate_g2, k_k, k_a, r_k, ln_x_w, ln_x_b, conv_w, conv_out_norm, q_norm, k_norm, attn_out_norm, w_o, ffn_norm, dense_wg, dense_wu, dense_wd, router, moe_wg, moe_wu, moe_wd):
    b, s, d = x.shape
    assert d == D_MODEL and s % DIL_SPAN == 0
    t = b * s
    depth = w_in.shape[0]
    tm = _row_tile(s, 512)
    seg384 = _seg_matrix(RWKV_DIM)
    seg256 = _seg_matrix(CONV_DIM)
    rwkv_consts = _rwkv_consts(b)
    cos_t, sin_t = _rope_tables(s)
    zeros_r = jnp.zeros((RWKV_DIM,), F32)

    x2d = x.reshape(t, d)
    v_first = None
    for l in range(depth):
        w_comb = w_in[l]
        widths = [RWKV_IN, CONV_IN, ATTN_IN]
        if l > 0:
            w_comb = jnp.concatenate(
                [w_comb, jnp.pad(vres_w_in[l - 1], ((0, 0), (0, VRES_PAD - VRES_LORA)))], axis=1)
            widths.append(VRES_PAD)
        outs = _inproj(x2d, mix_norm[l][None, :], w_comb.astype(BF16), widths, tm)
        p_rwkv, p_conv, p_attn = (o.reshape(b, s, -1) for o in outs[:3])

        wl = jnp.zeros((LORA_W, 3 * RWKV_DIM), F32)
        wl = wl.at[0:32, 0:RWKV_DIM].set(decay_w2[l])
        wl = wl.at[32:64, RWKV_DIM:2 * RWKV_DIM].set(iclr_a2[l])
        wl = wl.at[64:128, 2 * RWKV_DIM:].set(gate_g2[l]).astype(BF16)
        vecs = jnp.stack([decay_w0[l], iclr_a0[l], k_k[l], k_a[l], r_k[l].reshape(-1), ln_x_w[l],
                          ln_x_b[l], vres_v0[l - 1] if l > 0 else zeros_r])
        mu = tshift_mu[l][None, :]
        if l == 0:
            y_rwkv, v_first = _rwkv(p_rwkv, mu, wl, vecs, rwkv_consts)
        else:
            vmu = jnp.pad(vres_mu[l - 1], (0, VRES_PAD - VRES_LORA))[None, :]
            v2 = jnp.pad(vres_v2[l - 1], ((0, VRES_PAD - VRES_LORA), (0, 0))).astype(BF16)
            (y_rwkv,) = _rwkv(p_rwkv, mu, wl, vecs, rwkv_consts,
                              vres=(outs[3].reshape(b, s, VRES_PAD), v_first, vmu, v2))

        y_conv = _short_conv(p_conv, conv_w[l], conv_out_norm[l].reshape(1, -1), seg256, tm)

        reps = ATTN_DIM // HEAD_DIM
        q, k, v = _attn_prep(p_attn, cos_t, sin_t, jnp.tile(q_norm[l], reps)[None, :],
                             jnp.tile(k_norm[l], reps)[None, :], seg384, tm)
        res = [_attn(q, k, v, dil) for dil in DILATIONS]
        y_attn = _attn_combine([r[0] for r in res], [r[1] for r in res],
                               attn_out_norm[l].reshape(1, -1), seg384, tm)

        flat = lambda a: a.reshape(t, -1)
        wo = w_o[l].astype(BF16)
        gain = ffn_norm[l][None, :]
        i = l // 2
        if l % 2 == 0:
            xn, h2 = _outproj(x2d, flat(y_rwkv), flat(y_conv), flat(y_attn), wo, gain, tm)
            tmf = _row_tile(t, 1024)
            n_tiles = t // tmf
            x2d = _ffn(h2, dense_wg[i][None].astype(BF16), dense_wu[i][None].astype(BF16),
                       dense_wd[i][None].astype(BF16), jnp.zeros((n_tiles,), jnp.int32),
                       jnp.full((1,), n_tiles, jnp.int32), tmf, 512, res=xn)
        else:
            rw = jnp.pad(router[i], ((0, 0), (0, LANES - N_EXPERTS)))
            rw_hi = rw.astype(BF16)
            rw_lo = (rw - rw_hi.astype(F32)).astype(BF16)
            rt = jnp.arange(tm)
            tri = (rt[:, None] > rt[None, :]).astype(BF16)
            xn, h2, info, cnt = _outproj(x2d, flat(y_rwkv), flat(y_conv), flat(y_attn), wo, gain,
                                         tm, router=(rw_hi, rw_lo, tri))
            tme = 512
            counts = cnt[0, :N_EXPERTS].astype(jnp.int32)
            padded = (counts + tme - 1) // tme * tme
            ends = jnp.cumsum(padded)
            pstart = ends - padded
            n_rows = 2 * t + N_EXPERTS * tme
            n_tiles = n_rows // tme
            tile_start = jnp.arange(n_tiles, dtype=jnp.int32) * tme
            tile_expert = jnp.minimum(jnp.sum(ends[None, :] <= tile_start[:, None], axis=1),
                                      N_EXPERTS - 1).astype(jnp.int32)
            n_used = (ends[-1] // tme).astype(jnp.int32).reshape(1)
            e01 = info[:, INFO_E0:INFO_E1 + 1].astype(jnp.int32)
            dest = pstart[e01] + info[:, INFO_R0:INFO_R1 + 1].astype(jnp.int32)
            row_tok = jnp.zeros((n_rows,), jnp.int32).at[dest.reshape(-1)].set(
                jnp.repeat(jnp.arange(t, dtype=jnp.int32), 2))
            ys = _moe_ffn(h2, row_tok, moe_wg[i].astype(BF16), moe_wu[i].astype(BF16),
                          moe_wd[i].astype(BF16), tile_expert, n_used, tme, 512)
            x2d = _combine(xn, info, ys, dest, _row_tile(s, 256))
    return x2d.reshape(b, s, d)
```

```python
import functools

import jax
import jax.numpy as jnp
from jax import lax
from jax.experimental import pallas as pl
from jax.experimental.pallas import tpu as pltpu

F32 = jnp.float32
BF16 = jnp.bfloat16

D_MODEL = 1024
HEAD_DIM = 64
RWKV_DIM = 384
CONV_DIM = 256
ATTN_DIM = 384
LORA_W = 128
RWKV_IN = 3 * RWKV_DIM + LORA_W
CONV_IN = 3 * CONV_DIM
ATTN_IN = 3 * ATTN_DIM
VRES_LORA = 16
VRES_PAD = 128
CONV_WIDTH = 3
DILATIONS = (1, 4, 16)
ATTN_BLOCK = 128
DIL_SPAN = ATTN_BLOCK * 16
ROPE_THETA = 10000.0
D_FF = 3584
N_EXPERTS = 8
NORM_EPS = 1e-6
GN_EPS = 64e-5

LANES = 128
RWKV_CHUNK = 64
PAIR = 2 * HEAD_DIM
VMEM_LIMIT = 56 * 1024 * 1024


def _params(*sem):
    return pltpu.CompilerParams(dimension_semantics=sem, vmem_limit_bytes=VMEM_LIMIT)


def _bdot(a, b):
    return jnp.dot(a.astype(BF16), b.astype(BF16), preferred_element_type=F32)


def _bdot_nt(a, b):
    return lax.dot_general(a.astype(BF16), b.astype(BF16), (((1,), (1,)), ((), ())),
                           preferred_element_type=F32)


def _shift_rows(x, carry_row, n=1):
    row = lax.broadcasted_iota(jnp.int32, (x.shape[0], 1), 0)
    out = pltpu.roll(x, n, 0)
    for i in range(n):
        out = jnp.where(row == i, carry_row[i:i + 1, :], out)
    return out


def _inproj_body(x_ref, g_ref, w_ref, *o_refs):
    x = x_ref[...]
    h = x * lax.rsqrt(jnp.mean(x * x, axis=-1, keepdims=True) + NORM_EPS) * g_ref[...]
    p = jnp.dot(h.astype(BF16), w_ref[...], preferred_element_type=F32)
    off = 0
    for o_ref in o_refs:
        n = o_ref.shape[-1]
        o_ref[...] = p[:, off:off + n].astype(o_ref.dtype)
        off += n


def _inproj(x2d, gain, w, widths, tm):
    t, d = x2d.shape
    n = w.shape[1]
    return pl.pallas_call(
        _inproj_body,
        grid=(t // tm,),
        in_specs=[pl.BlockSpec((tm, d), lambda i: (i, 0)),
                  pl.BlockSpec((1, d), lambda i: (0, 0)),
                  pl.BlockSpec((d, n), lambda i: (0, 0))],
        out_specs=[pl.BlockSpec((tm, wd), lambda i: (i, 0)) for wd in widths],
        out_shape=[jax.ShapeDtypeStruct((t, wd), F32) for wd in widths],
        compiler_params=_params("arbitrary"),
        name="inproj",
    )(x2d, gain, w)


def _rwkv_body(*refs, has_vres):
    if has_vres:
        (p_ref, pv_ref, vf_ref, mu_ref, vmu_ref, wl_ref, v2_ref, vec_ref, seg_ref, tri_ref,
         cm_ref, y_ref, prev_ref, pprev_ref, h_ref) = refs
    else:
        (p_ref, mu_ref, wl_ref, vec_ref, seg_ref, tri_ref,
         cm_ref, y_ref, vout_ref, prev_ref, h_ref) = refs
    @pl.when(pl.program_id(0) == 0)
    def _():
        prev_ref[...] = jnp.zeros_like(prev_ref)
        h_ref[...] = jnp.zeros_like(h_ref)
        if has_vres:
            pprev_ref[...] = jnp.zeros_like(pprev_ref)

    nb, nt = p_ref.shape[0], p_ref.shape[1]

    def token_shift(src_ref, carry_ref, mix):
        out = []
        for bi in range(nb):
            cur = src_ref[bi]
            prev = _shift_rows(cur, carry_ref[bi:bi + 1, :])
            carry_ref[bi:bi + 1, :] = cur[nt - 1:nt, :]
            out.append(cur + (prev - cur) * mix)
        return jnp.concatenate(out, axis=0)

    x = token_shift(p_ref, prev_ref, mu_ref[...])
    r = x[:, 0:RWKV_DIM]
    k = x[:, RWKV_DIM:2 * RWKV_DIM]
    v = x[:, 2 * RWKV_DIM:3 * RWKV_DIM]
    z = x[:, 3 * RWKV_DIM:]
    lane = lax.broadcasted_iota(jnp.int32, (1, LANES), 1)
    zt = jnp.where(lane < 32, jnp.tanh(z), jnp.where(lane < 64, z, jax.nn.sigmoid(z)))
    lo = _bdot(zt, wl_ref[...])
    w0, a0, k_k, k_a, r_k, ln_w, ln_b, v0 = (vec_ref[i:i + 1, :] for i in range(8))
    seg = seg_ref[...]

    zz = -(w0 + lo[:, 0:RWKV_DIM])
    softplus = jnp.maximum(zz, 0.0) + jnp.log(1.0 + jnp.exp(-jnp.abs(zz)))
    lw = -jnp.exp(-softplus - 0.5)
    a = jax.nn.sigmoid(a0 + lo[:, RWKV_DIM:2 * RWKV_DIM])
    g = lo[:, 2 * RWKV_DIM:]
    if has_vres:
        xv = token_shift(pv_ref, pprev_ref, vmu_ref[...])
        v_first = vf_ref[...].reshape(nb * nt, RWKV_DIM)
        v = v + (v_first - v) * jax.nn.sigmoid(v0 + _bdot(xv, v2_ref[...]))
    else:
        vout_ref[...] = v.reshape(nb, nt, RWKV_DIM)
    kk = k * k_k
    kk = kk / jnp.maximum(jnp.sqrt(_bdot(kk * kk, seg)), 1e-12)
    kmod = k * (1.0 + (a - 1.0) * k_a)
    bonus = _bdot(r * kmod * r_k, seg) * v

    tri = tri_ref[...]
    lw_hi = lw.astype(BF16)
    lw_lo = (lw - lw_hi.astype(F32)).astype(BF16)
    cum = (jnp.dot(tri, lw_hi, preferred_element_type=F32)
           + jnp.dot(tri, lw_lo, preferred_element_type=F32))
    pinv = jnp.exp(-cum)
    r_t = r * jnp.exp(cum)
    a_t = -kk * jnp.exp(cum - lw)
    b_t = kk * a * pinv
    k_t = kmod * pinv
    p_ends = [jnp.exp(cum[bi * nt + nt - 1:(bi + 1) * nt, :]) for bi in range(nb)]
    p_end_rows = jnp.concatenate([jnp.broadcast_to(pe, (nt, RWKV_DIM)) for pe in p_ends], axis=0)
    bh_t = b_t * p_end_rows
    kh_t = k_t * p_end_rows

    strict, incl, blk8, eye = (cm_ref[i] for i in range(4))
    m0 = (lane < HEAD_DIM).astype(F32)
    m1 = 1.0 - m0

    def stack(t, rows, sl):
        ts = t[rows, sl]
        return jnp.concatenate([ts * m0, ts * m1], axis=0)

    npair = RWKV_DIM // PAIR
    chains = [(bi, j) for bi in range(nb) for j in range(npair)]
    each = lambda fn, *cols: [fn(*args) for args in zip(*cols)]
    cat0 = lambda *ts: jnp.concatenate(ts, axis=0)
    cat1 = lambda *ts: jnp.concatenate(ts, axis=1)

    def stacks(t):
        return [stack(t, slice(bi * nt, (bi + 1) * nt), slice(j * PAIR, (j + 1) * PAIR))
                for bi, j in chains]

    a_st, r_st, b_st, k_st, v_st, bh_st, kh_st = (stacks(t) for t in (a_t, r_t, b_t, k_t, v, bh_t, kh_t))
    gram = each(lambda a_, r_, b_, k_: _bdot_nt(cat0(a_, r_), cat0(b_, k_)), a_st, r_st, b_st, k_st)
    n2 = 2 * nt
    a_ab = [gm[:n2, :n2] * strict for gm in gram]
    a_ak = [gm[:n2, n2:] * strict for gm in gram]
    a_rb = [gm[n2:, :n2] * incl for gm in gram]
    a_rk = [gm[n2:, n2:] * incl for gm in gram]
    dg = [m * blk8 for m in a_ab]
    off = each(lambda m, d_: m - d_, a_ab, dg)
    d2 = each(_bdot, dg, dg)
    tdiag = [eye + d_ for d_ in dg]
    tdiag = each(lambda t_, d_: t_ + _bdot(t_, d_), tdiag, d2)
    d4 = each(_bdot, d2, d2)
    tdiag = each(lambda t_, d_: t_ + _bdot(t_, d_), tdiag, d4)
    e1 = each(_bdot, tdiag, off)
    e2 = each(_bdot, e1, e1)
    yb = [eye + e_ for e_ in e1]
    yb = each(lambda y_, e_: y_ + _bdot(y_, e_), yb, e2)
    e4 = each(_bdot, e2, e2)
    yb = each(lambda y_, e_: y_ + _bdot(y_, e_), yb, e4)
    tinv = each(_bdot, yb, tdiag)

    hs = [h_ref[bi, j] for bi, j in chains]
    xs = each(lambda a_, ak_, h_, v_: _bdot(cat1(a_, ak_), cat0(h_, v_)), a_st, a_ak, hs, v_st)
    us = each(_bdot, tinv, xs)
    y_st = each(lambda r_, rb_, rk_, h_, u_, v_: _bdot(cat1(r_, rb_, rk_), cat0(h_, u_, v_)),
                r_st, a_rb, a_rk, hs, us, v_st)
    pe_col = [jnp.sum(eye * p_ends[bi][:, j * PAIR:(j + 1) * PAIR], axis=1, keepdims=True)
              for bi, j in chains]
    h_new = each(lambda pc_, h_, bh_, kh_, u_, v_: pc_ * h_ + _bdot(cat0(bh_, kh_).T, cat0(u_, v_)),
                 pe_col, hs, bh_st, kh_st, us, v_st)
    for (bi, j), hn in zip(chains, h_new):
        h_ref[bi, j] = hn
    y_pair = [ys_[:nt] + ys_[nt:] for ys_ in y_st]
    y = cat0(*[cat1(*y_pair[bi * npair:(bi + 1) * npair]) for bi in range(nb)])
    inv_n = 1.0 / HEAD_DIM
    mean = _bdot(y, seg) * inv_n
    dy = y - mean
    var = _bdot(dy * dy, seg) * inv_n
    yn = dy * lax.rsqrt(var + GN_EPS) * ln_w + ln_b
    y_ref[...] = ((yn + bonus) * g).reshape(nb, nt, RWKV_DIM).astype(y_ref.dtype)


def _rwkv(p_rwkv, mu, wl, vecs, consts, vres=None):
    b, s, _ = p_rwkv.shape
    c = RWKV_CHUNK
    seg, tri, cm = consts
    tile = lambda wd: pl.BlockSpec((b, c, wd), lambda j: (0, j, 0))
    full = lambda arr: pl.BlockSpec(arr.shape, lambda j: (0,) * arr.ndim)
    y_shape = jax.ShapeDtypeStruct((b, s, RWKV_DIM), BF16)
    scratch = [pltpu.VMEM((b, RWKV_IN), F32)]
    if vres is None:
        ins = [p_rwkv, mu, wl, vecs, seg, tri, cm]
        in_specs = [tile(RWKV_IN)] + [full(t) for t in ins[1:]]
        out_shape = [y_shape, jax.ShapeDtypeStruct((b, s, RWKV_DIM), F32)]
        out_specs = [tile(RWKV_DIM), tile(RWKV_DIM)]
    else:
        p_vres, v_first, vmu, v2 = vres
        ins = [p_rwkv, p_vres, v_first, mu, vmu, wl, v2, vecs, seg, tri, cm]
        in_specs = [tile(RWKV_IN), tile(VRES_PAD), tile(RWKV_DIM)] + [full(t) for t in ins[3:]]
        out_shape = [y_shape]
        out_specs = [tile(RWKV_DIM)]
        scratch.append(pltpu.VMEM((b, VRES_PAD), F32))
    scratch.append(pltpu.VMEM((b, RWKV_DIM // PAIR, PAIR, PAIR), F32))
    return pl.pallas_call(
        functools.partial(_rwkv_body, has_vres=vres is not None),
        grid=(s // c,),
        in_specs=in_specs, out_specs=out_specs, out_shape=out_shape,
        scratch_shapes=scratch,
        compiler_params=_params("arbitrary"),
        name="rwkv7",
    )(*ins)


def _conv_body(p_ref, w_ref, gain_ref, seg_ref, o_ref, carry_ref):
    @pl.when(pl.program_id(1) == 0)
    def _():
        carry_ref[...] = jnp.zeros_like(carry_ref)

    p = p_ref[...]
    nt = p.shape[0]
    u = p[:, CONV_DIM:2 * CONV_DIM] * p[:, 2 * CONV_DIM:]
    carry = carry_ref[...]
    u1 = _shift_rows(u, carry[1:2, :], 1)
    u2 = _shift_rows(u, carry, 2)
    carry_ref[...] = u[nt - 2:nt, :]
    w = w_ref[...]
    y = p[:, 0:CONV_DIM] * (w[0:1, :] * u2 + w[1:2, :] * u1 + w[2:3, :] * u)
    ms = _bdot(y * y, seg_ref[...]) * (1.0 / HEAD_DIM)
    o_ref[...] = (y * lax.rsqrt(ms + NORM_EPS) * gain_ref[...]).astype(o_ref.dtype)


def _short_conv(p_conv, w, gain, seg, tm):
    b, s, _ = p_conv.shape
    return pl.pallas_call(
        _conv_body,
        grid=(b, s // tm),
        in_specs=[pl.BlockSpec((None, tm, CONV_IN), lambda i, j: (i, j, 0)),
                  pl.BlockSpec(w.shape, lambda i, j: (0, 0)),
                  pl.BlockSpec(gain.shape, lambda i, j: (0, 0)),
                  pl.BlockSpec(seg.shape, lambda i, j: (0, 0))],
        out_specs=pl.BlockSpec((None, tm, CONV_DIM), lambda i, j: (i, j, 0)),
        out_shape=jax.ShapeDtypeStruct((b, s, CONV_DIM), BF16),
        scratch_shapes=[pltpu.VMEM((CONV_WIDTH - 1, CONV_DIM), F32)],
        compiler_params=_params("arbitrary", "arbitrary"),
        name="short_conv",
    )(p_conv, w, gain, seg)


def _attn_prep_body(p_ref, cos_ref, sin_ref, qg_ref, kg_ref, seg_ref, q_ref, k_ref, v_ref):
    p = p_ref[...]
    seg = seg_ref[...]
    lane = lax.broadcasted_iota(jnp.int32, (1, LANES), 1)
    first = (lane & (HEAD_DIM - 1)) < HEAD_DIM // 2
    for part, g_ref, o_ref, scale in ((0, qg_ref, q_ref, HEAD_DIM ** -0.5), (1, kg_ref, k_ref, 1.0)):
        x = p[:, part * ATTN_DIM:(part + 1) * ATTN_DIM]
        ms = _bdot(x * x, seg) * (1.0 / HEAD_DIM)
        xn = x * lax.rsqrt(ms + NORM_EPS) * g_ref[...]
        rot = []
        for j in range(ATTN_DIM // LANES):
            xs = xn[:, j * LANES:(j + 1) * LANES]
            rot.append(jnp.where(first, pltpu.roll(xs, LANES - HEAD_DIM // 2, 1),
                                 pltpu.roll(xs, HEAD_DIM // 2, 1)))
        out = (xn * cos_ref[...] + jnp.concatenate(rot, axis=1) * sin_ref[...]) * scale
        for j in range(ATTN_DIM // PAIR):
            o_ref[j] = out[:, j * PAIR:(j + 1) * PAIR]
    for j in range(ATTN_DIM // PAIR):
        v_ref[j] = p[:, 2 * ATTN_DIM + j * PAIR:2 * ATTN_DIM + (j + 1) * PAIR]


def _pair_major_spec(rows, row_index):
    return pl.BlockSpec((None, ATTN_DIM // PAIR, rows, PAIR), lambda i, j: (i, 0, row_index(j), 0))


def _attn_prep(p_attn, cos, sin, qg, kg, seg, tm):
    b, s, _ = p_attn.shape
    tab = pl.BlockSpec((tm, ATTN_DIM), lambda i, j: (j, 0))
    full = lambda arr: pl.BlockSpec(arr.shape, lambda i, j: (0, 0))
    shp = jax.ShapeDtypeStruct((b, ATTN_DIM // PAIR, s, PAIR), F32)
    return pl.pallas_call(
        _attn_prep_body,
        grid=(b, s // tm),
        in_specs=[pl.BlockSpec((None, tm, ATTN_IN), lambda i, j: (i, j, 0)), tab, tab,
                  full(qg), full(kg), full(seg)],
        out_specs=[_pair_major_spec(tm, lambda j: j)] * 3,
        out_shape=[shp] * 3,
        compiler_params=_params("arbitrary", "arbitrary"),
        name="attn_prep",
    )(p_attn, cos, sin, qg, kg, seg)


def _attn_body(q_ref, k_ref, v_ref, kprev_ref, vprev_ref, o_ref, l_ref, *, dil):
    span = pl.program_id(1)
    blk = ATTN_BLOCK
    npair = ATTN_DIM // PAIR
    qi = lax.broadcasted_iota(jnp.int32, (blk, 2 * blk), 0)
    kj = lax.broadcasted_iota(jnp.int32, (blk, 2 * blk), 1)
    dist = qi + blk - kj
    mask = (dist >= 0) & (dist <= ATTN_BLOCK) & (kj + span * (2 * blk) >= blk)
    lane = lax.broadcasted_iota(jnp.int32, (1, LANES), 1)
    head0 = lane < HEAD_DIM
    m0 = head0.astype(F32)
    heads = [(j, hm) for j in range(npair) for hm in (m0, 1.0 - m0)]

    def one_class(r):
        rows = pl.ds(r, blk, stride=dil) if dil > 1 else pl.ds(0, blk)
        q = [q_ref[j, rows, :] for j in range(npair)]
        kcat = [jnp.concatenate([kprev_ref[j, rows, :], k_ref[j, rows, :]], axis=0) for j in range(npair)]
        vcat = [jnp.concatenate([vprev_ref[j, rows, :], v_ref[j, rows, :]], axis=0) for j in range(npair)]
        s_all = [_bdot_nt(q[j] * hm, kcat[j]) for j, hm in heads]
        s_all = [jnp.where(mask, s, -jnp.inf) for s in s_all]
        m_all = [jnp.max(s, axis=-1, keepdims=True) for s in s_all]
        pe_all = [jnp.exp(s - m) for s, m in zip(s_all, m_all)]
        den_all = [jnp.sum(pe, axis=-1, keepdims=True) for pe in pe_all]
        o_all = [_bdot(pe, vcat[j]) / den for pe, den, (j, _) in zip(pe_all, den_all, heads)]
        lse_all = [m + jnp.log(den) for m, den in zip(m_all, den_all)]
        for j in range(npair):
            o_ref[j, rows, :] = jnp.where(head0, o_all[2 * j], o_all[2 * j + 1])
            l_ref[j, rows, :] = jnp.where(head0, lse_all[2 * j], lse_all[2 * j + 1])

    if dil == 1:
        one_class(0)
    else:
        def step(r, carry):
            one_class(r)
            return carry
        lax.fori_loop(0, dil, step, 0)


def _attn(q, k, v, dil):
    b, _, s, _ = q.shape
    span = ATTN_BLOCK * dil
    spec = _pair_major_spec(span, lambda m: m)
    prev = _pair_major_spec(span, lambda m: jnp.maximum(m - 1, 0))
    shp = jax.ShapeDtypeStruct(q.shape, F32)
    return pl.pallas_call(
        functools.partial(_attn_body, dil=dil),
        grid=(b, s // span),
        in_specs=[spec, spec, spec, prev, prev],
        out_specs=[spec, spec],
        out_shape=[shp, shp],
        compiler_params=_params("arbitrary", "arbitrary"),
        name=f"attn_d{dil}",
    )(q, k, v, k, v)


def _attn_combine_body(*refs):
    n = len(DILATIONS)
    o_refs, l_refs = refs[:n], refs[n:2 * n]
    gain_ref, seg_ref, y_ref = refs[2 * n:]
    for j in range(ATTN_DIM // PAIR):
        sl = slice(j * PAIR, (j + 1) * PAIR)
        ls = [r[j] for r in l_refs]
        m = functools.reduce(jnp.maximum, ls)
        es = [jnp.exp(l - m) for l in ls]
        den = functools.reduce(jnp.add, es)
        o = functools.reduce(jnp.add, [e * r[j] for e, r in zip(es, o_refs)]) / den
        ms = _bdot(o * o, seg_ref[...]) * (1.0 / HEAD_DIM)
        y_ref[:, sl] = (o * lax.rsqrt(ms + NORM_EPS) * gain_ref[:, sl]).astype(y_ref.dtype)


def _attn_combine(os_, ls_, gain, seg, tm):
    b, _, s, _ = os_[0].shape
    tile = _pair_major_spec(tm, lambda j: j)
    full = lambda arr: pl.BlockSpec(arr.shape, lambda i, j: (0, 0))
    return pl.pallas_call(
        _attn_combine_body,
        grid=(b, s // tm),
        in_specs=[tile] * (2 * len(os_)) + [full(gain), full(seg)],
        out_specs=pl.BlockSpec((None, tm, ATTN_DIM), lambda i, j: (i, j, 0)),
        out_shape=jax.ShapeDtypeStruct((b, s, ATTN_DIM), BF16),
        compiler_params=_params("arbitrary", "arbitrary"),
        name="attn_combine",
    )(*os_, *ls_, gain, seg)


INFO_E0, INFO_E1, INFO_R0, INFO_R1, INFO_G0, INFO_G1 = range(6)


def _outproj_body(*refs, has_router):
    if has_router:
        (x_ref, yr_ref, yc_ref, ya_ref, wo_ref, g_ref, rwh_ref, rwl_ref, tri_ref,
         xn_ref, h_ref, info_ref, cnt_ref, run_ref) = refs
    else:
        x_ref, yr_ref, yc_ref, ya_ref, wo_ref, g_ref, xn_ref, h_ref = refs
    o1, o2 = RWKV_DIM, RWKV_DIM + CONV_DIM
    acc = jnp.dot(yr_ref[...], wo_ref[0:o1, :], preferred_element_type=F32)
    acc += jnp.dot(yc_ref[...], wo_ref[o1:o2, :], preferred_element_type=F32)
    acc += jnp.dot(ya_ref[...], wo_ref[o2:, :], preferred_element_type=F32)
    xn = x_ref[...] + acc
    xn_ref[...] = xn
    h = xn * lax.rsqrt(jnp.mean(xn * xn, axis=-1, keepdims=True) + NORM_EPS) * g_ref[...]
    h_ref[...] = h.astype(h_ref.dtype)
    if not has_router:
        return

    @pl.when(pl.program_id(0) == 0)
    def _():
        run_ref[...] = jnp.zeros_like(run_ref)

    h_hi = h.astype(BF16)
    h_lo = (h - h_hi.astype(F32)).astype(BF16)
    rwh = rwh_ref[...]
    logits = (jnp.dot(h_hi, rwh, preferred_element_type=F32)
              + jnp.dot(h_lo, rwh, preferred_element_type=F32)
              + jnp.dot(h_hi, rwl_ref[...], preferred_element_type=F32))
    lane = lax.broadcasted_iota(jnp.int32, (1, LANES), 1)
    logits = jnp.where(lane < N_EXPERTS, logits, -jnp.inf)

    def top(lg):
        m = jnp.max(lg, axis=-1, keepdims=True)
        idx = jnp.min(jnp.where(lg == m, lane, LANES), axis=-1, keepdims=True)
        return m, idx, lane == idx

    m1, i1, oh1 = top(logits)
    m2, i2, oh2 = top(jnp.where(oh1, -jnp.inf, logits))
    zexp = jnp.exp(m2 - m1)
    g0 = 1.0 / (1.0 + zexp)
    g1 = zexp * g0
    oh = oh1.astype(F32) + oh2.astype(F32)
    tot = jnp.dot(tri_ref[...], oh.astype(BF16), preferred_element_type=F32) + run_ref[...]
    r0 = jnp.sum(jnp.where(oh1, tot, 0.0), axis=-1, keepdims=True)
    r1 = jnp.sum(jnp.where(oh2, tot, 0.0), axis=-1, keepdims=True)
    run_ref[...] += jnp.sum(oh, axis=0, keepdims=True)
    cnt_ref[...] = run_ref[...]
    info = jnp.zeros(logits.shape, F32)
    for ln, val in ((INFO_E0, i1.astype(F32)), (INFO_E1, i2.astype(F32)), (INFO_R0, r0),
                    (INFO_R1, r1), (INFO_G0, g0), (INFO_G1, g1)):
        info = jnp.where(lane == ln, val, info)
    info_ref[...] = info


def _outproj(x2d, yr, yc, ya, wo, gain, tm, router=None):
    t, d = x2d.shape
    row = lambda wd: pl.BlockSpec((tm, wd), lambda i: (i, 0))
    full = lambda arr: pl.BlockSpec(arr.shape, lambda i: (0, 0))
    ins = [x2d, yr, yc, ya, wo, gain]
    in_specs = [row(d), row(RWKV_DIM), row(CONV_DIM), row(ATTN_DIM), full(wo), full(gain)]
    out_shape = [jax.ShapeDtypeStruct((t, d), F32)]
    out_specs = [row(d)]
    scratch = []
    if router is None:
        out_shape.append(jax.ShapeDtypeStruct((t, d), BF16))
        out_specs.append(row(d))
    else:
        ins += list(router)
        in_specs += [full(a) for a in router]
        out_shape += [jax.ShapeDtypeStruct((t, d), F32), jax.ShapeDtypeStruct((t, LANES), F32),
                      jax.ShapeDtypeStruct((1, LANES), F32)]
        out_specs += [row(d), row(LANES), pl.BlockSpec((1, LANES), lambda i: (0, 0))]
        scratch = [pltpu.VMEM((1, LANES), F32)]
    return pl.pallas_call(
        functools.partial(_outproj_body, has_router=router is not None),
        grid=(t // tm,),
        in_specs=in_specs, out_specs=out_specs, out_shape=out_shape,
        scratch_shapes=scratch,
        compiler_params=_params("arbitrary"),
        name="outproj",
    )(*ins)


def _ffn_body(te_ref, nu_ref, x_ref, wg_ref, wu_ref, wd_ref, *rest, has_res):
    if has_res:
        res_ref, o_ref, acc_ref = rest
    else:
        o_ref, acc_ref = rest
    i, f = pl.program_id(0), pl.program_id(1)

    @pl.when(f == 0)
    def _():
        acc_ref[...] = jnp.zeros_like(acc_ref)

    @pl.when(i < nu_ref[0])
    def _():
        x = x_ref[...].astype(BF16)
        hg = jnp.dot(x, wg_ref[...], preferred_element_type=F32)
        hu = jnp.dot(x, wu_ref[...], preferred_element_type=F32)
        act = hg * jax.nn.sigmoid(hg) * hu
        acc_ref[...] += jnp.dot(act.astype(BF16), wd_ref[...], preferred_element_type=F32)

    @pl.when(f == pl.num_programs(1) - 1)
    def _():
        out = acc_ref[...]
        if has_res:
            out = out + res_ref[...]
        o_ref[...] = out.astype(o_ref.dtype)


def _ffn(x, wg, wu, wd, tile_expert, n_used, tm, tf, res=None):
    n, d = x.shape
    ff = wg.shape[-1]
    row = pl.BlockSpec((tm, d), lambda i, f, te, nu: (i, 0))
    ins = [x, wg, wu, wd]
    in_specs = [row,
                pl.BlockSpec((None, d, tf), lambda i, f, te, nu: (te[i], 0, f)),
                pl.BlockSpec((None, d, tf), lambda i, f, te, nu: (te[i], 0, f)),
                pl.BlockSpec((None, tf, d), lambda i, f, te, nu: (te[i], f, 0))]
    if res is not None:
        ins.append(res)
        in_specs.append(row)
    return pl.pallas_call(
        functools.partial(_ffn_body, has_res=res is not None),
        grid_spec=pltpu.PrefetchScalarGridSpec(
            num_scalar_prefetch=2,
            grid=(n // tm, ff // tf),
            in_specs=in_specs,
            out_specs=row,
            scratch_shapes=[pltpu.VMEM((tm, d), F32)]),
        out_shape=jax.ShapeDtypeStruct((n, d), F32),
        compiler_params=_params("arbitrary", "arbitrary"),
        name="swiglu",
    )(tile_expert, n_used, *ins)


def _row_copy(src, s, dst, d, sem):
    return pltpu.make_async_copy(src.at[pl.ds(s, 1)], dst.at[pl.ds(d, 1)], sem)


def _moe_ffn_body(te_ref, nu_ref, tok_ref, tokn_ref, h_hbm, wg_ref, wu_ref, wd_ref, o_ref,
                  acc_ref, xbuf, sem, *, n_f):
    i, f = pl.program_id(0), pl.program_id(1)
    n_used = nu_ref[0]
    tm = xbuf.shape[1]
    slot = lax.rem(i, 2)
    rows_per_step = tm // n_f

    def row_copy(idx_ref, r, sl):
        return pltpu.make_async_copy(h_hbm.at[pl.ds(idx_ref[0, 0, r], 1)],
                                     xbuf.at[sl, pl.ds(r, 1)], sem.at[sl])

    @pl.when(f == 0)
    def _():
        acc_ref[...] = jnp.zeros_like(acc_ref)

    @pl.when((f == 0) & (i == 0))
    def _():
        def issue(r, carry):
            row_copy(tok_ref, r, 0).start()
            return carry
        lax.fori_loop(0, tm, issue, 0, unroll=8)

    @pl.when((f == 0) & (i <= n_used))
    def _():
        pltpu.make_async_copy(h_hbm.at[pl.ds(0, tm)], xbuf.at[slot], sem.at[slot]).wait()

    @pl.when(i < n_used)
    def _():
        for u in range(rows_per_step):
            row_copy(tokn_ref, f * rows_per_step + u, 1 - slot).start()
        x = xbuf[slot].astype(BF16)
        hg = jnp.dot(x, wg_ref[...], preferred_element_type=F32)
        hu = jnp.dot(x, wu_ref[...], preferred_element_type=F32)
        act = hg * jax.nn.sigmoid(hg) * hu
        acc_ref[...] += jnp.dot(act.astype(BF16), wd_ref[...], preferred_element_type=F32)

    @pl.when(f == pl.num_programs(1) - 1)
    def _():
        o_ref[...] = acc_ref[...]


def _moe_ffn(h2, row_tok, wg, wu, wd, tile_expert, n_used, tm, tf):
    d = h2.shape[1]
    n = row_tok.shape[0]
    ff = wg.shape[-1]
    n_tiles = n // tm
    assert tm % (ff // tf) == 0
    idx = row_tok.reshape(n_tiles, 1, tm)
    smem = lambda fn: pl.BlockSpec((1, 1, tm), fn, memory_space=pltpu.SMEM)
    return pl.pallas_call(
        functools.partial(_moe_ffn_body, n_f=ff // tf),
        grid_spec=pltpu.PrefetchScalarGridSpec(
            num_scalar_prefetch=2,
            grid=(n_tiles, ff // tf),
            in_specs=[smem(lambda i, f, te, nu: (i, 0, 0)),
                      smem(lambda i, f, te, nu: (jnp.minimum(i + 1, n_tiles - 1), 0, 0)),
                      pl.BlockSpec(memory_space=pl.ANY),
                      pl.BlockSpec((None, d, tf), lambda i, f, te, nu: (te[i], 0, f)),
                      pl.BlockSpec((None, d, tf), lambda i, f, te, nu: (te[i], 0, f)),
                      pl.BlockSpec((None, tf, d), lambda i, f, te, nu: (te[i], f, 0))],
            out_specs=pl.BlockSpec((tm, d), lambda i, f, te, nu: (i, 0)),
            scratch_shapes=[pltpu.VMEM((tm, d), F32), pltpu.VMEM((2, tm, d), F32),
                            pltpu.SemaphoreType.DMA((2,))]),
        out_shape=jax.ShapeDtypeStruct((n, d), F32),
        compiler_params=_params("arbitrary", "arbitrary"),
        name="moe_swiglu",
    )(tile_expert, n_used, idx, idx, h2, wg, wu, wd)


def _combine_body(dest_ref, x_ref, info_ref, ys_hbm, o_ref, buf0, buf1, sem):
    ntok = x_ref.shape[0]
    bufs = (buf0, buf1)

    def issue(t, carry):
        for kq in range(2):
            _row_copy(ys_hbm, dest_ref[0, 0, 2 * t + kq], bufs[kq], t, sem).start(priority=kq)
        return carry

    lax.fori_loop(0, ntok, issue, 0, unroll=8)
    for kq in range(2):
        pltpu.make_async_copy(ys_hbm.at[pl.ds(0, ntok)], bufs[kq], sem).wait()
    info = info_ref[...]
    g0 = info[:, INFO_G0:INFO_G0 + 1]
    g1 = info[:, INFO_G1:INFO_G1 + 1]
    o_ref[...] = x_ref[...] + g0 * buf0[...] + g1 * buf1[...]


def _combine(xn, info, ys, dest, tm):
    t, d = xn.shape
    row = lambda wd: pl.BlockSpec((tm, wd), lambda i: (i, 0))
    return pl.pallas_call(
        _combine_body,
        grid=(t // tm,),
        in_specs=[pl.BlockSpec((1, 1, 2 * tm), lambda i: (i, 0, 0), memory_space=pltpu.SMEM),
                  row(d), row(LANES), pl.BlockSpec(memory_space=pl.ANY)],
        out_specs=row(d),
        out_shape=jax.ShapeDtypeStruct((t, d), F32),
        scratch_shapes=[pltpu.VMEM((tm, d), F32), pltpu.VMEM((tm, d), F32),
                        pltpu.SemaphoreType.DMA],
        compiler_params=_params("arbitrary"),
        name="moe_combine",
    )(dest.reshape(t // tm, 1, 2 * tm), xn, info, ys)


def _seg_matrix(n):
    i = jnp.arange(n) // HEAD_DIM
    return (i[:, None] == i[None, :]).astype(BF16)


def _rwkv_consts(nb):
    c = RWKV_CHUNK
    t = jnp.arange(nb * c)
    tri = ((t[:, None] >= t[None, :]) & (t[:, None] // c == t[None, :] // c)).astype(BF16)
    i = jnp.arange(2 * c)
    same = (i[:, None] // c) == (i[None, :] // c)
    ti, tj = (i % c)[:, None], (i % c)[None, :]
    strict = same & (ti > tj)
    incl = same & (ti >= tj)
    blk8 = (i[:, None] // 8) == (i[None, :] // 8)
    eye = i[:, None] == i[None, :]
    cm = jnp.stack([strict, incl, blk8, eye]).astype(F32)
    return _seg_matrix(RWKV_DIM), tri, cm


def _rope_tables(s):
    half = HEAD_DIM // 2
    inv_freq = ROPE_THETA ** (-jnp.arange(half, dtype=F32) * 2.0 / HEAD_DIM)
    ang = jnp.arange(s, dtype=F32)[:, None] * inv_freq[None, :]
    cos, sin = jnp.cos(ang), jnp.sin(ang)
    reps = ATTN_DIM // HEAD_DIM
    cos_t = jnp.tile(jnp.concatenate([cos, cos], axis=1), (1, reps))
    sin_t = jnp.tile(jnp.concatenate([-sin, sin], axis=1), (1, reps))
    return cos_t, sin_t


def _row_tile(n, want):
    return want if n % want == 0 else n


def kernel(x, mix_norm, w_in, tshift_mu, vres_w_in, vres_mu, vres_v0, vres_v2, decay_w0, decay_w2, iclr_a0, iclr_a2, gate_g2, k_k, k_a, r_k, ln_x_w, ln_x_b, conv_w, conv_out_norm, q_norm, k_norm, attn_out_norm, w_o, ffn_norm, dense_wg, dense_wu, dense_wd, router, moe_wg, moe_wu, moe_wd):
    b, s, d = x.shape
    assert d == D_MODEL and s % DIL_SPAN == 0
    t = b * s
    depth = w_in.shape[0]
    tm = _row_tile(s, 512)
    seg384 = _seg_matrix(RWKV_DIM)
    seg256 = _seg_matrix(CONV_DIM)
    rwkv_consts = _rwkv_consts(b)
    cos_t, sin_t = _rope_tables(s)
    zeros_r = jnp.zeros((RWKV_DIM,), F32)

    x2d = x.reshape(t, d)
    v_first = None
    for l in range(depth):
        w_comb = w_in[l]
        widths = [RWKV_IN, CONV_IN, ATTN_IN]
        if l > 0:
            w_comb = jnp.concatenate(
                [w_comb, jnp.pad(vres_w_in[l - 1], ((0, 0), (0, VRES_PAD - VRES_LORA)))], axis=1)
            widths.append(VRES_PAD)
        outs = _inproj(x2d, mix_norm[l][None, :], w_comb.astype(BF16), widths, tm)
        p_rwkv, p_conv, p_attn = (o.reshape(b, s, -1) for o in outs[:3])

        wl = jnp.zeros((LORA_W, 3 * RWKV_DIM), F32)
        wl = wl.at[0:32, 0:RWKV_DIM].set(decay_w2[l])
        wl = wl.at[32:64, RWKV_DIM:2 * RWKV_DIM].set(iclr_a2[l])
        wl = wl.at[64:128, 2 * RWKV_DIM:].set(gate_g2[l]).astype(BF16)
        vecs = jnp.stack([decay_w0[l], iclr_a0[l], k_k[l], k_a[l], r_k[l].reshape(-1), ln_x_w[l],
                          ln_x_b[l], vres_v0[l - 1] if l > 0 else zeros_r])
        mu = tshift_mu[l][None, :]
        if l == 0:
            y_rwkv, v_first = _rwkv(p_rwkv, mu, wl, vecs, rwkv_consts)
        else:
            vmu = jnp.pad(vres_mu[l - 1], (0, VRES_PAD - VRES_LORA))[None, :]
            v2 = jnp.pad(vres_v2[l - 1], ((0, VRES_PAD - VRES_LORA), (0, 0))).astype(BF16)
            (y_rwkv,) = _rwkv(p_rwkv, mu, wl, vecs, rwkv_consts,
                              vres=(outs[3].reshape(b, s, VRES_PAD), v_first, vmu, v2))

        y_conv = _short_conv(p_conv, conv_w[l], conv_out_norm[l].reshape(1, -1), seg256, tm)

        reps = ATTN_DIM // HEAD_DIM
        q, k, v = _attn_prep(p_attn, cos_t, sin_t, jnp.tile(q_norm[l], reps)[None, :],
                             jnp.tile(k_norm[l], reps)[None, :], seg384, tm)
        res = [_attn(q, k, v, dil) for dil in DILATIONS]
        y_attn = _attn_combine([r[0] for r in res], [r[1] for r in res],
                               attn_out_norm[l].reshape(1, -1), _seg_matrix(PAIR), tm)

        flat = lambda a: a.reshape(t, -1)
        wo = w_o[l].astype(BF16)
        gain = ffn_norm[l][None, :]
        i = l // 2
        if l % 2 == 0:
            xn, h2 = _outproj(x2d, flat(y_rwkv), flat(y_conv), flat(y_attn), wo, gain, tm)
            tmf = _row_tile(t, 1024)
            n_tiles = t // tmf
            x2d = _ffn(h2, dense_wg[i][None].astype(BF16), dense_wu[i][None].astype(BF16),
                       dense_wd[i][None].astype(BF16), jnp.zeros((n_tiles,), jnp.int32),
                       jnp.full((1,), n_tiles, jnp.int32), tmf, 512, res=xn)
        else:
            rw = jnp.pad(router[i], ((0, 0), (0, LANES - N_EXPERTS)))
            rw_hi = rw.astype(BF16)
            rw_lo = (rw - rw_hi.astype(F32)).astype(BF16)
            rt = jnp.arange(tm)
            tri = (rt[:, None] > rt[None, :]).astype(BF16)
            xn, h2, info, cnt = _outproj(x2d, flat(y_rwkv), flat(y_conv), flat(y_attn), wo, gain,
                                         tm, router=(rw_hi, rw_lo, tri))
            tme = 512
            counts = cnt[0, :N_EXPERTS].astype(jnp.int32)
            padded = (counts + tme - 1) // tme * tme
            ends = jnp.cumsum(padded)
            pstart = ends - padded
            n_rows = 2 * t + N_EXPERTS * tme
            n_tiles = n_rows // tme
            tile_start = jnp.arange(n_tiles, dtype=jnp.int32) * tme
            tile_expert = jnp.minimum(jnp.sum(ends[None, :] <= tile_start[:, None], axis=1),
                                      N_EXPERTS - 1).astype(jnp.int32)
            n_used = (ends[-1] // tme).astype(jnp.int32).reshape(1)
            e01 = info[:, INFO_E0:INFO_E1 + 1].astype(jnp.int32)
            dest = pstart[e01] + info[:, INFO_R0:INFO_R1 + 1].astype(jnp.int32)
            row_tok = jnp.zeros((n_rows,), jnp.int32).at[dest.reshape(-1)].set(
                jnp.repeat(jnp.arange(t, dtype=jnp.int32), 2), unique_indices=True)
            ys = _moe_ffn(h2, row_tok, moe_wg[i].astype(BF16), moe_wu[i].astype(BF16),
                          moe_wd[i].astype(BF16), tile_expert, n_used, tme, 896)
            x2d = _combine(xn, info, ys, dest, _row_tile(s, 256))
    return x2d.reshape(b, s, d)
```

```python
import functools

import jax
import jax.numpy as jnp
from jax import lax
from jax.experimental import pallas as pl
from jax.experimental.pallas import tpu as pltpu

F32 = jnp.float32
BF16 = jnp.bfloat16

D_MODEL = 1024
HEAD_DIM = 64
RWKV_DIM = 384
CONV_DIM = 256
ATTN_DIM = 384
LORA_W = 128
RWKV_IN = 3 * RWKV_DIM + LORA_W
CONV_IN = 3 * CONV_DIM
ATTN_IN = 3 * ATTN_DIM
VRES_LORA = 16
VRES_PAD = 128
CONV_WIDTH = 3
DILATIONS = (1, 4, 16)
ATTN_BLOCK = 128
DIL_SPAN = ATTN_BLOCK * 16
ROPE_THETA = 10000.0
D_FF = 3584
N_EXPERTS = 8
NORM_EPS = 1e-6
GN_EPS = 64e-5

LANES = 128
RWKV_CHUNK = 64
PAIR = 2 * HEAD_DIM
VMEM_LIMIT = 56 * 1024 * 1024


def _params(*sem):
    return pltpu.CompilerParams(dimension_semantics=sem, vmem_limit_bytes=VMEM_LIMIT)


def _bdot(a, b):
    return jnp.dot(a.astype(BF16), b.astype(BF16), preferred_element_type=F32)


def _bdot_nt(a, b):
    return lax.dot_general(a.astype(BF16), b.astype(BF16), (((1,), (1,)), ((), ())),
                           preferred_element_type=F32)


def _shift_rows(x, carry_row, n=1):
    row = lax.broadcasted_iota(jnp.int32, (x.shape[0], 1), 0)
    out = pltpu.roll(x, n, 0)
    for i in range(n):
        out = jnp.where(row == i, carry_row[i:i + 1, :], out)
    return out


def _inproj_body(x_ref, g_ref, w_ref, *o_refs):
    x = x_ref[...]
    h = x * lax.rsqrt(jnp.mean(x * x, axis=-1, keepdims=True) + NORM_EPS) * g_ref[...]
    p = jnp.dot(h.astype(BF16), w_ref[...], preferred_element_type=F32)
    off = 0
    for o_ref in o_refs:
        n = o_ref.shape[-1]
        o_ref[...] = p[:, off:off + n].astype(o_ref.dtype)
        off += n


def _inproj(x2d, gain, w, widths, tm):
    t, d = x2d.shape
    n = w.shape[1]
    return pl.pallas_call(
        _inproj_body,
        grid=(t // tm,),
        in_specs=[pl.BlockSpec((tm, d), lambda i: (i, 0)),
                  pl.BlockSpec((1, d), lambda i: (0, 0)),
                  pl.BlockSpec((d, n), lambda i: (0, 0))],
        out_specs=[pl.BlockSpec((tm, wd), lambda i: (i, 0)) for wd in widths],
        out_shape=[jax.ShapeDtypeStruct((t, wd), F32) for wd in widths],
        compiler_params=_params("arbitrary"),
        name="inproj",
    )(x2d, gain, w)


def _rwkv_body(*refs, has_vres):
    if has_vres:
        (p_ref, pv_ref, vf_ref, mu_ref, vmu_ref, wl_ref, v2_ref, vec_ref, seg_ref, tri_ref,
         cm_ref, y_ref, prev_ref, pprev_ref, h_ref) = refs
    else:
        (p_ref, mu_ref, wl_ref, vec_ref, seg_ref, tri_ref,
         cm_ref, y_ref, vout_ref, prev_ref, h_ref) = refs
    @pl.when(pl.program_id(0) == 0)
    def _():
        prev_ref[...] = jnp.zeros_like(prev_ref)
        h_ref[...] = jnp.zeros_like(h_ref)
        if has_vres:
            pprev_ref[...] = jnp.zeros_like(pprev_ref)

    nb, nt = p_ref.shape[0], p_ref.shape[1]

    def token_shift(src_ref, carry_ref, mix):
        out = []
        for bi in range(nb):
            cur = src_ref[bi]
            prev = _shift_rows(cur, carry_ref[bi:bi + 1, :])
            carry_ref[bi:bi + 1, :] = cur[nt - 1:nt, :]
            out.append(cur + (prev - cur) * mix)
        return jnp.concatenate(out, axis=0)

    x = token_shift(p_ref, prev_ref, mu_ref[...])
    r = x[:, 0:RWKV_DIM]
    k = x[:, RWKV_DIM:2 * RWKV_DIM]
    v = x[:, 2 * RWKV_DIM:3 * RWKV_DIM]
    z = x[:, 3 * RWKV_DIM:]
    lane = lax.broadcasted_iota(jnp.int32, (1, LANES), 1)
    zt = jnp.where(lane < 32, jnp.tanh(z), jnp.where(lane < 64, z, jax.nn.sigmoid(z)))
    lo = _bdot(zt, wl_ref[...])
    w0, a0, k_k, k_a, r_k, ln_w, ln_b, v0 = (vec_ref[i:i + 1, :] for i in range(8))
    seg = seg_ref[...]

    zz = -(w0 + lo[:, 0:RWKV_DIM])
    softplus = jnp.maximum(zz, 0.0) + jnp.log(1.0 + jnp.exp(-jnp.abs(zz)))
    lw = -jnp.exp(-softplus - 0.5)
    a = jax.nn.sigmoid(a0 + lo[:, RWKV_DIM:2 * RWKV_DIM])
    g = lo[:, 2 * RWKV_DIM:]
    if has_vres:
        xv = token_shift(pv_ref, pprev_ref, vmu_ref[...])
        v_first = vf_ref[...].reshape(nb * nt, RWKV_DIM)
        v = v + (v_first - v) * jax.nn.sigmoid(v0 + _bdot(xv, v2_ref[...]))
    else:
        vout_ref[...] = v.reshape(nb, nt, RWKV_DIM)
    kk = k * k_k
    kk = kk / jnp.maximum(jnp.sqrt(_bdot(kk * kk, seg)), 1e-12)
    kmod = k * (1.0 + (a - 1.0) * k_a)
    bonus = _bdot(r * kmod * r_k, seg) * v

    tri = tri_ref[...]
    lw_hi = lw.astype(BF16)
    lw_lo = (lw - lw_hi.astype(F32)).astype(BF16)
    cum = (jnp.dot(tri, lw_hi, preferred_element_type=F32)
           + jnp.dot(tri, lw_lo, preferred_element_type=F32))
    pinv = jnp.exp(-cum)
    r_t = r * jnp.exp(cum)
    a_t = -kk * jnp.exp(cum - lw)
    b_t = kk * a * pinv
    k_t = kmod * pinv
    p_ends = [jnp.exp(cum[bi * nt + nt - 1:(bi + 1) * nt, :]) for bi in range(nb)]
    p_end_rows = jnp.concatenate([jnp.broadcast_to(pe, (nt, RWKV_DIM)) for pe in p_ends], axis=0)
    bh_t = b_t * p_end_rows
    kh_t = k_t * p_end_rows

    strict, incl, blk8, eye = (cm_ref[i] for i in range(4))
    m0 = (lane < HEAD_DIM).astype(F32)
    m1 = 1.0 - m0

    def stack(t, rows, sl):
        ts = t[rows, sl]
        return jnp.concatenate([ts * m0, ts * m1], axis=0)

    npair = RWKV_DIM // PAIR
    chains = [(bi, j) for bi in range(nb) for j in range(npair)]
    each = lambda fn, *cols: [fn(*args) for args in zip(*cols)]
    cat0 = lambda *ts: jnp.concatenate(ts, axis=0)
    cat1 = lambda *ts: jnp.concatenate(ts, axis=1)

    def stacks(t):
        return [stack(t, slice(bi * nt, (bi + 1) * nt), slice(j * PAIR, (j + 1) * PAIR))
                for bi, j in chains]

    a_st, r_st, b_st, k_st, v_st, bh_st, kh_st = (stacks(t) for t in (a_t, r_t, b_t, k_t, v, bh_t, kh_t))
    gram = each(lambda a_, r_, b_, k_: _bdot_nt(cat0(a_, r_), cat0(b_, k_)), a_st, r_st, b_st, k_st)
    n2 = 2 * nt
    a_ab = [gm[:n2, :n2] * strict for gm in gram]
    a_ak = [gm[:n2, n2:] * strict for gm in gram]
    a_rb = [gm[n2:, :n2] * incl for gm in gram]
    a_rk = [gm[n2:, n2:] * incl for gm in gram]
    dg = [m * blk8 for m in a_ab]
    off = each(lambda m, d_: m - d_, a_ab, dg)
    d2 = each(_bdot, dg, dg)
    tdiag = [eye + d_ for d_ in dg]
    tdiag = each(lambda t_, d_: t_ + _bdot(t_, d_), tdiag, d2)
    d4 = each(_bdot, d2, d2)
    tdiag = each(lambda t_, d_: t_ + _bdot(t_, d_), tdiag, d4)
    e1 = each(_bdot, tdiag, off)
    e2 = each(_bdot, e1, e1)
    yb = [eye + e_ for e_ in e1]
    yb = each(lambda y_, e_: y_ + _bdot(y_, e_), yb, e2)
    e4 = each(_bdot, e2, e2)
    yb = each(lambda y_, e_: y_ + _bdot(y_, e_), yb, e4)
    tinv = each(_bdot, yb, tdiag)

    hs = [h_ref[bi, j] for bi, j in chains]
    xs = each(lambda a_, ak_, h_, v_: _bdot(cat1(a_, ak_), cat0(h_, v_)), a_st, a_ak, hs, v_st)
    us = each(_bdot, tinv, xs)
    y_st = each(lambda r_, rb_, rk_, h_, u_, v_: _bdot(cat1(r_, rb_, rk_), cat0(h_, u_, v_)),
                r_st, a_rb, a_rk, hs, us, v_st)
    pe_col = [jnp.sum(eye * p_ends[bi][:, j * PAIR:(j + 1) * PAIR], axis=1, keepdims=True)
              for bi, j in chains]
    h_new = each(lambda pc_, h_, bh_, kh_, u_, v_: pc_ * h_ + _bdot(cat0(bh_, kh_).T, cat0(u_, v_)),
                 pe_col, hs, bh_st, kh_st, us, v_st)
    for (bi, j), hn in zip(chains, h_new):
        h_ref[bi, j] = hn
    y_pair = [ys_[:nt] + ys_[nt:] for ys_ in y_st]
    y = cat0(*[cat1(*y_pair[bi * npair:(bi + 1) * npair]) for bi in range(nb)])
    inv_n = 1.0 / HEAD_DIM
    mean = _bdot(y, seg) * inv_n
    dy = y - mean
    var = _bdot(dy * dy, seg) * inv_n
    yn = dy * lax.rsqrt(var + GN_EPS) * ln_w + ln_b
    y_ref[...] = ((yn + bonus) * g).reshape(nb, nt, RWKV_DIM).astype(y_ref.dtype)


def _rwkv(p_rwkv, mu, wl, vecs, consts, vres=None):
    b, s, _ = p_rwkv.shape
    c = RWKV_CHUNK
    seg, tri, cm = consts
    tile = lambda wd: pl.BlockSpec((b, c, wd), lambda j: (0, j, 0))
    full = lambda arr: pl.BlockSpec(arr.shape, lambda j: (0,) * arr.ndim)
    y_shape = jax.ShapeDtypeStruct((b, s, RWKV_DIM), BF16)
    scratch = [pltpu.VMEM((b, RWKV_IN), F32)]
    if vres is None:
        ins = [p_rwkv, mu, wl, vecs, seg, tri, cm]
        in_specs = [tile(RWKV_IN)] + [full(t) for t in ins[1:]]
        out_shape = [y_shape, jax.ShapeDtypeStruct((b, s, RWKV_DIM), F32)]
        out_specs = [tile(RWKV_DIM), tile(RWKV_DIM)]
    else:
        p_vres, v_first, vmu, v2 = vres
        ins = [p_rwkv, p_vres, v_first, mu, vmu, wl, v2, vecs, seg, tri, cm]
        in_specs = [tile(RWKV_IN), tile(VRES_PAD), tile(RWKV_DIM)] + [full(t) for t in ins[3:]]
        out_shape = [y_shape]
        out_specs = [tile(RWKV_DIM)]
        scratch.append(pltpu.VMEM((b, VRES_PAD), F32))
    scratch.append(pltpu.VMEM((b, RWKV_DIM // PAIR, PAIR, PAIR), F32))
    return pl.pallas_call(
        functools.partial(_rwkv_body, has_vres=vres is not None),
        grid=(s // c,),
        in_specs=in_specs, out_specs=out_specs, out_shape=out_shape,
        scratch_shapes=scratch,
        compiler_params=_params("arbitrary"),
        name="rwkv7",
    )(*ins)


def _conv_body(p_ref, w_ref, gain_ref, seg_ref, o_ref, carry_ref):
    @pl.when(pl.program_id(1) == 0)
    def _():
        carry_ref[...] = jnp.zeros_like(carry_ref)

    p = p_ref[...]
    nt = p.shape[0]
    u = p[:, CONV_DIM:2 * CONV_DIM] * p[:, 2 * CONV_DIM:]
    carry = carry_ref[...]
    u1 = _shift_rows(u, carry[1:2, :], 1)
    u2 = _shift_rows(u, carry, 2)
    carry_ref[...] = u[nt - 2:nt, :]
    w = w_ref[...]
    y = p[:, 0:CONV_DIM] * (w[0:1, :] * u2 + w[1:2, :] * u1 + w[2:3, :] * u)
    ms = _bdot(y * y, seg_ref[...]) * (1.0 / HEAD_DIM)
    o_ref[...] = (y * lax.rsqrt(ms + NORM_EPS) * gain_ref[...]).astype(o_ref.dtype)


def _short_conv(p_conv, w, gain, seg, tm):
    b, s, _ = p_conv.shape
    return pl.pallas_call(
        _conv_body,
        grid=(b, s // tm),
        in_specs=[pl.BlockSpec((None, tm, CONV_IN), lambda i, j: (i, j, 0)),
                  pl.BlockSpec(w.shape, lambda i, j: (0, 0)),
                  pl.BlockSpec(gain.shape, lambda i, j: (0, 0)),
                  pl.BlockSpec(seg.shape, lambda i, j: (0, 0))],
        out_specs=pl.BlockSpec((None, tm, CONV_DIM), lambda i, j: (i, j, 0)),
        out_shape=jax.ShapeDtypeStruct((b, s, CONV_DIM), BF16),
        scratch_shapes=[pltpu.VMEM((CONV_WIDTH - 1, CONV_DIM), F32)],
        compiler_params=_params("arbitrary", "arbitrary"),
        name="short_conv",
    )(p_conv, w, gain, seg)


def _attn_prep_body(p_ref, cos_ref, sin_ref, qg_ref, kg_ref, seg_ref, q_ref, k_ref, v_ref):
    p = p_ref[...]
    seg = seg_ref[...]
    lane = lax.broadcasted_iota(jnp.int32, (1, LANES), 1)
    first = (lane & (HEAD_DIM - 1)) < HEAD_DIM // 2
    for part, g_ref, o_ref, scale in ((0, qg_ref, q_ref, HEAD_DIM ** -0.5), (1, kg_ref, k_ref, 1.0)):
        x = p[:, part * ATTN_DIM:(part + 1) * ATTN_DIM]
        ms = _bdot(x * x, seg) * (1.0 / HEAD_DIM)
        xn = x * lax.rsqrt(ms + NORM_EPS) * g_ref[...]
        rot = []
        for j in range(ATTN_DIM // LANES):
            xs = xn[:, j * LANES:(j + 1) * LANES]
            rot.append(jnp.where(first, pltpu.roll(xs, LANES - HEAD_DIM // 2, 1),
                                 pltpu.roll(xs, HEAD_DIM // 2, 1)))
        out = (xn * cos_ref[...] + jnp.concatenate(rot, axis=1) * sin_ref[...]) * scale
        for j in range(ATTN_DIM // PAIR):
            o_ref[j] = out[:, j * PAIR:(j + 1) * PAIR]
    for j in range(ATTN_DIM // PAIR):
        v_ref[j] = p[:, 2 * ATTN_DIM + j * PAIR:2 * ATTN_DIM + (j + 1) * PAIR]


def _pair_major_spec(rows, row_index):
    return pl.BlockSpec((None, ATTN_DIM // PAIR, rows, PAIR), lambda i, j: (i, 0, row_index(j), 0))


def _attn_prep(p_attn, cos, sin, qg, kg, seg, tm):
    b, s, _ = p_attn.shape
    tab = pl.BlockSpec((tm, ATTN_DIM), lambda i, j: (j, 0))
    full = lambda arr: pl.BlockSpec(arr.shape, lambda i, j: (0, 0))
    shp = jax.ShapeDtypeStruct((b, ATTN_DIM // PAIR, s, PAIR), F32)
    return pl.pallas_call(
        _attn_prep_body,
        grid=(b, s // tm),
        in_specs=[pl.BlockSpec((None, tm, ATTN_IN), lambda i, j: (i, j, 0)), tab, tab,
                  full(qg), full(kg), full(seg)],
        out_specs=[_pair_major_spec(tm, lambda j: j)] * 3,
        out_shape=[shp] * 3,
        compiler_params=_params("arbitrary", "arbitrary"),
        name="attn_prep",
    )(p_attn, cos, sin, qg, kg, seg)


def _attn_body(q_ref, k_ref, v_ref, kprev_ref, vprev_ref, o_ref, l_ref, *, dil):
    span = pl.program_id(1)
    blk = ATTN_BLOCK
    npair = ATTN_DIM // PAIR
    qi = lax.broadcasted_iota(jnp.int32, (blk, 2 * blk), 0)
    kj = lax.broadcasted_iota(jnp.int32, (blk, 2 * blk), 1)
    dist = qi + blk - kj
    band = (dist >= 0) & (dist <= ATTN_BLOCK)
    lane = lax.broadcasted_iota(jnp.int32, (1, LANES), 1)
    head0 = lane < HEAD_DIM
    m0 = head0.astype(F32)
    heads = [(j, hm) for j in range(npair) for hm in (m0, 1.0 - m0)]

    def block_attn(q, kcat, vcat, block_index):
        mask = band & (kj + block_index * (2 * blk) >= blk)
        s_all = [_bdot_nt(q[j] * hm, kcat[j]) for j, hm in heads]
        s_all = [jnp.where(mask, s, -jnp.inf) for s in s_all]
        m_all = [jnp.max(s, axis=-1, keepdims=True) for s in s_all]
        pe_all = [jnp.exp(s - m) for s, m in zip(s_all, m_all)]
        den_all = [jnp.sum(pe, axis=-1, keepdims=True) for pe in pe_all]
        o_all = [_bdot(pe, vcat[j]) / den for pe, den, (j, _) in zip(pe_all, den_all, heads)]
        lse_all = [m + jnp.log(den) for m, den in zip(m_all, den_all)]
        o_pair = [jnp.where(head0, o_all[2 * j], o_all[2 * j + 1]) for j in range(npair)]
        l_pair = [jnp.where(head0, lse_all[2 * j], lse_all[2 * j + 1]) for j in range(npair)]
        return o_pair, l_pair

    if dil == 1:
        nq = q_ref.shape[1] // blk
        kext = [jnp.concatenate([kprev_ref[j], k_ref[j]], axis=0) for j in range(npair)]
        vext = [jnp.concatenate([vprev_ref[j], v_ref[j]], axis=0) for j in range(npair)]
        for qb in range(nq):
            rows = slice(qb * blk, (qb + 1) * blk)
            keys = slice(qb * blk, (qb + 2) * blk)
            o_pair, l_pair = block_attn([q_ref[j, rows, :] for j in range(npair)],
                                        [kx[keys] for kx in kext], [vx[keys] for vx in vext],
                                        span * nq + qb)
            for j in range(npair):
                o_ref[j, rows, :] = o_pair[j]
                l_ref[j, rows, :] = l_pair[j]
    else:
        def one_class(r, carry):
            rows = pl.ds(r, blk, stride=dil)
            cat = lambda prev_ref, ref: [jnp.concatenate([prev_ref[j, rows, :], ref[j, rows, :]], axis=0)
                                         for j in range(npair)]
            o_pair, l_pair = block_attn([q_ref[j, rows, :] for j in range(npair)],
                                        cat(kprev_ref, k_ref), cat(vprev_ref, v_ref), span)
            for j in range(npair):
                o_ref[j, rows, :] = o_pair[j]
                l_ref[j, rows, :] = l_pair[j]
            return carry
        lax.fori_loop(0, dil, one_class, 0)


def _attn(q, k, v, dil):
    b, _, s, _ = q.shape
    if dil == 1:
        nq = 4
        span = ATTN_BLOCK * nq
        prev = _pair_major_spec(ATTN_BLOCK, lambda m: jnp.maximum(m * nq - 1, 0))
    else:
        span = ATTN_BLOCK * dil
        prev = _pair_major_spec(span, lambda m: jnp.maximum(m - 1, 0))
    spec = _pair_major_spec(span, lambda m: m)
    shp = jax.ShapeDtypeStruct(q.shape, F32)
    return pl.pallas_call(
        functools.partial(_attn_body, dil=dil),
        grid=(b, s // span),
        in_specs=[spec, spec, spec, prev, prev],
        out_specs=[spec, spec],
        out_shape=[shp, shp],
        compiler_params=_params("arbitrary", "arbitrary"),
        name=f"attn_d{dil}",
    )(q, k, v, k, v)


def _attn_combine_body(*refs):
    n = len(DILATIONS)
    o_refs, l_refs = refs[:n], refs[n:2 * n]
    gain_ref, seg_ref, y_ref = refs[2 * n:]
    for j in range(ATTN_DIM // PAIR):
        sl = slice(j * PAIR, (j + 1) * PAIR)
        ls = [r[j] for r in l_refs]
        m = functools.reduce(jnp.maximum, ls)
        es = [jnp.exp(l - m) for l in ls]
        den = functools.reduce(jnp.add, es)
        o = functools.reduce(jnp.add, [e * r[j] for e, r in zip(es, o_refs)]) / den
        ms = _bdot(o * o, seg_ref[...]) * (1.0 / HEAD_DIM)
        y_ref[:, sl] = (o * lax.rsqrt(ms + NORM_EPS) * gain_ref[:, sl]).astype(y_ref.dtype)


def _attn_combine(os_, ls_, gain, seg, tm):
    b, _, s, _ = os_[0].shape
    tile = _pair_major_spec(tm, lambda j: j)
    full = lambda arr: pl.BlockSpec(arr.shape, lambda i, j: (0, 0))
    return pl.pallas_call(
        _attn_combine_body,
        grid=(b, s // tm),
        in_specs=[tile] * (2 * len(os_)) + [full(gain), full(seg)],
        out_specs=pl.BlockSpec((None, tm, ATTN_DIM), lambda i, j: (i, j, 0)),
        out_shape=jax.ShapeDtypeStruct((b, s, ATTN_DIM), BF16),
        compiler_params=_params("arbitrary", "arbitrary"),
        name="attn_combine",
    )(*os_, *ls_, gain, seg)


INFO_E0, INFO_E1, INFO_R0, INFO_R1, INFO_G0, INFO_G1 = range(6)


def _outproj_body(*refs, has_router):
    if has_router:
        (x_ref, yr_ref, yc_ref, ya_ref, wo_ref, g_ref, rwh_ref, rwl_ref, tri_ref,
         xn_ref, h_ref, info_ref, cnt_ref, run_ref) = refs
    else:
        x_ref, yr_ref, yc_ref, ya_ref, wo_ref, g_ref, xn_ref, h_ref = refs
    o1, o2 = RWKV_DIM, RWKV_DIM + CONV_DIM
    acc = jnp.dot(yr_ref[...], wo_ref[0:o1, :], preferred_element_type=F32)
    acc += jnp.dot(yc_ref[...], wo_ref[o1:o2, :], preferred_element_type=F32)
    acc += jnp.dot(ya_ref[...], wo_ref[o2:, :], preferred_element_type=F32)
    xn = x_ref[...] + acc
    xn_ref[...] = xn
    h = xn * lax.rsqrt(jnp.mean(xn * xn, axis=-1, keepdims=True) + NORM_EPS) * g_ref[...]
    h_ref[...] = h.astype(h_ref.dtype)
    if not has_router:
        return

    @pl.when(pl.program_id(0) == 0)
    def _():
        run_ref[...] = jnp.zeros_like(run_ref)

    h_hi = h.astype(BF16)
    h_lo = (h - h_hi.astype(F32)).astype(BF16)
    rwh = rwh_ref[...]
    logits = (jnp.dot(h_hi, rwh, preferred_element_type=F32)
              + jnp.dot(h_lo, rwh, preferred_element_type=F32)
              + jnp.dot(h_hi, rwl_ref[...], preferred_element_type=F32))
    lane = lax.broadcasted_iota(jnp.int32, (1, LANES), 1)
    logits = jnp.where(lane < N_EXPERTS, logits, -jnp.inf)

    def top(lg):
        m = jnp.max(lg, axis=-1, keepdims=True)
        idx = jnp.min(jnp.where(lg == m, lane, LANES), axis=-1, keepdims=True)
        return m, idx, lane == idx

    m1, i1, oh1 = top(logits)
    m2, i2, oh2 = top(jnp.where(oh1, -jnp.inf, logits))
    zexp = jnp.exp(m2 - m1)
    g0 = 1.0 / (1.0 + zexp)
    g1 = zexp * g0
    oh = oh1.astype(F32) + oh2.astype(F32)
    tot = jnp.dot(tri_ref[...], oh.astype(BF16), preferred_element_type=F32) + run_ref[...]
    r0 = jnp.sum(jnp.where(oh1, tot, 0.0), axis=-1, keepdims=True)
    r1 = jnp.sum(jnp.where(oh2, tot, 0.0), axis=-1, keepdims=True)
    run_ref[...] += jnp.sum(oh, axis=0, keepdims=True)
    cnt_ref[...] = run_ref[...]
    info = jnp.zeros(logits.shape, F32)
    for ln, val in ((INFO_E0, i1.astype(F32)), (INFO_E1, i2.astype(F32)), (INFO_R0, r0),
                    (INFO_R1, r1), (INFO_G0, g0), (INFO_G1, g1)):
        info = jnp.where(lane == ln, val, info)
    info_ref[...] = info


def _outproj(x2d, yr, yc, ya, wo, gain, tm, router=None):
    t, d = x2d.shape
    row = lambda wd: pl.BlockSpec((tm, wd), lambda i: (i, 0))
    full = lambda arr: pl.BlockSpec(arr.shape, lambda i: (0, 0))
    ins = [x2d, yr, yc, ya, wo, gain]
    in_specs = [row(d), row(RWKV_DIM), row(CONV_DIM), row(ATTN_DIM), full(wo), full(gain)]
    out_shape = [jax.ShapeDtypeStruct((t, d), F32)]
    out_specs = [row(d)]
    scratch = []
    if router is None:
        out_shape.append(jax.ShapeDtypeStruct((t, d), BF16))
        out_specs.append(row(d))
    else:
        ins += list(router)
        in_specs += [full(a) for a in router]
        out_shape += [jax.ShapeDtypeStruct((t, d), F32), jax.ShapeDtypeStruct((t, LANES), F32),
                      jax.ShapeDtypeStruct((1, LANES), F32)]
        out_specs += [row(d), row(LANES), pl.BlockSpec((1, LANES), lambda i: (0, 0))]
        scratch = [pltpu.VMEM((1, LANES), F32)]
    return pl.pallas_call(
        functools.partial(_outproj_body, has_router=router is not None),
        grid=(t // tm,),
        in_specs=in_specs, out_specs=out_specs, out_shape=out_shape,
        scratch_shapes=scratch,
        compiler_params=_params("arbitrary"),
        name="outproj",
    )(*ins)


def _ffn_body(te_ref, nu_ref, x_ref, wg_ref, wu_ref, wd_ref, *rest, has_res):
    if has_res:
        res_ref, o_ref, acc_ref = rest
    else:
        o_ref, acc_ref = rest
    i, f = pl.program_id(0), pl.program_id(1)

    @pl.when(f == 0)
    def _():
        acc_ref[...] = jnp.zeros_like(acc_ref)

    @pl.when(i < nu_ref[0])
    def _():
        x = x_ref[...].astype(BF16)
        hg = jnp.dot(x, wg_ref[...], preferred_element_type=F32)
        hu = jnp.dot(x, wu_ref[...], preferred_element_type=F32)
        act = hg * jax.nn.sigmoid(hg) * hu
        acc_ref[...] += jnp.dot(act.astype(BF16), wd_ref[...], preferred_element_type=F32)

    @pl.when(f == pl.num_programs(1) - 1)
    def _():
        out = acc_ref[...]
        if has_res:
            out = out + res_ref[...]
        o_ref[...] = out.astype(o_ref.dtype)


def _ffn(x, wg, wu, wd, tile_expert, n_used, tm, tf, res=None):
    n, d = x.shape
    ff = wg.shape[-1]
    row = pl.BlockSpec((tm, d), lambda i, f, te, nu: (i, 0))
    ins = [x, wg, wu, wd]
    in_specs = [row,
                pl.BlockSpec((None, d, tf), lambda i, f, te, nu: (te[i], 0, f)),
                pl.BlockSpec((None, d, tf), lambda i, f, te, nu: (te[i], 0, f)),
                pl.BlockSpec((None, tf, d), lambda i, f, te, nu: (te[i], f, 0))]
    if res is not None:
        ins.append(res)
        in_specs.append(row)
    return pl.pallas_call(
        functools.partial(_ffn_body, has_res=res is not None),
        grid_spec=pltpu.PrefetchScalarGridSpec(
            num_scalar_prefetch=2,
            grid=(n // tm, ff // tf),
            in_specs=in_specs,
            out_specs=row,
            scratch_shapes=[pltpu.VMEM((tm, d), F32)]),
        out_shape=jax.ShapeDtypeStruct((n, d), F32),
        compiler_params=_params("arbitrary", "arbitrary"),
        name="swiglu",
    )(tile_expert, n_used, *ins)


def _row_copy(src, s, dst, d, sem):
    return pltpu.make_async_copy(src.at[pl.ds(s, 1)], dst.at[pl.ds(d, 1)], sem)


def _moe_ffn_body(te_ref, nu_ref, tok_ref, tokn_ref, h_hbm, wg_ref, wu_ref, wd_ref, o_ref,
                  acc_ref, xbuf, sem, *, n_f):
    i, f = pl.program_id(0), pl.program_id(1)
    n_used = nu_ref[0]
    tm = acc_ref.shape[0]
    slot = lax.rem(i, 2)
    rows_per_step = xbuf.shape[1] // n_f

    n_issue = rows_per_step * n_f

    def row_copy(idx_ref, r, sl):
        src = idx_ref[0, 0, jnp.minimum(r, tm - 1)]
        return pltpu.make_async_copy(h_hbm.at[pl.ds(src, 1)], xbuf.at[sl, pl.ds(r, 1)], sem.at[sl])

    @pl.when(f == 0)
    def _():
        acc_ref[...] = jnp.zeros_like(acc_ref)

    @pl.when((f == 0) & (i == 0))
    def _():
        def issue(r, carry):
            row_copy(tok_ref, r, 0).start()
            return carry
        lax.fori_loop(0, n_issue, issue, 0)

    @pl.when((f == 0) & (i <= n_used))
    def _():
        pltpu.make_async_copy(h_hbm.at[pl.ds(0, n_issue)], xbuf.at[slot, pl.ds(0, n_issue)],
                              sem.at[slot]).wait()

    @pl.when(i < n_used)
    def _():
        for u in range(rows_per_step):
            row_copy(tokn_ref, f * rows_per_step + u, 1 - slot).start()
        x = xbuf[slot, pl.ds(0, tm)].astype(BF16)
        hg = jnp.dot(x, wg_ref[...], preferred_element_type=F32)
        hu = jnp.dot(x, wu_ref[...], preferred_element_type=F32)
        act = hg * jax.nn.sigmoid(hg) * hu
        acc_ref[...] += jnp.dot(act.astype(BF16), wd_ref[...], preferred_element_type=F32)

    @pl.when(f == pl.num_programs(1) - 1)
    def _():
        o_ref[...] = acc_ref[...]


def _moe_ffn(h2, row_tok, wg, wu, wd, tile_expert, n_used, tm, tf):
    d = h2.shape[1]
    n = row_tok.shape[0]
    ff = wg.shape[-1]
    n_tiles = n // tm
    n_f = ff // tf
    buf_rows = pl.cdiv(pl.cdiv(tm, n_f), 8) * 8 * n_f
    idx = row_tok.reshape(n_tiles, 1, tm)
    smem = lambda fn: pl.BlockSpec((1, 1, tm), fn, memory_space=pltpu.SMEM)
    return pl.pallas_call(
        functools.partial(_moe_ffn_body, n_f=n_f),
        grid_spec=pltpu.PrefetchScalarGridSpec(
            num_scalar_prefetch=2,
            grid=(n_tiles, n_f),
            in_specs=[smem(lambda i, f, te, nu: (i, 0, 0)),
                      smem(lambda i, f, te, nu: (jnp.minimum(i + 1, n_tiles - 1), 0, 0)),
                      pl.BlockSpec(memory_space=pl.ANY),
                      pl.BlockSpec((None, d, tf), lambda i, f, te, nu: (te[i], 0, f)),
                      pl.BlockSpec((None, d, tf), lambda i, f, te, nu: (te[i], 0, f)),
                      pl.BlockSpec((None, tf, d), lambda i, f, te, nu: (te[i], f, 0))],
            out_specs=pl.BlockSpec((tm, d), lambda i, f, te, nu: (i, 0)),
            scratch_shapes=[pltpu.VMEM((tm, d), F32), pltpu.VMEM((2, buf_rows, d), F32),
                            pltpu.SemaphoreType.DMA((2,))]),
        out_shape=jax.ShapeDtypeStruct((n, d), F32),
        compiler_params=_params("arbitrary", "arbitrary"),
        name="moe_swiglu",
    )(tile_expert, n_used, idx, idx, h2, wg, wu, wd)


def _combine_body(dest_ref, destn_ref, x_ref, info_ref, ys_hbm, o_ref, buf, sem):
    i = pl.program_id(0)
    ntok = x_ref.shape[0]
    slot = lax.rem(i, 2)

    def gather(idx_ref, sl):
        def issue(t, carry):
            for kq in range(2):
                _row_copy(ys_hbm, idx_ref[0, 0, 2 * t + kq], buf.at[sl, kq], t,
                          sem.at[sl]).start(priority=kq)
            return carry
        lax.fori_loop(0, ntok, issue, 0, unroll=8)

    @pl.when(i == 0)
    def _():
        gather(dest_ref, 0)

    @pl.when(i + 1 < pl.num_programs(0))
    def _():
        gather(destn_ref, 1 - slot)

    for kq in range(2):
        pltpu.make_async_copy(ys_hbm.at[pl.ds(0, ntok)], buf.at[slot, kq], sem.at[slot]).wait()
    info = info_ref[...]
    g0 = info[:, INFO_G0:INFO_G0 + 1]
    g1 = info[:, INFO_G1:INFO_G1 + 1]
    o_ref[...] = x_ref[...] + g0 * buf[slot, 0] + g1 * buf[slot, 1]


def _combine(xn, info, ys, dest, tm):
    t, d = xn.shape
    n = t // tm
    row = lambda wd: pl.BlockSpec((tm, wd), lambda i: (i, 0))
    smem = lambda fn: pl.BlockSpec((1, 1, 2 * tm), fn, memory_space=pltpu.SMEM)
    idx = dest.reshape(n, 1, 2 * tm)
    return pl.pallas_call(
        _combine_body,
        grid=(n,),
        in_specs=[smem(lambda i: (i, 0, 0)), smem(lambda i: (jnp.minimum(i + 1, n - 1), 0, 0)),
                  row(d), row(LANES), pl.BlockSpec(memory_space=pl.ANY)],
        out_specs=row(d),
        out_shape=jax.ShapeDtypeStruct((t, d), F32),
        scratch_shapes=[pltpu.VMEM((2, 2, tm, d), F32), pltpu.SemaphoreType.DMA((2,))],
        compiler_params=_params("arbitrary"),
        name="moe_combine",
    )(idx, idx, xn, info, ys)


def _seg_matrix(n):
    i = jnp.arange(n) // HEAD_DIM
    return (i[:, None] == i[None, :]).astype(BF16)


def _rwkv_consts(nb):
    c = RWKV_CHUNK
    t = jnp.arange(nb * c)
    tri = ((t[:, None] >= t[None, :]) & (t[:, None] // c == t[None, :] // c)).astype(BF16)
    i = jnp.arange(2 * c)
    same = (i[:, None] // c) == (i[None, :] // c)
    ti, tj = (i % c)[:, None], (i % c)[None, :]
    strict = same & (ti > tj)
    incl = same & (ti >= tj)
    blk8 = (i[:, None] // 8) == (i[None, :] // 8)
    eye = i[:, None] == i[None, :]
    cm = jnp.stack([strict, incl, blk8, eye]).astype(F32)
    return _seg_matrix(RWKV_DIM), tri, cm


def _rope_tables(s):
    half = HEAD_DIM // 2
    inv_freq = ROPE_THETA ** (-jnp.arange(half, dtype=F32) * 2.0 / HEAD_DIM)
    ang = jnp.arange(s, dtype=F32)[:, None] * inv_freq[None, :]
    cos, sin = jnp.cos(ang), jnp.sin(ang)
    reps = ATTN_DIM // HEAD_DIM
    cos_t = jnp.tile(jnp.concatenate([cos, cos], axis=1), (1, reps))
    sin_t = jnp.tile(jnp.concatenate([-sin, sin], axis=1), (1, reps))
    return cos_t, sin_t


def _row_tile(n, want):
    return want if n % want == 0 else n


def kernel(x, mix_norm, w_in, tshift_mu, vres_w_in, vres_mu, vres_v0, vres_v2, decay_w0, decay_w2, iclr_a0, iclr_a2, gate_g2, k_k, k_a, r_k, ln_x_w, ln_x_b, conv_w, conv_out_norm, q_norm, k_norm, attn_out_norm, w_o, ffn_norm, dense_wg, dense_wu, dense_wd, router, moe_wg, moe_wu, moe_wd):
    b, s, d = x.shape
    assert d == D_MODEL and s % DIL_SPAN == 0
    t = b * s
    depth = w_in.shape[0]
    tm = _row_tile(s, 512)
    seg384 = _seg_matrix(RWKV_DIM)
    seg256 = _seg_matrix(CONV_DIM)
    rwkv_consts = _rwkv_consts(b)
    cos_t, sin_t = _rope_tables(s)
    zeros_r = jnp.zeros((RWKV_DIM,), F32)

    x2d = x.reshape(t, d)
    v_first = None
    for l in range(depth):
        w_comb = w_in[l]
        widths = [RWKV_IN, CONV_IN, ATTN_IN]
        if l > 0:
            w_comb = jnp.concatenate(
                [w_comb, jnp.pad(vres_w_in[l - 1], ((0, 0), (0, VRES_PAD - VRES_LORA)))], axis=1)
            widths.append(VRES_PAD)
        outs = _inproj(x2d, mix_norm[l][None, :], w_comb.astype(BF16), widths, tm)
        p_rwkv, p_conv, p_attn = (o.reshape(b, s, -1) for o in outs[:3])

        wl = jnp.zeros((LORA_W, 3 * RWKV_DIM), F32)
        wl = wl.at[0:32, 0:RWKV_DIM].set(decay_w2[l])
        wl = wl.at[32:64, RWKV_DIM:2 * RWKV_DIM].set(iclr_a2[l])
        wl = wl.at[64:128, 2 * RWKV_DIM:].set(gate_g2[l]).astype(BF16)
        vecs = jnp.stack([decay_w0[l], iclr_a0[l], k_k[l], k_a[l], r_k[l].reshape(-1), ln_x_w[l],
                          ln_x_b[l], vres_v0[l - 1] if l > 0 else zeros_r])
        mu = tshift_mu[l][None, :]
        if l == 0:
            y_rwkv, v_first = _rwkv(p_rwkv, mu, wl, vecs, rwkv_consts)
        else:
            vmu = jnp.pad(vres_mu[l - 1], (0, VRES_PAD - VRES_LORA))[None, :]
            v2 = jnp.pad(vres_v2[l - 1], ((0, VRES_PAD - VRES_LORA), (0, 0))).astype(BF16)
            (y_rwkv,) = _rwkv(p_rwkv, mu, wl, vecs, rwkv_consts,
                              vres=(outs[3].reshape(b, s, VRES_PAD), v_first, vmu, v2))

        y_conv = _short_conv(p_conv, conv_w[l], conv_out_norm[l].reshape(1, -1), seg256, tm)

        reps = ATTN_DIM // HEAD_DIM
        q, k, v = _attn_prep(p_attn, cos_t, sin_t, jnp.tile(q_norm[l], reps)[None, :],
                             jnp.tile(k_norm[l], reps)[None, :], seg384, tm)
        res = [_attn(q, k, v, dil) for dil in DILATIONS]
        y_attn = _attn_combine([r[0] for r in res], [r[1] for r in res],
                               attn_out_norm[l].reshape(1, -1), _seg_matrix(PAIR), tm)

        flat = lambda a: a.reshape(t, -1)
        wo = w_o[l].astype(BF16)
        gain = ffn_norm[l][None, :]
        i = l // 2
        if l % 2 == 0:
            xn, h2 = _outproj(x2d, flat(y_rwkv), flat(y_conv), flat(y_attn), wo, gain, tm)
            tmf = _row_tile(t, 1024)
            n_tiles = t // tmf
            x2d = _ffn(h2, dense_wg[i][None].astype(BF16), dense_wu[i][None].astype(BF16),
                       dense_wd[i][None].astype(BF16), jnp.zeros((n_tiles,), jnp.int32),
                       jnp.full((1,), n_tiles, jnp.int32), tmf, 512, res=xn)
        else:
            rw = jnp.pad(router[i], ((0, 0), (0, LANES - N_EXPERTS)))
            rw_hi = rw.astype(BF16)
            rw_lo = (rw - rw_hi.astype(F32)).astype(BF16)
            rt = jnp.arange(tm)
            tri = (rt[:, None] > rt[None, :]).astype(BF16)
            xn, h2, info, cnt = _outproj(x2d, flat(y_rwkv), flat(y_conv), flat(y_attn), wo, gain,
                                         tm, router=(rw_hi, rw_lo, tri))
            tme = 1024
            counts = cnt[0, :N_EXPERTS].astype(jnp.int32)
            padded = (counts + tme - 1) // tme * tme
            ends = jnp.cumsum(padded)
            pstart = ends - padded
            n_rows = 2 * t + N_EXPERTS * tme
            n_tiles = n_rows // tme
            tile_start = jnp.arange(n_tiles, dtype=jnp.int32) * tme
            tile_expert = jnp.minimum(jnp.sum(ends[None, :] <= tile_start[:, None], axis=1),
                                      N_EXPERTS - 1).astype(jnp.int32)
            n_used = (ends[-1] // tme).astype(jnp.int32).reshape(1)
            e01 = info[:, INFO_E0:INFO_E1 + 1].astype(jnp.int32)
            dest = pstart[e01] + info[:, INFO_R0:INFO_R1 + 1].astype(jnp.int32)
            row_tok = jnp.zeros((n_rows,), jnp.int32).at[dest.reshape(-1)].set(
                jnp.repeat(jnp.arange(t, dtype=jnp.int32), 2), unique_indices=True)
            ys = _moe_ffn(h2, row_tok, moe_wg[i].astype(BF16), moe_wu[i].astype(BF16),
                          moe_wd[i].astype(BF16), tile_expert, n_used, tme, 512)
            x2d = _combine(xn, info, ys, dest, _row_tile(s, 256))
    return x2d.reshape(b, s, d)
```

```python
import functools

import jax
import jax.numpy as jnp
from jax import lax
from jax.experimental import pallas as pl
from jax.experimental.pallas import tpu as pltpu

F32 = jnp.float32
BF16 = jnp.bfloat16

D_MODEL = 1024
HEAD_DIM = 64
RWKV_DIM = 384
CONV_DIM = 256
ATTN_DIM = 384
LORA_W = 128
RWKV_IN = 3 * RWKV_DIM + LORA_W
CONV_IN = 3 * CONV_DIM
ATTN_IN = 3 * ATTN_DIM
VRES_LORA = 16
VRES_PAD = 128
CONV_WIDTH = 3
DILATIONS = (1, 4, 16)
ATTN_BLOCK = 128
DIL_SPAN = ATTN_BLOCK * 16
ROPE_THETA = 10000.0
D_FF = 3584
N_EXPERTS = 8
NORM_EPS = 1e-6
GN_EPS = 64e-5

LANES = 128
RWKV_CHUNK = 64
PAIR = 2 * HEAD_DIM
VMEM_LIMIT = 56 * 1024 * 1024


def _params(*sem):
    return pltpu.CompilerParams(dimension_semantics=sem, vmem_limit_bytes=VMEM_LIMIT)


def _bdot(a, b):
    return jnp.dot(a.astype(BF16), b.astype(BF16), preferred_element_type=F32)


def _bdot_nt(a, b):
    return lax.dot_general(a.astype(BF16), b.astype(BF16), (((1,), (1,)), ((), ())),
                           preferred_element_type=F32)


def _shift_rows(x, carry_row, n=1):
    row = lax.broadcasted_iota(jnp.int32, (x.shape[0], 1), 0)
    out = pltpu.roll(x, n, 0)
    for i in range(n):
        out = jnp.where(row == i, carry_row[i:i + 1, :], out)
    return out


def _inproj_body(x_ref, g_ref, w_ref, *o_refs):
    x = x_ref[...]
    h = x * lax.rsqrt(jnp.mean(x * x, axis=-1, keepdims=True) + NORM_EPS) * g_ref[...]
    p = jnp.dot(h.astype(BF16), w_ref[...], preferred_element_type=F32)
    off = 0
    for o_ref in o_refs:
        n = o_ref.shape[-1]
        o_ref[...] = p[:, off:off + n].astype(o_ref.dtype)
        off += n


def _inproj(x2d, gain, w, widths, tm):
    t, d = x2d.shape
    n = w.shape[1]
    return pl.pallas_call(
        _inproj_body,
        grid=(t // tm,),
        in_specs=[pl.BlockSpec((tm, d), lambda i: (i, 0)),
                  pl.BlockSpec((1, d), lambda i: (0, 0)),
                  pl.BlockSpec((d, n), lambda i: (0, 0))],
        out_specs=[pl.BlockSpec((tm, wd), lambda i: (i, 0)) for wd in widths],
        out_shape=[jax.ShapeDtypeStruct((t, wd), F32) for wd in widths],
        compiler_params=_params("arbitrary"),
        name="inproj",
    )(x2d, gain, w)


def _rwkv_body(*refs, has_vres):
    if has_vres:
        (p_ref, pv_ref, vf_ref, mu_ref, vmu_ref, wl_ref, v2_ref, vec_ref, seg_ref, tri_ref,
         cm_ref, y_ref, prev_ref, pprev_ref, h_ref) = refs
    else:
        (p_ref, mu_ref, wl_ref, vec_ref, seg_ref, tri_ref,
         cm_ref, y_ref, vout_ref, prev_ref, h_ref) = refs
    @pl.when(pl.program_id(0) == 0)
    def _():
        prev_ref[...] = jnp.zeros_like(prev_ref)
        h_ref[...] = jnp.zeros_like(h_ref)
        if has_vres:
            pprev_ref[...] = jnp.zeros_like(pprev_ref)

    nb, nt = p_ref.shape[0], p_ref.shape[1]

    def token_shift(src_ref, carry_ref, mix):
        out = []
        for bi in range(nb):
            cur = src_ref[bi]
            prev = _shift_rows(cur, carry_ref[bi:bi + 1, :])
            carry_ref[bi:bi + 1, :] = cur[nt - 1:nt, :]
            out.append(cur + (prev - cur) * mix)
        return jnp.concatenate(out, axis=0)

    x = token_shift(p_ref, prev_ref, mu_ref[...])
    r = x[:, 0:RWKV_DIM]
    k = x[:, RWKV_DIM:2 * RWKV_DIM]
    v = x[:, 2 * RWKV_DIM:3 * RWKV_DIM]
    z = x[:, 3 * RWKV_DIM:]
    lane = lax.broadcasted_iota(jnp.int32, (1, LANES), 1)
    zt = jnp.where(lane < 32, jnp.tanh(z), jnp.where(lane < 64, z, jax.nn.sigmoid(z)))
    lo = _bdot(zt, wl_ref[...])
    w0, a0, k_k, k_a, r_k, ln_w, ln_b, v0 = (vec_ref[i:i + 1, :] for i in range(8))
    seg = seg_ref[...]

    zz = -(w0 + lo[:, 0:RWKV_DIM])
    softplus = jnp.maximum(zz, 0.0) + jnp.log(1.0 + jnp.exp(-jnp.abs(zz)))
    lw = -jnp.exp(-softplus - 0.5)
    a = jax.nn.sigmoid(a0 + lo[:, RWKV_DIM:2 * RWKV_DIM])
    g = lo[:, 2 * RWKV_DIM:]
    if has_vres:
        xv = token_shift(pv_ref, pprev_ref, vmu_ref[...])
        v_first = vf_ref[...].reshape(nb * nt, RWKV_DIM)
        v = v + (v_first - v) * jax.nn.sigmoid(v0 + _bdot(xv, v2_ref[...]))
    else:
        vout_ref[...] = v.reshape(nb, nt, RWKV_DIM)
    kk = k * k_k
    kk = kk * jnp.minimum(lax.rsqrt(_bdot(kk * kk, seg)), 1e12)
    kmod = k * (1.0 + (a - 1.0) * k_a)
    bonus = _bdot(r * kmod * r_k, seg) * v

    tri = tri_ref[...]
    lw_hi = lw.astype(BF16)
    lw_lo = (lw - lw_hi.astype(F32)).astype(BF16)
    cum = (jnp.dot(tri, lw_hi, preferred_element_type=F32)
           + jnp.dot(tri, lw_lo, preferred_element_type=F32))
    pinv = jnp.exp(-cum)
    r_t = r * jnp.exp(cum)
    a_t = -kk * jnp.exp(cum - lw)
    b_t = kk * a * pinv
    k_t = kmod * pinv
    p_ends = [jnp.exp(cum[bi * nt + nt - 1:(bi + 1) * nt, :]) for bi in range(nb)]
    p_end_rows = jnp.concatenate([jnp.broadcast_to(pe, (nt, RWKV_DIM)) for pe in p_ends], axis=0)
    bh_t = b_t * p_end_rows
    kh_t = k_t * p_end_rows

    strict, incl, blk8, eye = (cm_ref[i] for i in range(4))
    m0 = (lane < HEAD_DIM).astype(F32)
    m1 = 1.0 - m0

    def stack(t, rows, sl):
        ts = t[rows, sl]
        return jnp.concatenate([ts * m0, ts * m1], axis=0)

    npair = RWKV_DIM // PAIR
    chains = [(bi, j) for bi in range(nb) for j in range(npair)]
    each = lambda fn, *cols: [fn(*args) for args in zip(*cols)]
    cat0 = lambda *ts: jnp.concatenate(ts, axis=0)
    cat1 = lambda *ts: jnp.concatenate(ts, axis=1)

    def stacks(t):
        return [stack(t, slice(bi * nt, (bi + 1) * nt), slice(j * PAIR, (j + 1) * PAIR))
                for bi, j in chains]

    a_st, r_st, b_st, k_st, v_st, bh_st, kh_st = (stacks(t) for t in (a_t, r_t, b_t, k_t, v, bh_t, kh_t))
    gram = each(lambda a_, r_, b_, k_: _bdot_nt(cat0(a_, r_), cat0(b_, k_)), a_st, r_st, b_st, k_st)
    n2 = 2 * nt
    a_ab = [gm[:n2, :n2] * strict for gm in gram]
    a_ak = [gm[:n2, n2:] * strict for gm in gram]
    a_rb = [gm[n2:, :n2] * incl for gm in gram]
    a_rk = [gm[n2:, n2:] * incl for gm in gram]
    dg = [m * blk8 for m in a_ab]
    off = each(lambda m, d_: m - d_, a_ab, dg)
    d2 = each(_bdot, dg, dg)
    tdiag = [eye + d_ for d_ in dg]
    tdiag = each(lambda t_, d_: t_ + _bdot(t_, d_), tdiag, d2)
    d4 = each(_bdot, d2, d2)
    tdiag = each(lambda t_, d_: t_ + _bdot(t_, d_), tdiag, d4)
    e1 = each(_bdot, tdiag, off)
    e2 = each(_bdot, e1, e1)
    yb = [eye + e_ for e_ in e1]
    yb = each(lambda y_, e_: y_ + _bdot(y_, e_), yb, e2)
    e4 = each(_bdot, e2, e2)
    yb = each(lambda y_, e_: y_ + _bdot(y_, e_), yb, e4)
    tinv = each(_bdot, yb, tdiag)

    hs = [h_ref[bi, j] for bi, j in chains]
    xs = each(lambda a_, ak_, h_, v_: _bdot(cat1(a_, ak_), cat0(h_, v_)), a_st, a_ak, hs, v_st)
    us = each(_bdot, tinv, xs)
    y_st = each(lambda r_, rb_, rk_, h_, u_, v_: _bdot(cat1(r_, rb_, rk_), cat0(h_, u_, v_)),
                r_st, a_rb, a_rk, hs, us, v_st)
    pe_col = [jnp.sum(eye * p_ends[bi][:, j * PAIR:(j + 1) * PAIR], axis=1, keepdims=True)
              for bi, j in chains]
    h_new = each(lambda pc_, h_, bh_, kh_, u_, v_: pc_ * h_ + _bdot(cat0(bh_, kh_).T, cat0(u_, v_)),
                 pe_col, hs, bh_st, kh_st, us, v_st)
    for (bi, j), hn in zip(chains, h_new):
        h_ref[bi, j] = hn
    y_pair = [ys_[:nt] + ys_[nt:] for ys_ in y_st]
    y = cat0(*[cat1(*y_pair[bi * npair:(bi + 1) * npair]) for bi in range(nb)])
    inv_n = 1.0 / HEAD_DIM
    mean = _bdot(y, seg) * inv_n
    dy = y - mean
    var = _bdot(dy * dy, seg) * inv_n
    yn = dy * lax.rsqrt(var + GN_EPS) * ln_w + ln_b
    y_ref[...] = ((yn + bonus) * g).reshape(nb, nt, RWKV_DIM).astype(y_ref.dtype)


def _rwkv(p_rwkv, mu, wl, vecs, consts, vres=None):
    b, s, _ = p_rwkv.shape
    c = RWKV_CHUNK
    seg, tri, cm = consts
    tile = lambda wd: pl.BlockSpec((b, c, wd), lambda j: (0, j, 0))
    full = lambda arr: pl.BlockSpec(arr.shape, lambda j: (0,) * arr.ndim)
    y_shape = jax.ShapeDtypeStruct((b, s, RWKV_DIM), BF16)
    scratch = [pltpu.VMEM((b, RWKV_IN), F32)]
    if vres is None:
        ins = [p_rwkv, mu, wl, vecs, seg, tri, cm]
        in_specs = [tile(RWKV_IN)] + [full(t) for t in ins[1:]]
        out_shape = [y_shape, jax.ShapeDtypeStruct((b, s, RWKV_DIM), F32)]
        out_specs = [tile(RWKV_DIM), tile(RWKV_DIM)]
    else:
        p_vres, v_first, vmu, v2 = vres
        ins = [p_rwkv, p_vres, v_first, mu, vmu, wl, v2, vecs, seg, tri, cm]
        in_specs = [tile(RWKV_IN), tile(VRES_PAD), tile(RWKV_DIM)] + [full(t) for t in ins[3:]]
        out_shape = [y_shape]
        out_specs = [tile(RWKV_DIM)]
        scratch.append(pltpu.VMEM((b, VRES_PAD), F32))
    scratch.append(pltpu.VMEM((b, RWKV_DIM // PAIR, PAIR, PAIR), F32))
    return pl.pallas_call(
        functools.partial(_rwkv_body, has_vres=vres is not None),
        grid=(s // c,),
        in_specs=in_specs, out_specs=out_specs, out_shape=out_shape,
        scratch_shapes=scratch,
        compiler_params=_params("arbitrary"),
        name="rwkv7",
    )(*ins)


def _conv_body(p_ref, w_ref, gain_ref, seg_ref, o_ref, carry_ref):
    @pl.when(pl.program_id(1) == 0)
    def _():
        carry_ref[...] = jnp.zeros_like(carry_ref)

    p = p_ref[...]
    nt = p.shape[0]
    u = p[:, CONV_DIM:2 * CONV_DIM] * p[:, 2 * CONV_DIM:]
    carry = carry_ref[...]
    u1 = _shift_rows(u, carry[1:2, :], 1)
    u2 = _shift_rows(u, carry, 2)
    carry_ref[...] = u[nt - 2:nt, :]
    w = w_ref[...]
    y = p[:, 0:CONV_DIM] * (w[0:1, :] * u2 + w[1:2, :] * u1 + w[2:3, :] * u)
    ms = _bdot(y * y, seg_ref[...]) * (1.0 / HEAD_DIM)
    o_ref[...] = (y * lax.rsqrt(ms + NORM_EPS) * gain_ref[...]).astype(o_ref.dtype)


def _short_conv(p_conv, w, gain, seg, tm):
    b, s, _ = p_conv.shape
    return pl.pallas_call(
        _conv_body,
        grid=(b, s // tm),
        in_specs=[pl.BlockSpec((None, tm, CONV_IN), lambda i, j: (i, j, 0)),
                  pl.BlockSpec(w.shape, lambda i, j: (0, 0)),
                  pl.BlockSpec(gain.shape, lambda i, j: (0, 0)),
                  pl.BlockSpec(seg.shape, lambda i, j: (0, 0))],
        out_specs=pl.BlockSpec((None, tm, CONV_DIM), lambda i, j: (i, j, 0)),
        out_shape=jax.ShapeDtypeStruct((b, s, CONV_DIM), BF16),
        scratch_shapes=[pltpu.VMEM((CONV_WIDTH - 1, CONV_DIM), F32)],
        compiler_params=_params("arbitrary", "arbitrary"),
        name="short_conv",
    )(p_conv, w, gain, seg)


def _attn_prep_body(p_ref, cos_ref, sin_ref, qg_ref, kg_ref, seg_ref, q_ref, k_ref, v_ref):
    p = p_ref[...]
    seg = seg_ref[...]
    lane = lax.broadcasted_iota(jnp.int32, (1, LANES), 1)
    first = (lane & (HEAD_DIM - 1)) < HEAD_DIM // 2
    for part, g_ref, o_ref, scale in ((0, qg_ref, q_ref, HEAD_DIM ** -0.5), (1, kg_ref, k_ref, 1.0)):
        x = p[:, part * ATTN_DIM:(part + 1) * ATTN_DIM]
        ms = _bdot(x * x, seg) * (1.0 / HEAD_DIM)
        xn = x * lax.rsqrt(ms + NORM_EPS) * g_ref[...]
        rot = []
        for j in range(ATTN_DIM // LANES):
            xs = xn[:, j * LANES:(j + 1) * LANES]
            rot.append(jnp.where(first, pltpu.roll(xs, LANES - HEAD_DIM // 2, 1),
                                 pltpu.roll(xs, HEAD_DIM // 2, 1)))
        out = (xn * cos_ref[...] + jnp.concatenate(rot, axis=1) * sin_ref[...]) * scale
        for j in range(ATTN_DIM // PAIR):
            o_ref[j] = out[:, j * PAIR:(j + 1) * PAIR]
    for j in range(ATTN_DIM // PAIR):
        v_ref[j] = p[:, 2 * ATTN_DIM + j * PAIR:2 * ATTN_DIM + (j + 1) * PAIR]


def _pair_major_spec(rows, row_index):
    return pl.BlockSpec((None, ATTN_DIM // PAIR, rows, PAIR), lambda i, j: (i, 0, row_index(j), 0))


def _attn_prep(p_attn, cos, sin, qg, kg, seg, tm):
    b, s, _ = p_attn.shape
    tab = pl.BlockSpec((tm, ATTN_DIM), lambda i, j: (j, 0))
    full = lambda arr: pl.BlockSpec(arr.shape, lambda i, j: (0, 0))
    shp = jax.ShapeDtypeStruct((b, ATTN_DIM // PAIR, s, PAIR), F32)
    return pl.pallas_call(
        _attn_prep_body,
        grid=(b, s // tm),
        in_specs=[pl.BlockSpec((None, tm, ATTN_IN), lambda i, j: (i, j, 0)), tab, tab,
                  full(qg), full(kg), full(seg)],
        out_specs=[_pair_major_spec(tm, lambda j: j)] * 3,
        out_shape=[shp] * 3,
        compiler_params=_params("arbitrary", "arbitrary"),
        name="attn_prep",
    )(p_attn, cos, sin, qg, kg, seg)


def _attn_body(q_ref, k_ref, v_ref, kprev_ref, vprev_ref, o_ref, l_ref, *, dil):
    span = pl.program_id(1)
    blk = ATTN_BLOCK
    npair = ATTN_DIM // PAIR
    qi = lax.broadcasted_iota(jnp.int32, (blk, 2 * blk), 0)
    kj = lax.broadcasted_iota(jnp.int32, (blk, 2 * blk), 1)
    dist = qi + blk - kj
    band = (dist >= 0) & (dist <= ATTN_BLOCK)
    lane = lax.broadcasted_iota(jnp.int32, (1, LANES), 1)
    head0 = lane < HEAD_DIM
    m0 = head0.astype(F32)
    heads = [(j, hm) for j in range(npair) for hm in (m0, 1.0 - m0)]

    def block_attn(q, kcat, vcat, block_index):
        mask = band & (kj + block_index * (2 * blk) >= blk)
        s_all = [_bdot_nt(q[j] * hm, kcat[j]) for j, hm in heads]
        s_all = [jnp.where(mask, s, -jnp.inf) for s in s_all]
        m_all = [jnp.max(s, axis=-1, keepdims=True) for s in s_all]
        pe_all = [jnp.exp(s - m) for s, m in zip(s_all, m_all)]
        den_all = [jnp.sum(pe, axis=-1, keepdims=True) for pe in pe_all]
        o_all = [_bdot(pe, vcat[j]) * (1.0 / den) for pe, den, (j, _) in zip(pe_all, den_all, heads)]
        lse_all = [m + jnp.log(den) for m, den in zip(m_all, den_all)]
        o_pair = [jnp.where(head0, o_all[2 * j], o_all[2 * j + 1]) for j in range(npair)]
        l_pair = [jnp.where(head0, lse_all[2 * j], lse_all[2 * j + 1]) for j in range(npair)]
        return o_pair, l_pair

    if dil == 1:
        nq = q_ref.shape[1] // blk
        kext = [jnp.concatenate([kprev_ref[j], k_ref[j]], axis=0) for j in range(npair)]
        vext = [jnp.concatenate([vprev_ref[j], v_ref[j]], axis=0) for j in range(npair)]
        for qb in range(nq):
            rows = slice(qb * blk, (qb + 1) * blk)
            keys = slice(qb * blk, (qb + 2) * blk)
            o_pair, l_pair = block_attn([q_ref[j, rows, :] for j in range(npair)],
                                        [kx[keys] for kx in kext], [vx[keys] for vx in vext],
                                        span * nq + qb)
            for j in range(npair):
                o_ref[j, rows, :] = o_pair[j]
                l_ref[j, rows, :] = l_pair[j]
    else:
        def one_class(r, carry):
            rows = pl.ds(r, blk, stride=dil)
            cat = lambda prev_ref, ref: [jnp.concatenate([prev_ref[j, rows, :], ref[j, rows, :]], axis=0)
                                         for j in range(npair)]
            o_pair, l_pair = block_attn([q_ref[j, rows, :] for j in range(npair)],
                                        cat(kprev_ref, k_ref), cat(vprev_ref, v_ref), span)
            for j in range(npair):
                o_ref[j, rows, :] = o_pair[j]
                l_ref[j, rows, :] = l_pair[j]
            return carry
        lax.fori_loop(0, dil, one_class, 0)


def _attn(q, k, v, dil):
    b, _, s, _ = q.shape
    if dil == 1:
        nq = 4
        span = ATTN_BLOCK * nq
        prev = _pair_major_spec(ATTN_BLOCK, lambda m: jnp.maximum(m * nq - 1, 0))
    else:
        span = ATTN_BLOCK * dil
        prev = _pair_major_spec(span, lambda m: jnp.maximum(m - 1, 0))
    spec = _pair_major_spec(span, lambda m: m)
    shp = jax.ShapeDtypeStruct(q.shape, F32)
    return pl.pallas_call(
        functools.partial(_attn_body, dil=dil),
        grid=(b, s // span),
        in_specs=[spec, spec, spec, prev, prev],
        out_specs=[spec, spec],
        out_shape=[shp, shp],
        compiler_params=_params("arbitrary", "arbitrary"),
        name=f"attn_d{dil}",
    )(q, k, v, k, v)


def _attn_combine_body(*refs):
    n = len(DILATIONS)
    o_refs, l_refs = refs[:n], refs[n:2 * n]
    gain_ref, seg_ref, y_ref = refs[2 * n:]
    for j in range(ATTN_DIM // PAIR):
        sl = slice(j * PAIR, (j + 1) * PAIR)
        ls = [r[j] for r in l_refs]
        m = functools.reduce(jnp.maximum, ls)
        es = [jnp.exp(l - m) for l in ls]
        den = functools.reduce(jnp.add, es)
        o = functools.reduce(jnp.add, [e * r[j] for e, r in zip(es, o_refs)]) / den
        ms = _bdot(o * o, seg_ref[...]) * (1.0 / HEAD_DIM)
        y_ref[:, sl] = (o * lax.rsqrt(ms + NORM_EPS) * gain_ref[:, sl]).astype(y_ref.dtype)


def _attn_combine(os_, ls_, gain, seg, tm):
    b, _, s, _ = os_[0].shape
    tile = _pair_major_spec(tm, lambda j: j)
    full = lambda arr: pl.BlockSpec(arr.shape, lambda i, j: (0, 0))
    return pl.pallas_call(
        _attn_combine_body,
        grid=(b, s // tm),
        in_specs=[tile] * (2 * len(os_)) + [full(gain), full(seg)],
        out_specs=pl.BlockSpec((None, tm, ATTN_DIM), lambda i, j: (i, j, 0)),
        out_shape=jax.ShapeDtypeStruct((b, s, ATTN_DIM), BF16),
        compiler_params=_params("arbitrary", "arbitrary"),
        name="attn_combine",
    )(*os_, *ls_, gain, seg)


INFO_E0, INFO_E1, INFO_R0, INFO_R1, INFO_G0, INFO_G1 = range(6)


def _outproj_body(*refs, has_router):
    if has_router:
        (x_ref, yr_ref, yc_ref, ya_ref, wo_ref, g_ref, rwh_ref, rwl_ref, tri_ref,
         xn_ref, h_ref, info_ref, cnt_ref) = refs
    else:
        x_ref, yr_ref, yc_ref, ya_ref, wo_ref, g_ref, xn_ref, h_ref = refs
    o1, o2 = RWKV_DIM, RWKV_DIM + CONV_DIM
    acc = jnp.dot(yr_ref[...], wo_ref[0:o1, :], preferred_element_type=F32)
    acc += jnp.dot(yc_ref[...], wo_ref[o1:o2, :], preferred_element_type=F32)
    acc += jnp.dot(ya_ref[...], wo_ref[o2:, :], preferred_element_type=F32)
    xn = x_ref[...] + acc
    xn_ref[...] = xn
    h = xn * lax.rsqrt(jnp.mean(xn * xn, axis=-1, keepdims=True) + NORM_EPS) * g_ref[...]
    h_ref[...] = h.astype(h_ref.dtype)
    if not has_router:
        return

    h_hi = h.astype(BF16)
    h_lo = (h - h_hi.astype(F32)).astype(BF16)
    rwh = rwh_ref[...]
    logits = (jnp.dot(h_hi, rwh, preferred_element_type=F32)
              + jnp.dot(h_lo, rwh, preferred_element_type=F32)
              + jnp.dot(h_hi, rwl_ref[...], preferred_element_type=F32))
    lane = lax.broadcasted_iota(jnp.int32, (1, LANES), 1)
    logits = jnp.where(lane < N_EXPERTS, logits, -jnp.inf)

    def top(lg):
        m = jnp.max(lg, axis=-1, keepdims=True)
        idx = jnp.min(jnp.where(lg == m, lane, LANES), axis=-1, keepdims=True)
        return m, idx, lane == idx

    m1, i1, oh1 = top(logits)
    m2, i2, oh2 = top(jnp.where(oh1, -jnp.inf, logits))
    zexp = jnp.exp(m2 - m1)
    g0 = 1.0 / (1.0 + zexp)
    g1 = zexp * g0
    oh = oh1.astype(F32) + oh2.astype(F32)
    tot = jnp.dot(tri_ref[...], oh.astype(BF16), preferred_element_type=F32)
    r0 = jnp.sum(jnp.where(oh1, tot, 0.0), axis=-1, keepdims=True)
    r1 = jnp.sum(jnp.where(oh2, tot, 0.0), axis=-1, keepdims=True)
    cnt_ref[...] = jnp.broadcast_to(jnp.sum(oh, axis=0, keepdims=True), cnt_ref.shape)
    info = jnp.zeros(logits.shape, F32)
    for ln, val in ((INFO_E0, i1.astype(F32)), (INFO_E1, i2.astype(F32)), (INFO_R0, r0),
                    (INFO_R1, r1), (INFO_G0, g0), (INFO_G1, g1)):
        info = jnp.where(lane == ln, val, info)
    info_ref[...] = info


def _outproj(x2d, yr, yc, ya, wo, gain, tm, router=None):
    t, d = x2d.shape
    row = lambda wd: pl.BlockSpec((tm, wd), lambda i: (i, 0))
    full = lambda arr: pl.BlockSpec(arr.shape, lambda i: (0, 0))
    ins = [x2d, yr, yc, ya, wo, gain]
    in_specs = [row(d), row(RWKV_DIM), row(CONV_DIM), row(ATTN_DIM), full(wo), full(gain)]
    out_shape = [jax.ShapeDtypeStruct((t, d), F32)]
    out_specs = [row(d)]
    scratch = []
    if router is None:
        out_shape.append(jax.ShapeDtypeStruct((t, d), BF16))
        out_specs.append(row(d))
    else:
        ins += list(router)
        in_specs += [full(a) for a in router]
        out_shape += [jax.ShapeDtypeStruct((t, d), F32), jax.ShapeDtypeStruct((t, LANES), F32),
                      jax.ShapeDtypeStruct((t // tm * 8, LANES), F32)]
        out_specs += [row(d), row(LANES), pl.BlockSpec((8, LANES), lambda i: (i, 0))]
    return pl.pallas_call(
        functools.partial(_outproj_body, has_router=router is not None),
        grid=(t // tm,),
        in_specs=in_specs, out_specs=out_specs, out_shape=out_shape,
        scratch_shapes=scratch,
        compiler_params=_params("arbitrary"),
        name="outproj",
    )(*ins)


def _ffn_body(te_ref, nu_ref, x_ref, wg_ref, wu_ref, wd_ref, *rest, has_res):
    if has_res:
        res_ref, o_ref, acc_ref = rest
    else:
        o_ref, acc_ref = rest
    i, f = pl.program_id(0), pl.program_id(1)

    @pl.when(f == 0)
    def _():
        acc_ref[...] = jnp.zeros_like(acc_ref)

    @pl.when(i < nu_ref[0])
    def _():
        x = x_ref[...].astype(BF16)
        hg = jnp.dot(x, wg_ref[...], preferred_element_type=F32)
        hu = jnp.dot(x, wu_ref[...], preferred_element_type=F32)
        act = hg * jax.nn.sigmoid(hg) * hu
        acc_ref[...] += jnp.dot(act.astype(BF16), wd_ref[...], preferred_element_type=F32)

    @pl.when(f == pl.num_programs(1) - 1)
    def _():
        out = acc_ref[...]
        if has_res:
            out = out + res_ref[...]
        o_ref[...] = out.astype(o_ref.dtype)


def _ffn(x, wg, wu, wd, tile_expert, n_used, tm, tf, res=None):
    n, d = x.shape
    ff = wg.shape[-1]
    row = pl.BlockSpec((tm, d), lambda i, f, te, nu: (i, 0))
    ins = [x, wg, wu, wd]
    in_specs = [row,
                pl.BlockSpec((None, d, tf), lambda i, f, te, nu: (te[i], 0, f)),
                pl.BlockSpec((None, d, tf), lambda i, f, te, nu: (te[i], 0, f)),
                pl.BlockSpec((None, tf, d), lambda i, f, te, nu: (te[i], f, 0))]
    if res is not None:
        ins.append(res)
        in_specs.append(row)
    return pl.pallas_call(
        functools.partial(_ffn_body, has_res=res is not None),
        grid_spec=pltpu.PrefetchScalarGridSpec(
            num_scalar_prefetch=2,
            grid=(n // tm, ff // tf),
            in_specs=in_specs,
            out_specs=row,
            scratch_shapes=[pltpu.VMEM((tm, d), F32)]),
        out_shape=jax.ShapeDtypeStruct((n, d), F32),
        compiler_params=_params("arbitrary", "arbitrary"),
        name="swiglu",
    )(tile_expert, n_used, *ins)


def _row_copy(src, s, dst, d, sem):
    return pltpu.make_async_copy(src.at[pl.ds(s, 1)], dst.at[pl.ds(d, 1)], sem)


DISPATCH_CHUNK = 128
SEG_ALIGN = 8


def _dispatch_body(base_ref, loff_ref, nch_ref, h_ref, info_ref, basev_ref, loffv_ref, xs_in,
                   xs_out, dest_ref, stage_ref, sem):
    del xs_in
    i = pl.program_id(0)
    tm = h_ref.shape[0]
    n_stage = stage_ref.shape[0] - DISPATCH_CHUNK

    @pl.when(i == 0)
    def _():
        stage_ref[pl.ds(n_stage, DISPATCH_CHUNK), :] = jnp.zeros((DISPATCH_CHUNK, stage_ref.shape[1]), F32)

    info = info_ref[...]
    lane = lax.broadcasted_iota(jnp.int32, (1, LANES), 1)
    lane_f = lane.astype(F32)
    pos, dest = [], []
    for e_ln, r_ln in ((INFO_E0, INFO_R0), (INFO_E1, INFO_R1)):
        onehot = lane_f == info[:, e_ln:e_ln + 1]
        rank = info[:, r_ln:r_ln + 1]
        pick = lambda v: jnp.sum(jnp.where(onehot, v, 0.0), axis=-1, keepdims=True)
        pos.append(pick(loffv_ref[0:1, :]) + rank)
        dest.append(pick(basev_ref[0:1, :]) + rank)
    dest_ref[...] = jnp.where(lane == 0, dest[0], jnp.where(lane == 1, dest[1], 0.0))
    pos_t = jnp.transpose(jnp.where(lane == 0, pos[0], jnp.where(lane == 1, pos[1], -1.0)))
    rows = lax.broadcasted_iota(jnp.int32, (n_stage, 1), 0).astype(F32)
    perm = ((rows == pos_t[0:1, :]) | (rows == pos_t[1:2, :])).astype(BF16)
    stage_ref[pl.ds(0, n_stage), :] = jnp.dot(perm, h_ref[...].astype(BF16),
                                              preferred_element_type=F32)

    max_chunks = tm // DISPATCH_CHUNK

    def chunk_copy(e, c):
        src = pl.multiple_of(loff_ref[i * N_EXPERTS + e] + c * DISPATCH_CHUNK, SEG_ALIGN)
        dst = pl.multiple_of(base_ref[i * N_EXPERTS + e] + c * DISPATCH_CHUNK, SEG_ALIGN)
        return pltpu.make_async_copy(stage_ref.at[pl.ds(src, DISPATCH_CHUNK)],
                                     xs_out.at[pl.ds(dst, DISPATCH_CHUNK)], sem)

    for start in (True, False):
        for e in range(N_EXPERTS):
            for c in range(max_chunks):
                @pl.when(c < nch_ref[i * N_EXPERTS + e])
                def _():
                    if start:
                        chunk_copy(e, c).start()
                    else:
                        chunk_copy(e, c).wait()


def _dispatch(h2, info, base, loff, nch, n_rows, tm):
    t, d = h2.shape
    n_tt = t // tm
    n_stage = 2 * tm + N_EXPERTS * SEG_ALIGN
    row = lambda wd: pl.BlockSpec((tm, wd), lambda i, *_: (i, 0))
    vec = pl.BlockSpec((8, LANES), lambda i, *_: (i, 0))
    lanes8 = lambda a: jnp.repeat(jnp.pad(a.astype(F32), ((0, 0), (0, LANES - N_EXPERTS))), 8, axis=0)
    flat = lambda a: a.reshape(-1).astype(jnp.int32)
    any_spec = pl.BlockSpec(memory_space=pl.ANY)
    return pl.pallas_call(
        _dispatch_body,
        grid_spec=pltpu.PrefetchScalarGridSpec(
            num_scalar_prefetch=3,
            grid=(n_tt,),
            in_specs=[row(d), row(LANES), vec, vec, any_spec],
            out_specs=[any_spec, row(LANES)],
            scratch_shapes=[pltpu.VMEM((n_stage + DISPATCH_CHUNK, d), F32),
                            pltpu.SemaphoreType.DMA]),
        out_shape=[jax.ShapeDtypeStruct((n_rows, d), F32), jax.ShapeDtypeStruct((t, LANES), F32)],
        input_output_aliases={7: 0},
        compiler_params=_params("arbitrary"),
        name="moe_dispatch",
    )(flat(base), flat(loff), flat(nch), h2, info, lanes8(base), lanes8(loff),
      jnp.zeros((n_rows, d), F32))


def _combine_body(dest_ref, destn_ref, x_ref, info_ref, ys_hbm, o_ref, buf, sem):
    i = pl.program_id(0)
    ntok = x_ref.shape[0]
    slot = lax.rem(i, 2)

    def gather(idx_ref, sl):
        def issue(t, carry):
            for kq in range(2):
                _row_copy(ys_hbm, idx_ref[0, 0, 2 * t + kq], buf.at[sl, kq], t,
                          sem.at[sl]).start(priority=kq)
            return carry
        lax.fori_loop(0, ntok, issue, 0, unroll=8)

    @pl.when(i == 0)
    def _():
        gather(dest_ref, 0)

    @pl.when(i + 1 < pl.num_programs(0))
    def _():
        gather(destn_ref, 1 - slot)

    for kq in range(2):
        pltpu.make_async_copy(ys_hbm.at[pl.ds(0, ntok)], buf.at[slot, kq], sem.at[slot]).wait()
    info = info_ref[...]
    g0 = info[:, INFO_G0:INFO_G0 + 1]
    g1 = info[:, INFO_G1:INFO_G1 + 1]
    o_ref[...] = x_ref[...] + g0 * buf[slot, 0] + g1 * buf[slot, 1]


def _combine(xn, info, ys, dest, tm):
    t, d = xn.shape
    n = t // tm
    row = lambda wd: pl.BlockSpec((tm, wd), lambda i: (i, 0))
    smem = lambda fn: pl.BlockSpec((1, 1, 2 * tm), fn, memory_space=pltpu.SMEM)
    idx = dest.reshape(n, 1, 2 * tm)
    return pl.pallas_call(
        _combine_body,
        grid=(n,),
        in_specs=[smem(lambda i: (i, 0, 0)), smem(lambda i: (jnp.minimum(i + 1, n - 1), 0, 0)),
                  row(d), row(LANES), pl.BlockSpec(memory_space=pl.ANY)],
        out_specs=row(d),
        out_shape=jax.ShapeDtypeStruct((t, d), F32),
        scratch_shapes=[pltpu.VMEM((2, 2, tm, d), F32), pltpu.SemaphoreType.DMA((2,))],
        compiler_params=_params("arbitrary"),
        name="moe_combine",
    )(idx, idx, xn, info, ys)


def _seg_matrix(n):
    i = jnp.arange(n) // HEAD_DIM
    return (i[:, None] == i[None, :]).astype(BF16)


def _rwkv_consts(nb):
    c = RWKV_CHUNK
    t = jnp.arange(nb * c)
    tri = ((t[:, None] >= t[None, :]) & (t[:, None] // c == t[None, :] // c)).astype(BF16)
    i = jnp.arange(2 * c)
    same = (i[:, None] // c) == (i[None, :] // c)
    ti, tj = (i % c)[:, None], (i % c)[None, :]
    strict = same & (ti > tj)
    incl = same & (ti >= tj)
    blk8 = (i[:, None] // 8) == (i[None, :] // 8)
    eye = i[:, None] == i[None, :]
    cm = jnp.stack([strict, incl, blk8, eye]).astype(F32)
    return _seg_matrix(RWKV_DIM), tri, cm


def _rope_tables(s):
    half = HEAD_DIM // 2
    inv_freq = ROPE_THETA ** (-jnp.arange(half, dtype=F32) * 2.0 / HEAD_DIM)
    ang = jnp.arange(s, dtype=F32)[:, None] * inv_freq[None, :]
    cos, sin = jnp.cos(ang), jnp.sin(ang)
    reps = ATTN_DIM // HEAD_DIM
    cos_t = jnp.tile(jnp.concatenate([cos, cos], axis=1), (1, reps))
    sin_t = jnp.tile(jnp.concatenate([-sin, sin], axis=1), (1, reps))
    return cos_t, sin_t


def _row_tile(n, want):
    return want if n % want == 0 else n


def kernel(x, mix_norm, w_in, tshift_mu, vres_w_in, vres_mu, vres_v0, vres_v2, decay_w0, decay_w2, iclr_a0, iclr_a2, gate_g2, k_k, k_a, r_k, ln_x_w, ln_x_b, conv_w, conv_out_norm, q_norm, k_norm, attn_out_norm, w_o, ffn_norm, dense_wg, dense_wu, dense_wd, router, moe_wg, moe_wu, moe_wd):
    b, s, d = x.shape
    assert d == D_MODEL and s % DIL_SPAN == 0
    t = b * s
    depth = w_in.shape[0]
    tm = _row_tile(s, 512)
    seg384 = _seg_matrix(RWKV_DIM)
    seg256 = _seg_matrix(CONV_DIM)
    rwkv_consts = _rwkv_consts(b)
    cos_t, sin_t = _rope_tables(s)
    zeros_r = jnp.zeros((RWKV_DIM,), F32)

    x2d = x.reshape(t, d)
    v_first = None
    for l in range(depth):
        w_comb = w_in[l]
        widths = [RWKV_IN, CONV_IN, ATTN_IN]
        if l > 0:
            w_comb = jnp.concatenate(
                [w_comb, jnp.pad(vres_w_in[l - 1], ((0, 0), (0, VRES_PAD - VRES_LORA)))], axis=1)
            widths.append(VRES_PAD)
        outs = _inproj(x2d, mix_norm[l][None, :], w_comb.astype(BF16), widths, tm)
        p_rwkv, p_conv, p_attn = (o.reshape(b, s, -1) for o in outs[:3])

        wl = jnp.zeros((LORA_W, 3 * RWKV_DIM), F32)
        wl = wl.at[0:32, 0:RWKV_DIM].set(decay_w2[l])
        wl = wl.at[32:64, RWKV_DIM:2 * RWKV_DIM].set(iclr_a2[l])
        wl = wl.at[64:128, 2 * RWKV_DIM:].set(gate_g2[l]).astype(BF16)
        vecs = jnp.stack([decay_w0[l], iclr_a0[l], k_k[l], k_a[l], r_k[l].reshape(-1), ln_x_w[l],
                          ln_x_b[l], vres_v0[l - 1] if l > 0 else zeros_r])
        mu = tshift_mu[l][None, :]
        if l == 0:
            y_rwkv, v_first = _rwkv(p_rwkv, mu, wl, vecs, rwkv_consts)
        else:
            vmu = jnp.pad(vres_mu[l - 1], (0, VRES_PAD - VRES_LORA))[None, :]
            v2 = jnp.pad(vres_v2[l - 1], ((0, VRES_PAD - VRES_LORA), (0, 0))).astype(BF16)
            (y_rwkv,) = _rwkv(p_rwkv, mu, wl, vecs, rwkv_consts,
                              vres=(outs[3].reshape(b, s, VRES_PAD), v_first, vmu, v2))

        y_conv = _short_conv(p_conv, conv_w[l], conv_out_norm[l].reshape(1, -1), seg256, tm)

        reps = ATTN_DIM // HEAD_DIM
        q, k, v = _attn_prep(p_attn, cos_t, sin_t, jnp.tile(q_norm[l], reps)[None, :],
                             jnp.tile(k_norm[l], reps)[None, :], seg384, tm)
        res = [_attn(q, k, v, dil) for dil in DILATIONS]
        y_attn = _attn_combine([r[0] for r in res], [r[1] for r in res],
                               attn_out_norm[l].reshape(1, -1), _seg_matrix(PAIR), tm)

        flat = lambda a: a.reshape(t, -1)
        wo = w_o[l].astype(BF16)
        gain = ffn_norm[l][None, :]
        i = l // 2
        if l % 2 == 0:
            xn, h2 = _outproj(x2d, flat(y_rwkv), flat(y_conv), flat(y_attn), wo, gain, tm)
            tmf = _row_tile(t, 1024)
            n_tiles = t // tmf
            x2d = _ffn(h2, dense_wg[i][None].astype(BF16), dense_wu[i][None].astype(BF16),
                       dense_wd[i][None].astype(BF16), jnp.zeros((n_tiles,), jnp.int32),
                       jnp.full((1,), n_tiles, jnp.int32), tmf, 512, res=xn)
        else:
            rw = jnp.pad(router[i], ((0, 0), (0, LANES - N_EXPERTS)))
            rw_hi = rw.astype(BF16)
            rw_lo = (rw - rw_hi.astype(F32)).astype(BF16)
            rt = jnp.arange(tm)
            tri = (rt[:, None] > rt[None, :]).astype(BF16)
            xn, h2, info, cnt = _outproj(x2d, flat(y_rwkv), flat(y_conv), flat(y_attn), wo, gain,
                                         tm, router=(rw_hi, rw_lo, tri))
            tme = 1024
            n_tt = t // tm
            cnt_te = cnt.reshape(n_tt, 8, LANES)[:, 0, :N_EXPERTS].astype(jnp.int32)
            seg = (cnt_te + SEG_ALIGN - 1) // SEG_ALIGN * SEG_ALIGN
            run = jnp.cumsum(seg, axis=0) - seg
            padded = (jnp.sum(seg, axis=0) + DISPATCH_CHUNK + tme - 1) // tme * tme
            ends = jnp.cumsum(padded)
            base = (ends - padded)[None, :] + run
            loff = jnp.cumsum(seg, axis=1) - seg
            nch = (seg + DISPATCH_CHUNK - 1) // DISPATCH_CHUNK
            n_rows = pl.cdiv(2 * t + n_tt * N_EXPERTS * (SEG_ALIGN - 1)
                             + N_EXPERTS * (DISPATCH_CHUNK + tme), tme) * tme
            n_tiles = n_rows // tme
            tile_start = jnp.arange(n_tiles, dtype=jnp.int32) * tme
            tile_expert = jnp.minimum(jnp.sum(ends[None, :] <= tile_start[:, None], axis=1),
                                      N_EXPERTS - 1).astype(jnp.int32)
            n_used = (ends[-1] // tme).astype(jnp.int32).reshape(1)
            xs, dest_l = _dispatch(h2, info, base, loff, nch, n_rows, tm)
            dest = dest_l[:, 0:2].astype(jnp.int32)
            ys = _ffn(xs, moe_wg[i].astype(BF16), moe_wu[i].astype(BF16), moe_wd[i].astype(BF16),
                      tile_expert, n_used, tme, 512)
            x2d = _combine(xn, info, ys, dest, _row_tile(s, 256))
    return x2d.reshape(b, s, d)
```

```python
import functools

import jax
import jax.numpy as jnp
from jax import lax
from jax.experimental import pallas as pl
from jax.experimental.pallas import tpu as pltpu

F32 = jnp.float32
BF16 = jnp.bfloat16

D_MODEL = 1024
HEAD_DIM = 64
RWKV_DIM = 384
CONV_DIM = 256
ATTN_DIM = 384
LORA_W = 128
RWKV_IN = 3 * RWKV_DIM + LORA_W
CONV_IN = 3 * CONV_DIM
ATTN_IN = 3 * ATTN_DIM
VRES_LORA = 16
VRES_PAD = 128
CONV_WIDTH = 3
DILATIONS = (1, 4, 16)
ATTN_BLOCK = 128
DIL_SPAN = ATTN_BLOCK * 16
ROPE_THETA = 10000.0
D_FF = 3584
N_EXPERTS = 8
NORM_EPS = 1e-6
GN_EPS = 64e-5

LANES = 128
RWKV_CHUNK = 64
PAIR = 2 * HEAD_DIM
VMEM_LIMIT = 56 * 1024 * 1024


def _params(*sem):
    return pltpu.CompilerParams(dimension_semantics=sem, vmem_limit_bytes=VMEM_LIMIT)


def _bdot(a, b):
    return jnp.dot(a.astype(BF16), b.astype(BF16), preferred_element_type=F32)


def _bdot_nt(a, b):
    return lax.dot_general(a.astype(BF16), b.astype(BF16), (((1,), (1,)), ((), ())),
                           preferred_element_type=F32)


def _shift_rows(x, carry_row, n=1):
    row = lax.broadcasted_iota(jnp.int32, (x.shape[0], 1), 0)
    out = pltpu.roll(x, n, 0)
    for i in range(n):
        out = jnp.where(row == i, carry_row[i:i + 1, :], out)
    return out


def _inproj_body(x_ref, g_ref, w_ref, *o_refs):
    x = x_ref[...]
    h = x * lax.rsqrt(jnp.mean(x * x, axis=-1, keepdims=True) + NORM_EPS) * g_ref[...]
    p = jnp.dot(h.astype(BF16), w_ref[...], preferred_element_type=F32)
    off = 0
    for o_ref in o_refs:
        n = o_ref.shape[-1]
        o_ref[...] = p[:, off:off + n].astype(o_ref.dtype)
        off += n


def _inproj(x2d, gain, w, widths, tm):
    t, d = x2d.shape
    n = w.shape[1]
    return pl.pallas_call(
        _inproj_body,
        grid=(t // tm,),
        in_specs=[pl.BlockSpec((tm, d), lambda i: (i, 0)),
                  pl.BlockSpec((1, d), lambda i: (0, 0)),
                  pl.BlockSpec((d, n), lambda i: (0, 0))],
        out_specs=[pl.BlockSpec((tm, wd), lambda i: (i, 0)) for wd in widths],
        out_shape=[jax.ShapeDtypeStruct((t, wd), F32) for wd in widths],
        compiler_params=_params("arbitrary"),
        name="inproj",
    )(x2d, gain, w)


def _rwkv_body(*refs, has_vres):
    if has_vres:
        (p_ref, pv_ref, vf_ref, mu_ref, vmu_ref, wl_ref, v2_ref, vec_ref, seg_ref, tri_ref,
         cm_ref, y_ref, prev_ref, pprev_ref, h_ref) = refs
    else:
        (p_ref, mu_ref, wl_ref, vec_ref, seg_ref, tri_ref,
         cm_ref, y_ref, vout_ref, prev_ref, h_ref) = refs
    @pl.when(pl.program_id(0) == 0)
    def _():
        prev_ref[...] = jnp.zeros_like(prev_ref)
        h_ref[...] = jnp.zeros_like(h_ref)
        if has_vres:
            pprev_ref[...] = jnp.zeros_like(pprev_ref)

    nb, nt = p_ref.shape[0], p_ref.shape[1]

    def token_shift(src_ref, carry_ref, mix):
        out = []
        for bi in range(nb):
            cur = src_ref[bi]
            prev = _shift_rows(cur, carry_ref[bi:bi + 1, :])
            carry_ref[bi:bi + 1, :] = cur[nt - 1:nt, :]
            out.append(cur + (prev - cur) * mix)
        return jnp.concatenate(out, axis=0)

    x = token_shift(p_ref, prev_ref, mu_ref[...])
    r = x[:, 0:RWKV_DIM]
    k = x[:, RWKV_DIM:2 * RWKV_DIM]
    v = x[:, 2 * RWKV_DIM:3 * RWKV_DIM]
    z = x[:, 3 * RWKV_DIM:]
    lane = lax.broadcasted_iota(jnp.int32, (1, LANES), 1)
    zt = jnp.where(lane < 32, jnp.tanh(z), jnp.where(lane < 64, z, jax.nn.sigmoid(z)))
    lo = _bdot(zt, wl_ref[...])
    w0, a0, k_k, k_a, r_k, ln_w, ln_b, v0 = (vec_ref[i:i + 1, :] for i in range(8))
    seg = seg_ref[...]

    zz = -(w0 + lo[:, 0:RWKV_DIM])
    softplus = jnp.maximum(zz, 0.0) + jnp.log(1.0 + jnp.exp(-jnp.abs(zz)))
    lw = -jnp.exp(-softplus - 0.5)
    a = jax.nn.sigmoid(a0 + lo[:, RWKV_DIM:2 * RWKV_DIM])
    g = lo[:, 2 * RWKV_DIM:]
    if has_vres:
        xv = token_shift(pv_ref, pprev_ref, vmu_ref[...])
        v_first = vf_ref[...].reshape(nb * nt, RWKV_DIM)
        v = v + (v_first - v) * jax.nn.sigmoid(v0 + _bdot(xv, v2_ref[...]))
    else:
        vout_ref[...] = v.reshape(nb, nt, RWKV_DIM)
    kk = k * k_k
    kk = kk * jnp.minimum(lax.rsqrt(_bdot(kk * kk, seg)), 1e12)
    kmod = k * (1.0 + (a - 1.0) * k_a)
    bonus = _bdot(r * kmod * r_k, seg) * v

    tri = tri_ref[...]
    lw_hi = lw.astype(BF16)
    lw_lo = (lw - lw_hi.astype(F32)).astype(BF16)
    cum = (jnp.dot(tri, lw_hi, preferred_element_type=F32)
           + jnp.dot(tri, lw_lo, preferred_element_type=F32))
    pinv = jnp.exp(-cum)
    r_t = r * jnp.exp(cum)
    a_t = -kk * jnp.exp(cum - lw)
    b_t = kk * a * pinv
    k_t = kmod * pinv
    p_ends = [jnp.exp(cum[bi * nt + nt - 1:(bi + 1) * nt, :]) for bi in range(nb)]
    p_end_rows = jnp.concatenate([jnp.broadcast_to(pe, (nt, RWKV_DIM)) for pe in p_ends], axis=0)
    bh_t = b_t * p_end_rows
    kh_t = k_t * p_end_rows

    strict, incl, blk8, eye = (cm_ref[i] for i in range(4))
    m0 = (lane < HEAD_DIM).astype(F32)
    m1 = 1.0 - m0

    def stack(t, rows, sl):
        ts = t[rows, sl]
        return jnp.concatenate([ts * m0, ts * m1], axis=0)

    npair = RWKV_DIM // PAIR
    chains = [(bi, j) for bi in range(nb) for j in range(npair)]
    each = lambda fn, *cols: [fn(*args) for args in zip(*cols)]
    cat0 = lambda *ts: jnp.concatenate(ts, axis=0)
    cat1 = lambda *ts: jnp.concatenate(ts, axis=1)

    def stacks(t):
        return [stack(t, slice(bi * nt, (bi + 1) * nt), slice(j * PAIR, (j + 1) * PAIR))
                for bi, j in chains]

    a_st, r_st, b_st, k_st, v_st, bh_st, kh_st = (stacks(t) for t in (a_t, r_t, b_t, k_t, v, bh_t, kh_t))
    gram = each(lambda a_, r_, b_, k_: _bdot_nt(cat0(a_, r_), cat0(b_, k_)), a_st, r_st, b_st, k_st)
    n2 = 2 * nt
    a_ab = [gm[:n2, :n2] * strict for gm in gram]
    a_ak = [gm[:n2, n2:] * strict for gm in gram]
    a_rb = [gm[n2:, :n2] * incl for gm in gram]
    a_rk = [gm[n2:, n2:] * incl for gm in gram]
    dg = [m * blk8 for m in a_ab]
    off = each(lambda m, d_: m - d_, a_ab, dg)
    d2 = each(_bdot, dg, dg)
    tdiag = [eye + d_ for d_ in dg]
    tdiag = each(lambda t_, d_: t_ + _bdot(t_, d_), tdiag, d2)
    d4 = each(_bdot, d2, d2)
    tdiag = each(lambda t_, d_: t_ + _bdot(t_, d_), tdiag, d4)
    e1 = each(_bdot, tdiag, off)
    e2 = each(_bdot, e1, e1)
    yb = [eye + e_ for e_ in e1]
    yb = each(lambda y_, e_: y_ + _bdot(y_, e_), yb, e2)
    e4 = each(_bdot, e2, e2)
    yb = each(lambda y_, e_: y_ + _bdot(y_, e_), yb, e4)
    tinv = each(_bdot, yb, tdiag)

    hs = [h_ref[bi, j] for bi, j in chains]
    xs = each(lambda a_, ak_, h_, v_: _bdot(cat1(a_, ak_), cat0(h_, v_)), a_st, a_ak, hs, v_st)
    us = each(_bdot, tinv, xs)
    y_st = each(lambda r_, rb_, rk_, h_, u_, v_: _bdot(cat1(r_, rb_, rk_), cat0(h_, u_, v_)),
                r_st, a_rb, a_rk, hs, us, v_st)
    pe_col = [jnp.sum(eye * p_ends[bi][:, j * PAIR:(j + 1) * PAIR], axis=1, keepdims=True)
              for bi, j in chains]
    h_new = each(lambda pc_, h_, bh_, kh_, u_, v_: pc_ * h_ + _bdot(cat0(bh_, kh_).T, cat0(u_, v_)),
                 pe_col, hs, bh_st, kh_st, us, v_st)
    for (bi, j), hn in zip(chains, h_new):
        h_ref[bi, j] = hn
    y_pair = [ys_[:nt] + ys_[nt:] for ys_ in y_st]
    y = cat0(*[cat1(*y_pair[bi * npair:(bi + 1) * npair]) for bi in range(nb)])
    inv_n = 1.0 / HEAD_DIM
    mean = _bdot(y, seg) * inv_n
    dy = y - mean
    var = _bdot(dy * dy, seg) * inv_n
    yn = dy * lax.rsqrt(var + GN_EPS) * ln_w + ln_b
    y_ref[...] = ((yn + bonus) * g).reshape(nb, nt, RWKV_DIM).astype(y_ref.dtype)


def _rwkv(p_rwkv, mu, wl, vecs, consts, vres=None):
    b, s, _ = p_rwkv.shape
    c = RWKV_CHUNK
    seg, tri, cm = consts
    tile = lambda wd: pl.BlockSpec((b, c, wd), lambda j: (0, j, 0))
    full = lambda arr: pl.BlockSpec(arr.shape, lambda j: (0,) * arr.ndim)
    y_shape = jax.ShapeDtypeStruct((b, s, RWKV_DIM), BF16)
    scratch = [pltpu.VMEM((b, RWKV_IN), F32)]
    if vres is None:
        ins = [p_rwkv, mu, wl, vecs, seg, tri, cm]
        in_specs = [tile(RWKV_IN)] + [full(t) for t in ins[1:]]
        out_shape = [y_shape, jax.ShapeDtypeStruct((b, s, RWKV_DIM), F32)]
        out_specs = [tile(RWKV_DIM), tile(RWKV_DIM)]
    else:
        p_vres, v_first, vmu, v2 = vres
        ins = [p_rwkv, p_vres, v_first, mu, vmu, wl, v2, vecs, seg, tri, cm]
        in_specs = [tile(RWKV_IN), tile(VRES_PAD), tile(RWKV_DIM)] + [full(t) for t in ins[3:]]
        out_shape = [y_shape]
        out_specs = [tile(RWKV_DIM)]
        scratch.append(pltpu.VMEM((b, VRES_PAD), F32))
    scratch.append(pltpu.VMEM((b, RWKV_DIM // PAIR, PAIR, PAIR), F32))
    return pl.pallas_call(
        functools.partial(_rwkv_body, has_vres=vres is not None),
        grid=(s // c,),
        in_specs=in_specs, out_specs=out_specs, out_shape=out_shape,
        scratch_shapes=scratch,
        compiler_params=_params("arbitrary"),
        name="rwkv7",
    )(*ins)


def _conv_body(p_ref, w_ref, gain_ref, seg_ref, o_ref, carry_ref):
    @pl.when(pl.program_id(1) == 0)
    def _():
        carry_ref[...] = jnp.zeros_like(carry_ref)

    p = p_ref[...]
    nt = p.shape[0]
    u = p[:, CONV_DIM:2 * CONV_DIM] * p[:, 2 * CONV_DIM:]
    carry = carry_ref[...]
    u1 = _shift_rows(u, carry[1:2, :], 1)
    u2 = _shift_rows(u, carry, 2)
    carry_ref[...] = u[nt - 2:nt, :]
    w = w_ref[...]
    y = p[:, 0:CONV_DIM] * (w[0:1, :] * u2 + w[1:2, :] * u1 + w[2:3, :] * u)
    ms = _bdot(y * y, seg_ref[...]) * (1.0 / HEAD_DIM)
    o_ref[...] = (y * lax.rsqrt(ms + NORM_EPS) * gain_ref[...]).astype(o_ref.dtype)


def _short_conv(p_conv, w, gain, seg, tm):
    b, s, _ = p_conv.shape
    return pl.pallas_call(
        _conv_body,
        grid=(b, s // tm),
        in_specs=[pl.BlockSpec((None, tm, CONV_IN), lambda i, j: (i, j, 0)),
                  pl.BlockSpec(w.shape, lambda i, j: (0, 0)),
                  pl.BlockSpec(gain.shape, lambda i, j: (0, 0)),
                  pl.BlockSpec(seg.shape, lambda i, j: (0, 0))],
        out_specs=pl.BlockSpec((None, tm, CONV_DIM), lambda i, j: (i, j, 0)),
        out_shape=jax.ShapeDtypeStruct((b, s, CONV_DIM), BF16),
        scratch_shapes=[pltpu.VMEM((CONV_WIDTH - 1, CONV_DIM), F32)],
        compiler_params=_params("arbitrary", "arbitrary"),
        name="short_conv",
    )(p_conv, w, gain, seg)


def _attn_prep_body(p_ref, cos_ref, sin_ref, qg_ref, kg_ref, seg_ref, q_ref, k_ref, v_ref):
    p = p_ref[...]
    seg = seg_ref[...]
    lane = lax.broadcasted_iota(jnp.int32, (1, LANES), 1)
    first = (lane & (HEAD_DIM - 1)) < HEAD_DIM // 2
    for part, g_ref, o_ref, scale in ((0, qg_ref, q_ref, HEAD_DIM ** -0.5), (1, kg_ref, k_ref, 1.0)):
        x = p[:, part * ATTN_DIM:(part + 1) * ATTN_DIM]
        ms = _bdot(x * x, seg) * (1.0 / HEAD_DIM)
        xn = x * lax.rsqrt(ms + NORM_EPS) * g_ref[...]
        rot = []
        for j in range(ATTN_DIM // LANES):
            xs = xn[:, j * LANES:(j + 1) * LANES]
            rot.append(jnp.where(first, pltpu.roll(xs, LANES - HEAD_DIM // 2, 1),
                                 pltpu.roll(xs, HEAD_DIM // 2, 1)))
        out = (xn * cos_ref[...] + jnp.concatenate(rot, axis=1) * sin_ref[...]) * scale
        for j in range(ATTN_DIM // PAIR):
            o_ref[j] = out[:, j * PAIR:(j + 1) * PAIR]
    for j in range(ATTN_DIM // PAIR):
        v_ref[j] = p[:, 2 * ATTN_DIM + j * PAIR:2 * ATTN_DIM + (j + 1) * PAIR]


def _pair_major_spec(rows, row_index):
    return pl.BlockSpec((None, ATTN_DIM // PAIR, rows, PAIR), lambda i, j: (i, 0, row_index(j), 0))


def _attn_prep(p_attn, cos, sin, qg, kg, seg, tm):
    b, s, _ = p_attn.shape
    tab = pl.BlockSpec((tm, ATTN_DIM), lambda i, j: (j, 0))
    full = lambda arr: pl.BlockSpec(arr.shape, lambda i, j: (0, 0))
    shp = jax.ShapeDtypeStruct((b, ATTN_DIM // PAIR, s, PAIR), F32)
    return pl.pallas_call(
        _attn_prep_body,
        grid=(b, s // tm),
        in_specs=[pl.BlockSpec((None, tm, ATTN_IN), lambda i, j: (i, j, 0)), tab, tab,
                  full(qg), full(kg), full(seg)],
        out_specs=[_pair_major_spec(tm, lambda j: j)] * 3,
        out_shape=[shp] * 3,
        compiler_params=_params("arbitrary", "arbitrary"),
        name="attn_prep",
    )(p_attn, cos, sin, qg, kg, seg)


def _attn_body(q_ref, k_ref, v_ref, kprev_ref, vprev_ref, o_ref, l_ref, *, dil):
    span = pl.program_id(1)
    blk = ATTN_BLOCK
    npair = ATTN_DIM // PAIR
    qi = lax.broadcasted_iota(jnp.int32, (blk, 2 * blk), 0)
    kj = lax.broadcasted_iota(jnp.int32, (blk, 2 * blk), 1)
    dist = qi + blk - kj
    band = (dist >= 0) & (dist <= ATTN_BLOCK)
    lane = lax.broadcasted_iota(jnp.int32, (1, LANES), 1)
    head0 = lane < HEAD_DIM
    m0 = head0.astype(F32)
    heads = [(j, hm) for j in range(npair) for hm in (m0, 1.0 - m0)]

    def block_attn(q, kcat, vcat, block_index):
        mask = band & (kj + block_index * (2 * blk) >= blk)
        s_all = [_bdot_nt(q[j] * hm, kcat[j]) for j, hm in heads]
        s_all = [jnp.where(mask, s, -jnp.inf) for s in s_all]
        m_all = [jnp.max(s, axis=-1, keepdims=True) for s in s_all]
        pe_all = [jnp.exp(s - m) for s, m in zip(s_all, m_all)]
        den_all = [jnp.sum(pe, axis=-1, keepdims=True) for pe in pe_all]
        o_all = [_bdot(pe, vcat[j]) * (1.0 / den) for pe, den, (j, _) in zip(pe_all, den_all, heads)]
        lse_all = [m + jnp.log(den) for m, den in zip(m_all, den_all)]
        o_pair = [jnp.where(head0, o_all[2 * j], o_all[2 * j + 1]) for j in range(npair)]
        l_pair = [jnp.where(head0, lse_all[2 * j], lse_all[2 * j + 1]) for j in range(npair)]
        return o_pair, l_pair

    if dil == 1:
        nq = q_ref.shape[1] // blk
        kext = [jnp.concatenate([kprev_ref[j], k_ref[j]], axis=0) for j in range(npair)]
        vext = [jnp.concatenate([vprev_ref[j], v_ref[j]], axis=0) for j in range(npair)]
        for qb in range(nq):
            rows = slice(qb * blk, (qb + 1) * blk)
            keys = slice(qb * blk, (qb + 2) * blk)
            o_pair, l_pair = block_attn([q_ref[j, rows, :] for j in range(npair)],
                                        [kx[keys] for kx in kext], [vx[keys] for vx in vext],
                                        span * nq + qb)
            for j in range(npair):
                o_ref[j, rows, :] = o_pair[j]
                l_ref[j, rows, :] = l_pair[j]
    else:
        def one_class(r, carry):
            rows = pl.ds(r, blk, stride=dil)
            cat = lambda prev_ref, ref: [jnp.concatenate([prev_ref[j, rows, :], ref[j, rows, :]], axis=0)
                                         for j in range(npair)]
            o_pair, l_pair = block_attn([q_ref[j, rows, :] for j in range(npair)],
                                        cat(kprev_ref, k_ref), cat(vprev_ref, v_ref), span)
            for j in range(npair):
                o_ref[j, rows, :] = o_pair[j]
                l_ref[j, rows, :] = l_pair[j]
            return carry
        lax.fori_loop(0, dil, one_class, 0)


def _attn(q, k, v, dil):
    b, _, s, _ = q.shape
    if dil == 1:
        nq = 4
        span = ATTN_BLOCK * nq
        prev = _pair_major_spec(ATTN_BLOCK, lambda m: jnp.maximum(m * nq - 1, 0))
    else:
        span = ATTN_BLOCK * dil
        prev = _pair_major_spec(span, lambda m: jnp.maximum(m - 1, 0))
    spec = _pair_major_spec(span, lambda m: m)
    shp = jax.ShapeDtypeStruct(q.shape, F32)
    return pl.pallas_call(
        functools.partial(_attn_body, dil=dil),
        grid=(b, s // span),
        in_specs=[spec, spec, spec, prev, prev],
        out_specs=[spec, spec],
        out_shape=[shp, shp],
        compiler_params=_params("arbitrary", "arbitrary"),
        name=f"attn_d{dil}",
    )(q, k, v, k, v)


def _attn_combine_body(*refs):
    n = len(DILATIONS)
    o_refs, l_refs = refs[:n], refs[n:2 * n]
    gain_ref, seg_ref, y_ref = refs[2 * n:]
    for j in range(ATTN_DIM // PAIR):
        sl = slice(j * PAIR, (j + 1) * PAIR)
        ls = [r[j] for r in l_refs]
        m = functools.reduce(jnp.maximum, ls)
        es = [jnp.exp(l - m) for l in ls]
        den = functools.reduce(jnp.add, es)
        o = functools.reduce(jnp.add, [e * r[j] for e, r in zip(es, o_refs)]) / den
        ms = _bdot(o * o, seg_ref[...]) * (1.0 / HEAD_DIM)
        y_ref[:, sl] = (o * lax.rsqrt(ms + NORM_EPS) * gain_ref[:, sl]).astype(y_ref.dtype)


def _attn_combine(os_, ls_, gain, seg, tm):
    b, _, s, _ = os_[0].shape
    tile = _pair_major_spec(tm, lambda j: j)
    full = lambda arr: pl.BlockSpec(arr.shape, lambda i, j: (0, 0))
    return pl.pallas_call(
        _attn_combine_body,
        grid=(b, s // tm),
        in_specs=[tile] * (2 * len(os_)) + [full(gain), full(seg)],
        out_specs=pl.BlockSpec((None, tm, ATTN_DIM), lambda i, j: (i, j, 0)),
        out_shape=jax.ShapeDtypeStruct((b, s, ATTN_DIM), BF16),
        compiler_params=_params("arbitrary", "arbitrary"),
        name="attn_combine",
    )(*os_, *ls_, gain, seg)


INFO_E0, INFO_E1, INFO_R0, INFO_R1, INFO_G0, INFO_G1 = range(6)


def _outproj_body(*refs, has_router):
    if has_router:
        (x_ref, yr_ref, yc_ref, ya_ref, wo_ref, g_ref, rwh_ref, rwl_ref, tri_ref,
         xn_ref, h_ref, info_ref, cnt_ref) = refs
    else:
        x_ref, yr_ref, yc_ref, ya_ref, wo_ref, g_ref, xn_ref, h_ref = refs
    o1, o2 = RWKV_DIM, RWKV_DIM + CONV_DIM
    acc = jnp.dot(yr_ref[...], wo_ref[0:o1, :], preferred_element_type=F32)
    acc += jnp.dot(yc_ref[...], wo_ref[o1:o2, :], preferred_element_type=F32)
    acc += jnp.dot(ya_ref[...], wo_ref[o2:, :], preferred_element_type=F32)
    xn = x_ref[...] + acc
    xn_ref[...] = xn
    h = xn * lax.rsqrt(jnp.mean(xn * xn, axis=-1, keepdims=True) + NORM_EPS) * g_ref[...]
    h_ref[...] = h.astype(h_ref.dtype)
    if not has_router:
        return

    h_hi = h.astype(BF16)
    h_lo = (h - h_hi.astype(F32)).astype(BF16)
    rwh = rwh_ref[...]
    logits = (jnp.dot(h_hi, rwh, preferred_element_type=F32)
              + jnp.dot(h_lo, rwh, preferred_element_type=F32)
              + jnp.dot(h_hi, rwl_ref[...], preferred_element_type=F32))
    lane = lax.broadcasted_iota(jnp.int32, (1, LANES), 1)
    logits = jnp.where(lane < N_EXPERTS, logits, -jnp.inf)

    n_grp = 4
    rows_g = logits.shape[0] // n_grp
    lg = [logits[g * rows_g:(g + 1) * rows_g] for g in range(n_grp)]

    def top(ls):
        ms = [jnp.max(l, axis=-1, keepdims=True) for l in ls]
        idx = [jnp.min(jnp.where(l == m, lane, LANES), axis=-1, keepdims=True) for l, m in zip(ls, ms)]
        return ms, idx, [lane == ix for ix in idx]

    m1, i1, oh1 = top(lg)
    m2, i2, oh2 = top([jnp.where(o, -jnp.inf, l) for o, l in zip(oh1, lg)])
    cat = lambda parts: jnp.concatenate(parts, axis=0)
    m1, i1, oh1, m2, i2, oh2 = (cat(v) for v in (m1, i1, oh1, m2, i2, oh2))
    zexp = jnp.exp(m2 - m1)
    g0 = 1.0 / (1.0 + zexp)
    g1 = zexp * g0
    oh = oh1.astype(F32) + oh2.astype(F32)
    tot = jnp.dot(tri_ref[...], oh.astype(BF16), preferred_element_type=F32)
    r0 = jnp.sum(jnp.where(oh1, tot, 0.0), axis=-1, keepdims=True)
    r1 = jnp.sum(jnp.where(oh2, tot, 0.0), axis=-1, keepdims=True)
    cnt_ref[...] = jnp.broadcast_to(jnp.sum(oh, axis=0, keepdims=True), cnt_ref.shape)
    info = jnp.zeros(logits.shape, F32)
    for ln, val in ((INFO_E0, i1.astype(F32)), (INFO_E1, i2.astype(F32)), (INFO_R0, r0),
                    (INFO_R1, r1), (INFO_G0, g0), (INFO_G1, g1)):
        info = jnp.where(lane == ln, val, info)
    info_ref[...] = info


def _outproj(x2d, yr, yc, ya, wo, gain, tm, router=None):
    t, d = x2d.shape
    row = lambda wd: pl.BlockSpec((tm, wd), lambda i: (i, 0))
    full = lambda arr: pl.BlockSpec(arr.shape, lambda i: (0, 0))
    ins = [x2d, yr, yc, ya, wo, gain]
    in_specs = [row(d), row(RWKV_DIM), row(CONV_DIM), row(ATTN_DIM), full(wo), full(gain)]
    out_shape = [jax.ShapeDtypeStruct((t, d), F32)]
    out_specs = [row(d)]
    scratch = []
    if router is None:
        out_shape.append(jax.ShapeDtypeStruct((t, d), BF16))
        out_specs.append(row(d))
    else:
        ins += list(router)
        in_specs += [full(a) for a in router]
        out_shape += [jax.ShapeDtypeStruct((t, d), BF16), jax.ShapeDtypeStruct((t, LANES), F32),
                      jax.ShapeDtypeStruct((t // tm * 8, LANES), F32)]
        out_specs += [row(d), row(LANES), pl.BlockSpec((8, LANES), lambda i: (i, 0))]
    return pl.pallas_call(
        functools.partial(_outproj_body, has_router=router is not None),
        grid=(t // tm,),
        in_specs=in_specs, out_specs=out_specs, out_shape=out_shape,
        scratch_shapes=scratch,
        compiler_params=_params("arbitrary"),
        name="outproj",
    )(*ins)


def _ffn_body(te_ref, nu_ref, x_ref, wg_ref, wu_ref, wd_ref, *rest, has_res):
    if has_res:
        res_ref, o_ref, acc_ref = rest
    else:
        o_ref, acc_ref = rest
    i, f = pl.program_id(0), pl.program_id(1)

    @pl.when(f == 0)
    def _():
        acc_ref[...] = jnp.zeros_like(acc_ref)

    @pl.when(i < nu_ref[0])
    def _():
        x = x_ref[...].astype(BF16)
        hg = jnp.dot(x, wg_ref[...], preferred_element_type=F32)
        hu = jnp.dot(x, wu_ref[...], preferred_element_type=F32)
        act = hg * jax.nn.sigmoid(hg) * hu
        acc_ref[...] += jnp.dot(act.astype(BF16), wd_ref[...], preferred_element_type=F32)

    @pl.when(f == pl.num_programs(1) - 1)
    def _():
        out = acc_ref[...]
        if has_res:
            out = out + res_ref[...]
        o_ref[...] = out.astype(o_ref.dtype)


def _ffn(x, wg, wu, wd, tile_expert, n_used, tm, tf, res=None):
    n, d = x.shape
    ff = wg.shape[-1]
    row = pl.BlockSpec((tm, d), lambda i, f, te, nu: (i, 0))
    ins = [x, wg, wu, wd]
    in_specs = [row,
                pl.BlockSpec((None, d, tf), lambda i, f, te, nu: (te[i], 0, f)),
                pl.BlockSpec((None, d, tf), lambda i, f, te, nu: (te[i], 0, f)),
                pl.BlockSpec((None, tf, d), lambda i, f, te, nu: (te[i], f, 0))]
    if res is not None:
        ins.append(res)
        in_specs.append(row)
    return pl.pallas_call(
        functools.partial(_ffn_body, has_res=res is not None),
        grid_spec=pltpu.PrefetchScalarGridSpec(
            num_scalar_prefetch=2,
            grid=(n // tm, ff // tf),
            in_specs=in_specs,
            out_specs=row,
            scratch_shapes=[pltpu.VMEM((tm, d), F32)]),
        out_shape=jax.ShapeDtypeStruct((n, d), F32),
        compiler_params=_params("arbitrary", "arbitrary"),
        name="swiglu",
    )(tile_expert, n_used, *ins)


def _row_copy(src, s, dst, d, sem):
    return pltpu.make_async_copy(src.at[pl.ds(s, 1)], dst.at[pl.ds(d, 1)], sem)


SEG_ALIGN = 16


def _dispatch_body(base_ref, loff_ref, seg_ref, h_ref, info_ref, basev_ref, loffv_ref, xs_in,
                   xs_out, dest_ref, stage_ref, sem):
    del xs_in
    i = pl.program_id(0)
    tm = h_ref.shape[0]
    n_stage = stage_ref.shape[0]

    info = info_ref[...]
    lane = lax.broadcasted_iota(jnp.int32, (1, LANES), 1)
    lane_f = lane.astype(F32)
    pos, dest = [], []
    for e_ln, r_ln in ((INFO_E0, INFO_R0), (INFO_E1, INFO_R1)):
        onehot = lane_f == info[:, e_ln:e_ln + 1]
        rank = info[:, r_ln:r_ln + 1]
        pick = lambda v: jnp.sum(jnp.where(onehot, v, 0.0), axis=-1, keepdims=True)
        pos.append(pick(loffv_ref[0:1, :]) + rank)
        dest.append(pick(basev_ref[0:1, :]) + rank)
    dest_ref[...] = jnp.where(lane == 0, dest[0], jnp.where(lane == 1, dest[1], 0.0))
    pos_t = jnp.transpose(jnp.where(lane == 0, pos[0], jnp.where(lane == 1, pos[1], -1.0)))
    rows = lax.broadcasted_iota(jnp.int32, (n_stage, 1), 0).astype(F32)
    perm = ((rows == pos_t[0:1, :]) | (rows == pos_t[1:2, :])).astype(BF16)
    stage_ref[...] = jnp.dot(perm, h_ref[...].astype(BF16),
                             preferred_element_type=F32).astype(stage_ref.dtype)

    def piece_copy(e, off, size):
        src = pl.multiple_of(loff_ref[i * N_EXPERTS + e] + off, SEG_ALIGN)
        dst = pl.multiple_of(base_ref[i * N_EXPERTS + e] + off, SEG_ALIGN)
        return pltpu.make_async_copy(stage_ref.at[pl.ds(src, size)], xs_out.at[pl.ds(dst, size)], sem)

    sizes = [tm >> k for k in range(tm.bit_length()) if (tm >> k) >= SEG_ALIGN]
    for start in (True, False):
        for e in range(N_EXPERTS):
            seg = seg_ref[i * N_EXPERTS + e]
            for size in sizes:
                off = seg & ~(2 * size - 1)

                @pl.when((seg & size) != 0)
                def _():
                    if start:
                        piece_copy(e, off, size).start()
                    else:
                        piece_copy(e, off, size).wait()


def _dispatch(h2, info, base, loff, seg, n_rows, tm):
    t, d = h2.shape
    n_tt = t // tm
    n_stage = 2 * tm + N_EXPERTS * SEG_ALIGN
    row = lambda wd: pl.BlockSpec((tm, wd), lambda i, *_: (i, 0))
    vec = pl.BlockSpec((8, LANES), lambda i, *_: (i, 0))
    lanes8 = lambda a: jnp.repeat(jnp.pad(a.astype(F32), ((0, 0), (0, LANES - N_EXPERTS))), 8, axis=0)
    flat = lambda a: a.reshape(-1).astype(jnp.int32)
    any_spec = pl.BlockSpec(memory_space=pl.ANY)
    return pl.pallas_call(
        _dispatch_body,
        grid_spec=pltpu.PrefetchScalarGridSpec(
            num_scalar_prefetch=3,
            grid=(n_tt,),
            in_specs=[row(d), row(LANES), vec, vec, any_spec],
            out_specs=[any_spec, row(LANES)],
            scratch_shapes=[pltpu.VMEM((n_stage, d), BF16), pltpu.SemaphoreType.DMA]),
        out_shape=[jax.ShapeDtypeStruct((n_rows, d), BF16), jax.ShapeDtypeStruct((t, LANES), F32)],
        input_output_aliases={7: 0},
        compiler_params=_params("arbitrary"),
        name="moe_dispatch",
    )(flat(base), flat(loff), flat(seg), h2, info, lanes8(base), lanes8(loff),
      jnp.zeros((n_rows, d), BF16))


def _combine_body(dest_ref, destn_ref, x_ref, info_ref, ys_hbm, o_ref, buf, sem):
    i = pl.program_id(0)
    ntok = x_ref.shape[0]
    slot = lax.rem(i, 2)

    def gather(idx_ref, sl):
        def issue(t, carry):
            for kq in range(2):
                _row_copy(ys_hbm, idx_ref[0, 0, 2 * t + kq], buf.at[sl, kq], t,
                          sem.at[sl]).start(priority=kq)
            return carry
        lax.fori_loop(0, ntok, issue, 0, unroll=8)

    @pl.when(i == 0)
    def _():
        gather(dest_ref, 0)

    @pl.when(i + 1 < pl.num_programs(0))
    def _():
        gather(destn_ref, 1 - slot)

    for kq in range(2):
        pltpu.make_async_copy(ys_hbm.at[pl.ds(0, ntok)], buf.at[slot, kq], sem.at[slot]).wait()
    info = info_ref[...]
    g0 = info[:, INFO_G0:INFO_G0 + 1]
    g1 = info[:, INFO_G1:INFO_G1 + 1]
    o_ref[...] = x_ref[...] + g0 * buf[slot, 0] + g1 * buf[slot, 1]


def _combine(xn, info, ys, dest, tm):
    t, d = xn.shape
    n = t // tm
    row = lambda wd: pl.BlockSpec((tm, wd), lambda i: (i, 0))
    smem = lambda fn: pl.BlockSpec((1, 1, 2 * tm), fn, memory_space=pltpu.SMEM)
    idx = dest.reshape(n, 1, 2 * tm)
    return pl.pallas_call(
        _combine_body,
        grid=(n,),
        in_specs=[smem(lambda i: (i, 0, 0)), smem(lambda i: (jnp.minimum(i + 1, n - 1), 0, 0)),
                  row(d), row(LANES), pl.BlockSpec(memory_space=pl.ANY)],
        out_specs=row(d),
        out_shape=jax.ShapeDtypeStruct((t, d), F32),
        scratch_shapes=[pltpu.VMEM((2, 2, tm, d), F32), pltpu.SemaphoreType.DMA((2,))],
        compiler_params=_params("arbitrary"),
        name="moe_combine",
    )(idx, idx, xn, info, ys)


def _seg_matrix(n):
    i = jnp.arange(n) // HEAD_DIM
    return (i[:, None] == i[None, :]).astype(BF16)


def _rwkv_consts(nb):
    c = RWKV_CHUNK
    t = jnp.arange(nb * c)
    tri = ((t[:, None] >= t[None, :]) & (t[:, None] // c == t[None, :] // c)).astype(BF16)
    i = jnp.arange(2 * c)
    same = (i[:, None] // c) == (i[None, :] // c)
    ti, tj = (i % c)[:, None], (i % c)[None, :]
    strict = same & (ti > tj)
    incl = same & (ti >= tj)
    blk8 = (i[:, None] // 8) == (i[None, :] // 8)
    eye = i[:, None] == i[None, :]
    cm = jnp.stack([strict, incl, blk8, eye]).astype(F32)
    return _seg_matrix(RWKV_DIM), tri, cm


def _rope_tables(s):
    half = HEAD_DIM // 2
    inv_freq = ROPE_THETA ** (-jnp.arange(half, dtype=F32) * 2.0 / HEAD_DIM)
    ang = jnp.arange(s, dtype=F32)[:, None] * inv_freq[None, :]
    cos, sin = jnp.cos(ang), jnp.sin(ang)
    reps = ATTN_DIM // HEAD_DIM
    cos_t = jnp.tile(jnp.concatenate([cos, cos], axis=1), (1, reps))
    sin_t = jnp.tile(jnp.concatenate([-sin, sin], axis=1), (1, reps))
    return cos_t, sin_t


def _row_tile(n, want):
    return want if n % want == 0 else n


def kernel(x, mix_norm, w_in, tshift_mu, vres_w_in, vres_mu, vres_v0, vres_v2, decay_w0, decay_w2, iclr_a0, iclr_a2, gate_g2, k_k, k_a, r_k, ln_x_w, ln_x_b, conv_w, conv_out_norm, q_norm, k_norm, attn_out_norm, w_o, ffn_norm, dense_wg, dense_wu, dense_wd, router, moe_wg, moe_wu, moe_wd):
    b, s, d = x.shape
    assert d == D_MODEL and s % DIL_SPAN == 0
    t = b * s
    depth = w_in.shape[0]
    tm = _row_tile(s, 512)
    seg384 = _seg_matrix(RWKV_DIM)
    seg256 = _seg_matrix(CONV_DIM)
    rwkv_consts = _rwkv_consts(b)
    cos_t, sin_t = _rope_tables(s)
    zeros_r = jnp.zeros((RWKV_DIM,), F32)

    x2d = x.reshape(t, d)
    v_first = None
    for l in range(depth):
        w_comb = w_in[l]
        widths = [RWKV_IN, CONV_IN, ATTN_IN]
        if l > 0:
            w_comb = jnp.concatenate(
                [w_comb, jnp.pad(vres_w_in[l - 1], ((0, 0), (0, VRES_PAD - VRES_LORA)))], axis=1)
            widths.append(VRES_PAD)
        outs = _inproj(x2d, mix_norm[l][None, :], w_comb.astype(BF16), widths, tm)
        p_rwkv, p_conv, p_attn = (o.reshape(b, s, -1) for o in outs[:3])

        wl = jnp.zeros((LORA_W, 3 * RWKV_DIM), F32)
        wl = wl.at[0:32, 0:RWKV_DIM].set(decay_w2[l])
        wl = wl.at[32:64, RWKV_DIM:2 * RWKV_DIM].set(iclr_a2[l])
        wl = wl.at[64:128, 2 * RWKV_DIM:].set(gate_g2[l]).astype(BF16)
        vecs = jnp.stack([decay_w0[l], iclr_a0[l], k_k[l], k_a[l], r_k[l].reshape(-1), ln_x_w[l],
                          ln_x_b[l], vres_v0[l - 1] if l > 0 else zeros_r])
        mu = tshift_mu[l][None, :]
        if l == 0:
            y_rwkv, v_first = _rwkv(p_rwkv, mu, wl, vecs, rwkv_consts)
        else:
            vmu = jnp.pad(vres_mu[l - 1], (0, VRES_PAD - VRES_LORA))[None, :]
            v2 = jnp.pad(vres_v2[l - 1], ((0, VRES_PAD - VRES_LORA), (0, 0))).astype(BF16)
            (y_rwkv,) = _rwkv(p_rwkv, mu, wl, vecs, rwkv_consts,
                              vres=(outs[3].reshape(b, s, VRES_PAD), v_first, vmu, v2))

        y_conv = _short_conv(p_conv, conv_w[l], conv_out_norm[l].reshape(1, -1), seg256, tm)

        reps = ATTN_DIM // HEAD_DIM
        q, k, v = _attn_prep(p_attn, cos_t, sin_t, jnp.tile(q_norm[l], reps)[None, :],
                             jnp.tile(k_norm[l], reps)[None, :], seg384, tm)
        res = [_attn(q, k, v, dil) for dil in DILATIONS]
        y_attn = _attn_combine([r[0] for r in res], [r[1] for r in res],
                               attn_out_norm[l].reshape(1, -1), _seg_matrix(PAIR), tm)

        flat = lambda a: a.reshape(t, -1)
        wo = w_o[l].astype(BF16)
        gain = ffn_norm[l][None, :]
        i = l // 2
        if l % 2 == 0:
            xn, h2 = _outproj(x2d, flat(y_rwkv), flat(y_conv), flat(y_attn), wo, gain, tm)
            tmf = _row_tile(t, 1024)
            n_tiles = t // tmf
            x2d = _ffn(h2, dense_wg[i][None].astype(BF16), dense_wu[i][None].astype(BF16),
                       dense_wd[i][None].astype(BF16), jnp.zeros((n_tiles,), jnp.int32),
                       jnp.full((1,), n_tiles, jnp.int32), tmf, 512, res=xn)
        else:
            rw = jnp.pad(router[i], ((0, 0), (0, LANES - N_EXPERTS)))
            rw_hi = rw.astype(BF16)
            rw_lo = (rw - rw_hi.astype(F32)).astype(BF16)
            rt = jnp.arange(tm)
            tri = (rt[:, None] > rt[None, :]).astype(BF16)
            xn, h2, info, cnt = _outproj(x2d, flat(y_rwkv), flat(y_conv), flat(y_attn), wo, gain,
                                         tm, router=(rw_hi, rw_lo, tri))
            tme = 1024
            n_tt = t // tm
            cnt_te = cnt.reshape(n_tt, 8, LANES)[:, 0, :N_EXPERTS].astype(jnp.int32)
            seg = (cnt_te + SEG_ALIGN - 1) // SEG_ALIGN * SEG_ALIGN
            run = jnp.cumsum(seg, axis=0) - seg
            padded = (jnp.sum(seg, axis=0) + tme - 1) // tme * tme
            ends = jnp.cumsum(padded)
            base = (ends - padded)[None, :] + run
            loff = jnp.cumsum(seg, axis=1) - seg
            n_rows = pl.cdiv(2 * t + n_tt * N_EXPERTS * (SEG_ALIGN - 1) + N_EXPERTS * tme, tme) * tme
            n_tiles = n_rows // tme
            tile_start = jnp.arange(n_tiles, dtype=jnp.int32) * tme
            tile_expert = jnp.minimum(jnp.sum(ends[None, :] <= tile_start[:, None], axis=1),
                                      N_EXPERTS - 1).astype(jnp.int32)
            n_used = (ends[-1] // tme).astype(jnp.int32).reshape(1)
            xs, dest_l = _dispatch(h2, info, base, loff, seg, n_rows, tm)
            dest = dest_l[:, 0:2].astype(jnp.int32)
            ys = _ffn(xs, moe_wg[i].astype(BF16), moe_wu[i].astype(BF16), moe_wd[i].astype(BF16),
                      tile_expert, n_used, tme, 512)
            x2d = _combine(xn, info, ys, dest, _row_tile(s, 512))
    return x2d.reshape(b, s, d)
```

```python
import functools

import jax
import jax.numpy as jnp
from jax import lax
from jax.experimental import pallas as pl
from jax.experimental.pallas import tpu as pltpu

F32 = jnp.float32
BF16 = jnp.bfloat16

D_MODEL = 1024
HEAD_DIM = 64
RWKV_DIM = 384
CONV_DIM = 256
ATTN_DIM = 384
LORA_W = 128
RWKV_IN = 3 * RWKV_DIM + LORA_W
CONV_IN = 3 * CONV_DIM
ATTN_IN = 3 * ATTN_DIM
VRES_LORA = 16
VRES_PAD = 128
CONV_WIDTH = 3
DILATIONS = (1, 4, 16)
ATTN_BLOCK = 128
DIL_SPAN = ATTN_BLOCK * 16
ROPE_THETA = 10000.0
D_FF = 3584
N_EXPERTS = 8
NORM_EPS = 1e-6
GN_EPS = 64e-5

LANES = 128
RWKV_CHUNK = 64
PAIR = 2 * HEAD_DIM
VMEM_LIMIT = 56 * 1024 * 1024


def _params(*sem):
    return pltpu.CompilerParams(dimension_semantics=sem, vmem_limit_bytes=VMEM_LIMIT)


def _bdot(a, b):
    return jnp.dot(a.astype(BF16), b.astype(BF16), preferred_element_type=F32)


def _bdot_nt(a, b):
    return lax.dot_general(a.astype(BF16), b.astype(BF16), (((1,), (1,)), ((), ())),
                           preferred_element_type=F32)


def _shift_rows(x, carry_row, n=1):
    row = lax.broadcasted_iota(jnp.int32, (x.shape[0], 1), 0)
    out = pltpu.roll(x, n, 0)
    for i in range(n):
        out = jnp.where(row == i, carry_row[i:i + 1, :], out)
    return out


def _inproj_body(x_ref, g_ref, w_ref, cw_ref, cgain_ref, cseg_ref, cos_ref, sin_ref, qg_ref, kg_ref,
                 aseg_ref, *rest, has_vres, tiles_per_seq):
    if has_vres:
        prw_ref, pv_ref, yc_ref, q_ref, k_ref, v_ref, carry_ref = rest
    else:
        prw_ref, yc_ref, q_ref, k_ref, v_ref, carry_ref = rest
    o_conv, o_attn, o_end = RWKV_IN, RWKV_IN + CONV_IN, RWKV_IN + CONV_IN + ATTN_IN
    first_tile = lax.rem(pl.program_id(0), tiles_per_seq) == 0
    n_split = 2
    rows_h = x_ref.shape[0] // n_split
    hs = []
    for c in range(n_split):
        x = x_ref[pl.ds(c * rows_h, rows_h), :]
        hs.append((x * lax.rsqrt(jnp.mean(x * x, axis=-1, keepdims=True) + NORM_EPS)
                   * g_ref[...]).astype(BF16))
    ps = [jnp.dot(h, w_ref[...], preferred_element_type=F32) for h in hs]
    for c, p in enumerate(ps):
        rows = pl.ds(c * rows_h, rows_h)
        prw_ref[rows, :] = p[:, :o_conv]
        if has_vres:
            pv_ref[rows, :] = p[:, o_end:]
        _conv_tile(p[:, o_conv:o_attn], first_tile if c == 0 else False, cw_ref, cgain_ref, cseg_ref,
                   yc_ref, carry_ref, rows)
        _attn_prep_tile(p[:, o_attn:o_end], cos_ref, sin_ref, qg_ref, kg_ref, aseg_ref,
                        q_ref, k_ref, v_ref, rows)


def _inproj(x, gain, w, conv, attn, has_vres, tm):
    b, s, d = x.shape
    t = b * s
    n = w.shape[1]
    tps = s // tm
    conv_w, conv_gain, conv_seg = conv
    cos, sin, qg, kg, attn_seg = attn
    full = lambda arr: pl.BlockSpec(arr.shape, lambda i: (0, 0))
    row = lambda wd: pl.BlockSpec((tm, wd), lambda i: (i, 0))
    tab = pl.BlockSpec((tm, ATTN_DIM), lambda i: (lax.rem(i, tps), 0))
    qkv = pl.BlockSpec((None, ATTN_DIM // PAIR, tm, PAIR), lambda i: (i // tps, 0, lax.rem(i, tps), 0))
    qkv_shape = jax.ShapeDtypeStruct((b, ATTN_DIM // PAIR, s, PAIR), F32)
    out_specs = [row(RWKV_IN)] + ([row(VRES_PAD)] if has_vres else []) + [row(CONV_DIM)] + [qkv] * 3
    out_shape = ([jax.ShapeDtypeStruct((t, RWKV_IN), F32)]
                 + ([jax.ShapeDtypeStruct((t, VRES_PAD), F32)] if has_vres else [])
                 + [jax.ShapeDtypeStruct((t, CONV_DIM), BF16)] + [qkv_shape] * 3)
    return pl.pallas_call(
        functools.partial(_inproj_body, has_vres=has_vres, tiles_per_seq=tps),
        grid=(t // tm,),
        in_specs=[row(d), full(gain), full(w), full(conv_w), full(conv_gain), full(conv_seg),
                  tab, tab, full(qg), full(kg), full(attn_seg)],
        out_specs=out_specs, out_shape=out_shape,
        scratch_shapes=[pltpu.VMEM((CONV_WIDTH - 1, CONV_DIM), F32)],
        compiler_params=_params("arbitrary"),
        name="inproj",
    )(x.reshape(t, d), gain, w, conv_w, conv_gain, conv_seg, cos, sin, qg, kg, attn_seg)


def _rwkv_body(*refs, has_vres):
    if has_vres:
        (p_ref, pv_ref, vf_ref, mu_ref, vmu_ref, wl_ref, v2_ref, vec_ref, seg_ref, tri_ref,
         cm_ref, y_ref, prev_ref, pprev_ref, h_ref) = refs
    else:
        (p_ref, mu_ref, wl_ref, vec_ref, seg_ref, tri_ref,
         cm_ref, y_ref, vout_ref, prev_ref, h_ref) = refs
    @pl.when(pl.program_id(0) == 0)
    def _():
        prev_ref[...] = jnp.zeros_like(prev_ref)
        h_ref[...] = jnp.zeros_like(h_ref)
        if has_vres:
            pprev_ref[...] = jnp.zeros_like(pprev_ref)

    nb, nt = p_ref.shape[0], p_ref.shape[1]

    def token_shift(src_ref, carry_ref, mix):
        out = []
        for bi in range(nb):
            cur = src_ref[bi]
            prev = _shift_rows(cur, carry_ref[bi:bi + 1, :])
            carry_ref[bi:bi + 1, :] = cur[nt - 1:nt, :]
            out.append(cur + (prev - cur) * mix)
        return jnp.concatenate(out, axis=0)

    x = token_shift(p_ref, prev_ref, mu_ref[...])
    r = x[:, 0:RWKV_DIM]
    k = x[:, RWKV_DIM:2 * RWKV_DIM]
    v = x[:, 2 * RWKV_DIM:3 * RWKV_DIM]
    z = x[:, 3 * RWKV_DIM:]
    lane = lax.broadcasted_iota(jnp.int32, (1, LANES), 1)
    zt = jnp.where(lane < 32, jnp.tanh(z), jnp.where(lane < 64, z, jax.nn.sigmoid(z)))
    lo = _bdot(zt, wl_ref[...])
    w0, a0, k_k, k_a, r_k, ln_w, ln_b, v0 = (vec_ref[i:i + 1, :] for i in range(8))
    seg = seg_ref[...]

    zz = -(w0 + lo[:, 0:RWKV_DIM])
    softplus = jnp.maximum(zz, 0.0) + jnp.log(1.0 + jnp.exp(-jnp.abs(zz)))
    lw = -jnp.exp(-softplus - 0.5)
    a = jax.nn.sigmoid(a0 + lo[:, RWKV_DIM:2 * RWKV_DIM])
    g = lo[:, 2 * RWKV_DIM:]
    if has_vres:
        xv = token_shift(pv_ref, pprev_ref, vmu_ref[...])
        v_first = vf_ref[...].reshape(nb * nt, RWKV_DIM)
        v = v + (v_first - v) * jax.nn.sigmoid(v0 + _bdot(xv, v2_ref[...]))
    else:
        vout_ref[...] = v.reshape(nb, nt, RWKV_DIM)
    kk = k * k_k
    kk = kk * jnp.minimum(lax.rsqrt(_bdot(kk * kk, seg)), 1e12)
    kmod = k * (1.0 + (a - 1.0) * k_a)
    bonus = _bdot(r * kmod * r_k, seg) * v

    tri = tri_ref[...]
    lw_hi = lw.astype(BF16)
    lw_lo = (lw - lw_hi.astype(F32)).astype(BF16)
    cum = (jnp.dot(tri, lw_hi, preferred_element_type=F32)
           + jnp.dot(tri, lw_lo, preferred_element_type=F32))
    pinv = jnp.exp(-cum)
    r_t = r * jnp.exp(cum)
    a_t = -kk * jnp.exp(cum - lw)
    b_t = kk * a * pinv
    k_t = kmod * pinv
    p_ends = [jnp.exp(cum[bi * nt + nt - 1:(bi + 1) * nt, :]) for bi in range(nb)]
    p_end_rows = jnp.concatenate([jnp.broadcast_to(pe, (nt, RWKV_DIM)) for pe in p_ends], axis=0)
    bh_t = b_t * p_end_rows
    kh_t = k_t * p_end_rows

    strict, incl, blk8, eye = (cm_ref[i] for i in range(4))
    m0 = (lane < HEAD_DIM).astype(F32)
    m1 = 1.0 - m0

    def stack(t, rows, sl):
        ts = t[rows, sl]
        return jnp.concatenate([ts * m0, ts * m1], axis=0)

    npair = RWKV_DIM // PAIR
    chains = [(bi, j) for bi in range(nb) for j in range(npair)]
    each = lambda fn, *cols: [fn(*args) for args in zip(*cols)]
    cat0 = lambda *ts: jnp.concatenate(ts, axis=0)
    cat1 = lambda *ts: jnp.concatenate(ts, axis=1)

    def stacks(t):
        return [stack(t, slice(bi * nt, (bi + 1) * nt), slice(j * PAIR, (j + 1) * PAIR))
                for bi, j in chains]

    a_st, r_st, b_st, k_st, v_st, bh_st, kh_st = (stacks(t) for t in (a_t, r_t, b_t, k_t, v, bh_t, kh_t))
    gram = each(lambda a_, r_, b_, k_: _bdot_nt(cat0(a_, r_), cat0(b_, k_)), a_st, r_st, b_st, k_st)
    n2 = 2 * nt
    a_ab = [gm[:n2, :n2] * strict for gm in gram]
    a_ak = [gm[:n2, n2:] * strict for gm in gram]
    a_rb = [gm[n2:, :n2] * incl for gm in gram]
    a_rk = [gm[n2:, n2:] * incl for gm in gram]
    dg = [m * blk8 for m in a_ab]
    off = each(lambda m, d_: m - d_, a_ab, dg)
    d2 = each(_bdot, dg, dg)
    tdiag = [eye + d_ for d_ in dg]
    tdiag = each(lambda t_, d_: t_ + _bdot(t_, d_), tdiag, d2)
    d4 = each(_bdot, d2, d2)
    tdiag = each(lambda t_, d_: t_ + _bdot(t_, d_), tdiag, d4)
    e1 = each(_bdot, tdiag, off)
    e2 = each(_bdot, e1, e1)
    yb = [eye + e_ for e_ in e1]
    yb = each(lambda y_, e_: y_ + _bdot(y_, e_), yb, e2)
    e4 = each(_bdot, e2, e2)
    yb = each(lambda y_, e_: y_ + _bdot(y_, e_), yb, e4)
    tinv = each(_bdot, yb, tdiag)

    hs = [h_ref[bi, j] for bi, j in chains]
    xs = each(lambda a_, ak_, h_, v_: _bdot(cat1(a_, ak_), cat0(h_, v_)), a_st, a_ak, hs, v_st)
    us = each(_bdot, tinv, xs)
    y_st = each(lambda r_, rb_, rk_, h_, u_, v_: _bdot(cat1(r_, rb_, rk_), cat0(h_, u_, v_)),
                r_st, a_rb, a_rk, hs, us, v_st)
    pe_col = [jnp.sum(eye * p_ends[bi][:, j * PAIR:(j + 1) * PAIR], axis=1, keepdims=True)
              for bi, j in chains]
    h_new = each(lambda pc_, h_, bh_, kh_, u_, v_: pc_ * h_ + _bdot(cat0(bh_, kh_).T, cat0(u_, v_)),
                 pe_col, hs, bh_st, kh_st, us, v_st)
    for (bi, j), hn in zip(chains, h_new):
        h_ref[bi, j] = hn
    y_pair = [ys_[:nt] + ys_[nt:] for ys_ in y_st]
    y = cat0(*[cat1(*y_pair[bi * npair:(bi + 1) * npair]) for bi in range(nb)])
    inv_n = 1.0 / HEAD_DIM
    mean = _bdot(y, seg) * inv_n
    dy = y - mean
    var = _bdot(dy * dy, seg) * inv_n
    yn = dy * lax.rsqrt(var + GN_EPS) * ln_w + ln_b
    y_ref[...] = ((yn + bonus) * g).reshape(nb, nt, RWKV_DIM).astype(y_ref.dtype)


def _rwkv(p_rwkv, mu, wl, vecs, consts, vres=None):
    b, s, _ = p_rwkv.shape
    c = RWKV_CHUNK
    seg, tri, cm = consts
    tile = lambda wd: pl.BlockSpec((b, c, wd), lambda j: (0, j, 0))
    full = lambda arr: pl.BlockSpec(arr.shape, lambda j: (0,) * arr.ndim)
    y_shape = jax.ShapeDtypeStruct((b, s, RWKV_DIM), BF16)
    scratch = [pltpu.VMEM((b, RWKV_IN), F32)]
    if vres is None:
        ins = [p_rwkv, mu, wl, vecs, seg, tri, cm]
        in_specs = [tile(RWKV_IN)] + [full(t) for t in ins[1:]]
        out_shape = [y_shape, jax.ShapeDtypeStruct((b, s, RWKV_DIM), F32)]
        out_specs = [tile(RWKV_DIM), tile(RWKV_DIM)]
    else:
        p_vres, v_first, vmu, v2 = vres
        ins = [p_rwkv, p_vres, v_first, mu, vmu, wl, v2, vecs, seg, tri, cm]
        in_specs = [tile(RWKV_IN), tile(VRES_PAD), tile(RWKV_DIM)] + [full(t) for t in ins[3:]]
        out_shape = [y_shape]
        out_specs = [tile(RWKV_DIM)]
        scratch.append(pltpu.VMEM((b, VRES_PAD), F32))
    scratch.append(pltpu.VMEM((b, RWKV_DIM // PAIR, PAIR, PAIR), F32))
    return pl.pallas_call(
        functools.partial(_rwkv_body, has_vres=vres is not None),
        grid=(s // c,),
        in_specs=in_specs, out_specs=out_specs, out_shape=out_shape,
        scratch_shapes=scratch,
        compiler_params=_params("arbitrary"),
        name="rwkv7",
    )(*ins)


def _conv_tile(p, first_tile, w_ref, gain_ref, seg_ref, o_ref, carry_ref, rows):
    if first_tile is not False:
        @pl.when(first_tile)
        def _():
            carry_ref[...] = jnp.zeros_like(carry_ref)

    nt = p.shape[0]
    u = p[:, CONV_DIM:2 * CONV_DIM] * p[:, 2 * CONV_DIM:]
    carry = carry_ref[...]
    u1 = _shift_rows(u, carry[1:2, :], 1)
    u2 = _shift_rows(u, carry, 2)
    carry_ref[...] = u[nt - 2:nt, :]
    w = w_ref[...]
    y = p[:, 0:CONV_DIM] * (w[0:1, :] * u2 + w[1:2, :] * u1 + w[2:3, :] * u)
    ms = _bdot(y * y, seg_ref[...]) * (1.0 / HEAD_DIM)
    o_ref[rows, :] = (y * lax.rsqrt(ms + NORM_EPS) * gain_ref[...]).astype(o_ref.dtype)


def _attn_prep_tile(p, cos_ref, sin_ref, qg_ref, kg_ref, seg_ref, q_ref, k_ref, v_ref, rows):
    seg = seg_ref[...]
    cos, sin = cos_ref[rows, :], sin_ref[rows, :]
    lane = lax.broadcasted_iota(jnp.int32, (1, LANES), 1)
    first = (lane & (HEAD_DIM - 1)) < HEAD_DIM // 2
    for part, g_ref, o_ref, scale in ((0, qg_ref, q_ref, HEAD_DIM ** -0.5), (1, kg_ref, k_ref, 1.0)):
        x = p[:, part * ATTN_DIM:(part + 1) * ATTN_DIM]
        ms = _bdot(x * x, seg) * (1.0 / HEAD_DIM)
        xn = x * lax.rsqrt(ms + NORM_EPS) * g_ref[...]
        rot = []
        for j in range(ATTN_DIM // LANES):
            xs = xn[:, j * LANES:(j + 1) * LANES]
            rot.append(jnp.where(first, pltpu.roll(xs, LANES - HEAD_DIM // 2, 1),
                                 pltpu.roll(xs, HEAD_DIM // 2, 1)))
        out = (xn * cos + jnp.concatenate(rot, axis=1) * sin) * scale
        for j in range(ATTN_DIM // PAIR):
            o_ref[j, rows, :] = out[:, j * PAIR:(j + 1) * PAIR]
    for j in range(ATTN_DIM // PAIR):
        v_ref[j, rows, :] = p[:, 2 * ATTN_DIM + j * PAIR:2 * ATTN_DIM + (j + 1) * PAIR]


def _pair_major_spec(rows, row_index):
    return pl.BlockSpec((None, ATTN_DIM // PAIR, rows, PAIR), lambda i, j: (i, 0, row_index(j), 0))


def _attn_body(q_ref, k_ref, v_ref, kprev_ref, vprev_ref, o_ref, l_ref, *, dil):
    span = pl.program_id(1)
    blk = ATTN_BLOCK
    npair = ATTN_DIM // PAIR
    qi = lax.broadcasted_iota(jnp.int32, (blk, 2 * blk), 0)
    kj = lax.broadcasted_iota(jnp.int32, (blk, 2 * blk), 1)
    dist = qi + blk - kj
    band = (dist >= 0) & (dist <= ATTN_BLOCK)
    lane = lax.broadcasted_iota(jnp.int32, (1, LANES), 1)
    head0 = lane < HEAD_DIM
    m0 = head0.astype(F32)
    heads = [(j, hm) for j in range(npair) for hm in (m0, 1.0 - m0)]

    def block_attn(q, kcat, vcat, block_index):
        mask = band & (kj + block_index * (2 * blk) >= blk)
        s_all = [_bdot_nt(q[j] * hm, kcat[j]) for j, hm in heads]
        s_all = [jnp.where(mask, s, -jnp.inf) for s in s_all]
        m_all = [jnp.max(s, axis=-1, keepdims=True) for s in s_all]
        pe_all = [jnp.exp(s - m) for s, m in zip(s_all, m_all)]
        den_all = [jnp.sum(pe, axis=-1, keepdims=True) for pe in pe_all]
        o_all = [_bdot(pe, vcat[j]) * (1.0 / den) for pe, den, (j, _) in zip(pe_all, den_all, heads)]
        lse_all = [m + jnp.log(den) for m, den in zip(m_all, den_all)]
        o_pair = [jnp.where(head0, o_all[2 * j], o_all[2 * j + 1]) for j in range(npair)]
        l_pair = [jnp.where(head0, lse_all[2 * j], lse_all[2 * j + 1]) for j in range(npair)]
        return o_pair, l_pair

    if dil == 1:
        nq = q_ref.shape[1] // blk
        kext = [jnp.concatenate([kprev_ref[j], k_ref[j]], axis=0) for j in range(npair)]
        vext = [jnp.concatenate([vprev_ref[j], v_ref[j]], axis=0) for j in range(npair)]
        for qb in range(nq):
            rows = slice(qb * blk, (qb + 1) * blk)
            keys = slice(qb * blk, (qb + 2) * blk)
            o_pair, l_pair = block_attn([q_ref[j, rows, :] for j in range(npair)],
                                        [kx[keys] for kx in kext], [vx[keys] for vx in vext],
                                        span * nq + qb)
            for j in range(npair):
                o_ref[j, rows, :] = o_pair[j]
                l_ref[j, rows, :] = l_pair[j]
    else:
        def one_class(r, carry):
            rows = pl.ds(r, blk, stride=dil)
            cat = lambda prev_ref, ref: [jnp.concatenate([prev_ref[j, rows, :], ref[j, rows, :]], axis=0)
                                         for j in range(npair)]
            o_pair, l_pair = block_attn([q_ref[j, rows, :] for j in range(npair)],
                                        cat(kprev_ref, k_ref), cat(vprev_ref, v_ref), span)
            for j in range(npair):
                o_ref[j, rows, :] = o_pair[j]
                l_ref[j, rows, :] = l_pair[j]
            return carry
        lax.fori_loop(0, dil, one_class, 0)


def _attn(q, k, v, dil):
    b, _, s, _ = q.shape
    if dil == 1:
        nq = 4
        span = ATTN_BLOCK * nq
        prev = _pair_major_spec(ATTN_BLOCK, lambda m: jnp.maximum(m * nq - 1, 0))
    else:
        span = ATTN_BLOCK * dil
        prev = _pair_major_spec(span, lambda m: jnp.maximum(m - 1, 0))
    spec = _pair_major_spec(span, lambda m: m)
    shp = jax.ShapeDtypeStruct(q.shape, F32)
    return pl.pallas_call(
        functools.partial(_attn_body, dil=dil),
        grid=(b, s // span),
        in_specs=[spec, spec, spec, prev, prev],
        out_specs=[spec, spec],
        out_shape=[shp, shp],
        compiler_params=_params("arbitrary", "arbitrary"),
        name=f"attn_d{dil}",
    )(q, k, v, k, v)


def _attn_combine_body(*refs):
    n = len(DILATIONS)
    o_refs, l_refs = refs[:n], refs[n:2 * n]
    gain_ref, seg_ref, y_ref = refs[2 * n:]
    for j in range(ATTN_DIM // PAIR):
        sl = slice(j * PAIR, (j + 1) * PAIR)
        ls = [r[j] for r in l_refs]
        m = functools.reduce(jnp.maximum, ls)
        es = [jnp.exp(l - m) for l in ls]
        den = functools.reduce(jnp.add, es)
        o = functools.reduce(jnp.add, [e * r[j] for e, r in zip(es, o_refs)]) / den
        ms = _bdot(o * o, seg_ref[...]) * (1.0 / HEAD_DIM)
        y_ref[:, sl] = (o * lax.rsqrt(ms + NORM_EPS) * gain_ref[:, sl]).astype(y_ref.dtype)


def _attn_combine(os_, ls_, gain, seg, tm):
    b, _, s, _ = os_[0].shape
    tile = _pair_major_spec(tm, lambda j: j)
    full = lambda arr: pl.BlockSpec(arr.shape, lambda i, j: (0, 0))
    return pl.pallas_call(
        _attn_combine_body,
        grid=(b, s // tm),
        in_specs=[tile] * (2 * len(os_)) + [full(gain), full(seg)],
        out_specs=pl.BlockSpec((None, tm, ATTN_DIM), lambda i, j: (i, j, 0)),
        out_shape=jax.ShapeDtypeStruct((b, s, ATTN_DIM), BF16),
        compiler_params=_params("arbitrary", "arbitrary"),
        name="attn_combine",
    )(*os_, *ls_, gain, seg)


INFO_E0, INFO_E1, INFO_R0, INFO_R1, INFO_G0, INFO_G1 = range(6)


def _outproj_body(*refs, has_router):
    if has_router:
        (x_ref, yr_ref, yc_ref, ya_ref, wo_ref, g_ref, rwh_ref, rwl_ref, tri_ref,
         xn_ref, h_ref, info_ref, cnt_ref) = refs
    else:
        x_ref, yr_ref, yc_ref, ya_ref, wo_ref, g_ref, xn_ref, h_ref = refs
    o1, o2 = RWKV_DIM, RWKV_DIM + CONV_DIM
    acc = jnp.dot(yr_ref[...], wo_ref[0:o1, :], preferred_element_type=F32)
    acc += jnp.dot(yc_ref[...], wo_ref[o1:o2, :], preferred_element_type=F32)
    acc += jnp.dot(ya_ref[...], wo_ref[o2:, :], preferred_element_type=F32)
    xn = x_ref[...] + acc
    xn_ref[...] = xn
    h = xn * lax.rsqrt(jnp.mean(xn * xn, axis=-1, keepdims=True) + NORM_EPS) * g_ref[...]
    h_ref[...] = h.astype(h_ref.dtype)
    if not has_router:
        return

    h_hi = h.astype(BF16)
    h_lo = (h - h_hi.astype(F32)).astype(BF16)
    rwh = rwh_ref[...]
    logits = (jnp.dot(h_hi, rwh, preferred_element_type=F32)
              + jnp.dot(h_lo, rwh, preferred_element_type=F32)
              + jnp.dot(h_hi, rwl_ref[...], preferred_element_type=F32))
    lane = lax.broadcasted_iota(jnp.int32, (1, LANES), 1)
    logits = jnp.where(lane < N_EXPERTS, logits, -jnp.inf)

    n_grp = 4
    rows_g = logits.shape[0] // n_grp
    lg = [logits[g * rows_g:(g + 1) * rows_g] for g in range(n_grp)]

    def top(ls):
        ms = [jnp.max(l, axis=-1, keepdims=True) for l in ls]
        idx = [jnp.min(jnp.where(l == m, lane, LANES), axis=-1, keepdims=True) for l, m in zip(ls, ms)]
        return ms, idx, [lane == ix for ix in idx]

    m1, i1, oh1 = top(lg)
    m2, i2, oh2 = top([jnp.where(o, -jnp.inf, l) for o, l in zip(oh1, lg)])
    cat = lambda parts: jnp.concatenate(parts, axis=0)
    m1, i1, oh1, m2, i2, oh2 = (cat(v) for v in (m1, i1, oh1, m2, i2, oh2))
    zexp = jnp.exp(m2 - m1)
    g0 = 1.0 / (1.0 + zexp)
    g1 = zexp * g0
    oh = oh1.astype(F32) + oh2.astype(F32)
    tot = jnp.dot(tri_ref[...], oh.astype(BF16), preferred_element_type=F32)
    r0 = jnp.sum(jnp.where(oh1, tot, 0.0), axis=-1, keepdims=True)
    r1 = jnp.sum(jnp.where(oh2, tot, 0.0), axis=-1, keepdims=True)
    cnt_ref[...] = jnp.broadcast_to(jnp.sum(oh, axis=0, keepdims=True), cnt_ref.shape)
    info = jnp.zeros(logits.shape, F32)
    for ln, val in ((INFO_E0, i1.astype(F32)), (INFO_E1, i2.astype(F32)), (INFO_R0, r0),
                    (INFO_R1, r1), (INFO_G0, g0), (INFO_G1, g1)):
        info = jnp.where(lane == ln, val, info)
    info_ref[...] = info


def _outproj(x2d, yr, yc, ya, wo, gain, tm, router=None):
    t, d = x2d.shape
    row = lambda wd: pl.BlockSpec((tm, wd), lambda i: (i, 0))
    full = lambda arr: pl.BlockSpec(arr.shape, lambda i: (0, 0))
    ins = [x2d, yr, yc, ya, wo, gain]
    in_specs = [row(d), row(RWKV_DIM), row(CONV_DIM), row(ATTN_DIM), full(wo), full(gain)]
    out_shape = [jax.ShapeDtypeStruct((t, d), F32)]
    out_specs = [row(d)]
    scratch = []
    if router is None:
        out_shape.append(jax.ShapeDtypeStruct((t, d), BF16))
        out_specs.append(row(d))
    else:
        ins += list(router)
        in_specs += [full(a) for a in router]
        out_shape += [jax.ShapeDtypeStruct((t, d), BF16), jax.ShapeDtypeStruct((t, LANES), F32),
                      jax.ShapeDtypeStruct((t // tm * 8, LANES), F32)]
        out_specs += [row(d), row(LANES), pl.BlockSpec((8, LANES), lambda i: (i, 0))]
    return pl.pallas_call(
        functools.partial(_outproj_body, has_router=router is not None),
        grid=(t // tm,),
        in_specs=in_specs, out_specs=out_specs, out_shape=out_shape,
        scratch_shapes=scratch,
        compiler_params=_params("arbitrary"),
        name="outproj",
    )(*ins)


def _ffn_body(te_ref, nu_ref, x_ref, wg_ref, wu_ref, wd_ref, *rest, has_res):
    if has_res:
        res_ref, o_ref, acc_ref = rest
    else:
        o_ref, acc_ref = rest
    i, f = pl.program_id(0), pl.program_id(1)

    @pl.when(f == 0)
    def _():
        acc_ref[...] = jnp.zeros_like(acc_ref)

    @pl.when(i < nu_ref[0])
    def _():
        x = x_ref[...].astype(BF16)
        hg = jnp.dot(x, wg_ref[...], preferred_element_type=F32)
        hu = jnp.dot(x, wu_ref[...], preferred_element_type=F32)
        act = hg * jax.nn.sigmoid(hg) * hu
        acc_ref[...] += jnp.dot(act.astype(BF16), wd_ref[...], preferred_element_type=F32)

    @pl.when(f == pl.num_programs(1) - 1)
    def _():
        out = acc_ref[...]
        if has_res:
            out = out + res_ref[...]
        o_ref[...] = out.astype(o_ref.dtype)


def _ffn(x, wg, wu, wd, tile_expert, n_used, tm, tf, res=None):
    n, d = x.shape
    ff = wg.shape[-1]
    row = pl.BlockSpec((tm, d), lambda i, f, te, nu: (i, 0))
    ins = [x, wg, wu, wd]
    in_specs = [row,
                pl.BlockSpec((None, d, tf), lambda i, f, te, nu: (te[i], 0, f)),
                pl.BlockSpec((None, d, tf), lambda i, f, te, nu: (te[i], 0, f)),
                pl.BlockSpec((None, tf, d), lambda i, f, te, nu: (te[i], f, 0))]
    if res is not None:
        ins.append(res)
        in_specs.append(row)
    return pl.pallas_call(
        functools.partial(_ffn_body, has_res=res is not None),
        grid_spec=pltpu.PrefetchScalarGridSpec(
            num_scalar_prefetch=2,
            grid=(n // tm, ff // tf),
            in_specs=in_specs,
            out_specs=row,
            scratch_shapes=[pltpu.VMEM((tm, d), F32)]),
        out_shape=jax.ShapeDtypeStruct((n, d), F32),
        compiler_params=_params("arbitrary", "arbitrary"),
        name="swiglu",
    )(tile_expert, n_used, *ins)


def _row_copy(src, s, dst, d, sem):
    return pltpu.make_async_copy(src.at[pl.ds(s, 1)], dst.at[pl.ds(d, 1)], sem)


SEG_ALIGN = 16


def _dispatch_body(base_ref, loff_ref, seg_ref, h_ref, info_ref, basev_ref, loffv_ref, xs_in,
                   xs_out, dest_ref, stage_ref, sem):
    del xs_in
    i = pl.program_id(0)
    tm = h_ref.shape[0]
    n_stage = stage_ref.shape[0]

    info = info_ref[...]
    lane = lax.broadcasted_iota(jnp.int32, (1, LANES), 1)
    lane_f = lane.astype(F32)
    pos, dest = [], []
    for e_ln, r_ln in ((INFO_E0, INFO_R0), (INFO_E1, INFO_R1)):
        onehot = lane_f == info[:, e_ln:e_ln + 1]
        rank = info[:, r_ln:r_ln + 1]
        pick = lambda v: jnp.sum(jnp.where(onehot, v, 0.0), axis=-1, keepdims=True)
        pos.append(pick(loffv_ref[0:1, :]) + rank)
        dest.append(pick(basev_ref[0:1, :]) + rank)
    dest_ref[...] = jnp.where(lane == 0, dest[0], jnp.where(lane == 1, dest[1], 0.0))
    pos_t = jnp.transpose(jnp.where(lane == 0, pos[0], jnp.where(lane == 1, pos[1], -1.0)))
    rows = lax.broadcasted_iota(jnp.int32, (n_stage, 1), 0).astype(F32)
    perm = ((rows == pos_t[0:1, :]) | (rows == pos_t[1:2, :])).astype(BF16)
    stage_ref[...] = jnp.dot(perm, h_ref[...].astype(BF16),
                             preferred_element_type=F32).astype(stage_ref.dtype)

    def piece_copy(e, off, size):
        src = pl.multiple_of(loff_ref[i * N_EXPERTS + e] + off, SEG_ALIGN)
        dst = pl.multiple_of(base_ref[i * N_EXPERTS + e] + off, SEG_ALIGN)
        return pltpu.make_async_copy(stage_ref.at[pl.ds(src, size)], xs_out.at[pl.ds(dst, size)], sem)

    sizes = [tm >> k for k in range(tm.bit_length()) if (tm >> k) >= SEG_ALIGN]
    for start in (True, False):
        for e in range(N_EXPERTS):
            seg = seg_ref[i * N_EXPERTS + e]
            for size in sizes:
                off = seg & ~(2 * size - 1)

                @pl.when((seg & size) != 0)
                def _():
                    if start:
                        piece_copy(e, off, size).start()
                    else:
                        piece_copy(e, off, size).wait()


def _dispatch(h2, info, base, loff, seg, n_rows, tm):
    t, d = h2.shape
    n_tt = t // tm
    n_stage = 2 * tm + N_EXPERTS * SEG_ALIGN
    row = lambda wd: pl.BlockSpec((tm, wd), lambda i, *_: (i, 0))
    vec = pl.BlockSpec((8, LANES), lambda i, *_: (i, 0))
    lanes8 = lambda a: jnp.repeat(jnp.pad(a.astype(F32), ((0, 0), (0, LANES - N_EXPERTS))), 8, axis=0)
    flat = lambda a: a.reshape(-1).astype(jnp.int32)
    any_spec = pl.BlockSpec(memory_space=pl.ANY)
    return pl.pallas_call(
        _dispatch_body,
        grid_spec=pltpu.PrefetchScalarGridSpec(
            num_scalar_prefetch=3,
            grid=(n_tt,),
            in_specs=[row(d), row(LANES), vec, vec, any_spec],
            out_specs=[any_spec, row(LANES)],
            scratch_shapes=[pltpu.VMEM((n_stage, d), BF16), pltpu.SemaphoreType.DMA]),
        out_shape=[jax.ShapeDtypeStruct((n_rows, d), BF16), jax.ShapeDtypeStruct((t, LANES), F32)],
        input_output_aliases={7: 0},
        compiler_params=_params("arbitrary"),
        name="moe_dispatch",
    )(flat(base), flat(loff), flat(seg), h2, info, lanes8(base), lanes8(loff),
      jnp.zeros((n_rows, d), BF16))


def _combine_body(dest_ref, destn_ref, x_ref, info_ref, ys_hbm, o_ref, buf, sem):
    i = pl.program_id(0)
    ntok = x_ref.shape[0]
    slot = lax.rem(i, 2)

    def gather(idx_ref, sl):
        def issue(t, carry):
            for kq in range(2):
                _row_copy(ys_hbm, idx_ref[0, 0, 2 * t + kq], buf.at[sl, kq], t,
                          sem.at[sl]).start(priority=kq)
            return carry
        lax.fori_loop(0, ntok, issue, 0, unroll=8)

    @pl.when(i == 0)
    def _():
        gather(dest_ref, 0)

    @pl.when(i + 1 < pl.num_programs(0))
    def _():
        gather(destn_ref, 1 - slot)

    for kq in range(2):
        pltpu.make_async_copy(ys_hbm.at[pl.ds(0, ntok)], buf.at[slot, kq], sem.at[slot]).wait()
    info = info_ref[...]
    g0 = info[:, INFO_G0:INFO_G0 + 1]
    g1 = info[:, INFO_G1:INFO_G1 + 1]
    o_ref[...] = x_ref[...] + g0 * buf[slot, 0] + g1 * buf[slot, 1]


def _combine(xn, info, ys, dest, tm):
    t, d = xn.shape
    n = t // tm
    row = lambda wd: pl.BlockSpec((tm, wd), lambda i: (i, 0))
    smem = lambda fn: pl.BlockSpec((1, 1, 2 * tm), fn, memory_space=pltpu.SMEM)
    idx = dest.reshape(n, 1, 2 * tm)
    return pl.pallas_call(
        _combine_body,
        grid=(n,),
        in_specs=[smem(lambda i: (i, 0, 0)), smem(lambda i: (jnp.minimum(i + 1, n - 1), 0, 0)),
                  row(d), row(LANES), pl.BlockSpec(memory_space=pl.ANY)],
        out_specs=row(d),
        out_shape=jax.ShapeDtypeStruct((t, d), F32),
        scratch_shapes=[pltpu.VMEM((2, 2, tm, d), F32), pltpu.SemaphoreType.DMA((2,))],
        compiler_params=_params("arbitrary"),
        name="moe_combine",
    )(idx, idx, xn, info, ys)


def _seg_matrix(n):
    i = jnp.arange(n) // HEAD_DIM
    return (i[:, None] == i[None, :]).astype(BF16)


def _rwkv_consts(nb):
    c = RWKV_CHUNK
    t = jnp.arange(nb * c)
    tri = ((t[:, None] >= t[None, :]) & (t[:, None] // c == t[None, :] // c)).astype(BF16)
    i = jnp.arange(2 * c)
    same = (i[:, None] // c) == (i[None, :] // c)
    ti, tj = (i % c)[:, None], (i % c)[None, :]
    strict = same & (ti > tj)
    incl = same & (ti >= tj)
    blk8 = (i[:, None] // 8) == (i[None, :] // 8)
    eye = i[:, None] == i[None, :]
    cm = jnp.stack([strict, incl, blk8, eye]).astype(F32)
    return _seg_matrix(RWKV_DIM), tri, cm


def _rope_tables(s):
    half = HEAD_DIM // 2
    inv_freq = ROPE_THETA ** (-jnp.arange(half, dtype=F32) * 2.0 / HEAD_DIM)
    ang = jnp.arange(s, dtype=F32)[:, None] * inv_freq[None, :]
    cos, sin = jnp.cos(ang), jnp.sin(ang)
    reps = ATTN_DIM // HEAD_DIM
    cos_t = jnp.tile(jnp.concatenate([cos, cos], axis=1), (1, reps))
    sin_t = jnp.tile(jnp.concatenate([-sin, sin], axis=1), (1, reps))
    return cos_t, sin_t


def _row_tile(n, want):
    return want if n % want == 0 else n


def kernel(x, mix_norm, w_in, tshift_mu, vres_w_in, vres_mu, vres_v0, vres_v2, decay_w0, decay_w2, iclr_a0, iclr_a2, gate_g2, k_k, k_a, r_k, ln_x_w, ln_x_b, conv_w, conv_out_norm, q_norm, k_norm, attn_out_norm, w_o, ffn_norm, dense_wg, dense_wu, dense_wd, router, moe_wg, moe_wu, moe_wd):
    b, s, d = x.shape
    assert d == D_MODEL and s % DIL_SPAN == 0
    t = b * s
    depth = w_in.shape[0]
    tm = _row_tile(s, 512)
    seg384 = _seg_matrix(RWKV_DIM)
    seg256 = _seg_matrix(CONV_DIM)
    rwkv_consts = _rwkv_consts(b)
    cos_t, sin_t = _rope_tables(s)
    zeros_r = jnp.zeros((RWKV_DIM,), F32)

    x2d = x.reshape(t, d)
    v_first = None
    for l in range(depth):
        w_comb = w_in[l]
        if l > 0:
            w_comb = jnp.concatenate(
                [w_comb, jnp.pad(vres_w_in[l - 1], ((0, 0), (0, VRES_PAD - VRES_LORA)))], axis=1)
        reps = ATTN_DIM // HEAD_DIM
        outs = _inproj(x2d.reshape(b, s, d), mix_norm[l][None, :], w_comb.astype(BF16),
                       (conv_w[l], conv_out_norm[l].reshape(1, -1), seg256),
                       (cos_t, sin_t, jnp.tile(q_norm[l], reps)[None, :],
                        jnp.tile(k_norm[l], reps)[None, :], seg384), l > 0, tm)
        p_rwkv = outs[0].reshape(b, s, RWKV_IN)
        y_conv, q, k, v = outs[-4:]

        wl = jnp.zeros((LORA_W, 3 * RWKV_DIM), F32)
        wl = wl.at[0:32, 0:RWKV_DIM].set(decay_w2[l])
        wl = wl.at[32:64, RWKV_DIM:2 * RWKV_DIM].set(iclr_a2[l])
        wl = wl.at[64:128, 2 * RWKV_DIM:].set(gate_g2[l]).astype(BF16)
        vecs = jnp.stack([decay_w0[l], iclr_a0[l], k_k[l], k_a[l], r_k[l].reshape(-1), ln_x_w[l],
                          ln_x_b[l], vres_v0[l - 1] if l > 0 else zeros_r])
        mu = tshift_mu[l][None, :]
        if l == 0:
            y_rwkv, v_first = _rwkv(p_rwkv, mu, wl, vecs, rwkv_consts)
        else:
            vmu = jnp.pad(vres_mu[l - 1], (0, VRES_PAD - VRES_LORA))[None, :]
            v2 = jnp.pad(vres_v2[l - 1], ((0, VRES_PAD - VRES_LORA), (0, 0))).astype(BF16)
            (y_rwkv,) = _rwkv(p_rwkv, mu, wl, vecs, rwkv_consts,
                              vres=(outs[1].reshape(b, s, VRES_PAD), v_first, vmu, v2))

        res = [_attn(q, k, v, dil) for dil in DILATIONS]
        y_attn = _attn_combine([r[0] for r in res], [r[1] for r in res],
                               attn_out_norm[l].reshape(1, -1), _seg_matrix(PAIR), tm)

        flat = lambda a: a.reshape(t, -1)
        wo = w_o[l].astype(BF16)
        gain = ffn_norm[l][None, :]
        i = l // 2
        if l % 2 == 0:
            xn, h2 = _outproj(x2d, flat(y_rwkv), flat(y_conv), flat(y_attn), wo, gain, tm)
            tmf = _row_tile(t, 1024)
            n_tiles = t // tmf
            x2d = _ffn(h2, dense_wg[i][None].astype(BF16), dense_wu[i][None].astype(BF16),
                       dense_wd[i][None].astype(BF16), jnp.zeros((n_tiles,), jnp.int32),
                       jnp.full((1,), n_tiles, jnp.int32), tmf, 512, res=xn)
        else:
            rw = jnp.pad(router[i], ((0, 0), (0, LANES - N_EXPERTS)))
            rw_hi = rw.astype(BF16)
            rw_lo = (rw - rw_hi.astype(F32)).astype(BF16)
            rt = jnp.arange(tm)
            tri = (rt[:, None] > rt[None, :]).astype(BF16)
            xn, h2, info, cnt = _outproj(x2d, flat(y_rwkv), flat(y_conv), flat(y_attn), wo, gain,
                                         tm, router=(rw_hi, rw_lo, tri))
            tme = 1024
            n_tt = t // tm
            cnt_te = cnt.reshape(n_tt, 8, LANES)[:, 0, :N_EXPERTS].astype(jnp.int32)
            seg = (cnt_te + SEG_ALIGN - 1) // SEG_ALIGN * SEG_ALIGN
            run = jnp.cumsum(seg, axis=0) - seg
            padded = (jnp.sum(seg, axis=0) + tme - 1) // tme * tme
            ends = jnp.cumsum(padded)
            base = (ends - padded)[None, :] + run
            loff = jnp.cumsum(seg, axis=1) - seg
            n_rows = pl.cdiv(2 * t + n_tt * N_EXPERTS * (SEG_ALIGN - 1) + N_EXPERTS * tme, tme) * tme
            n_tiles = n_rows // tme
            tile_start = jnp.arange(n_tiles, dtype=jnp.int32) * tme
            tile_expert = jnp.minimum(jnp.sum(ends[None, :] <= tile_start[:, None], axis=1),
                                      N_EXPERTS - 1).astype(jnp.int32)
            n_used = (ends[-1] // tme).astype(jnp.int32).reshape(1)
            xs, dest_l = _dispatch(h2, info, base, loff, seg, n_rows, tm)
            dest = dest_l[:, 0:2].astype(jnp.int32)
            ys = _ffn(xs, moe_wg[i].astype(BF16), moe_wu[i].astype(BF16), moe_wd[i].astype(BF16),
                      tile_expert, n_used, tme, 512)
            x2d = _combine(xn, info, ys, dest, _row_tile(s, 512))
    return x2d.reshape(b, s, d)
```

```python
import functools

import jax
import jax.numpy as jnp
from jax import lax
from jax.experimental import pallas as pl
from jax.experimental.pallas import tpu as pltpu

F32 = jnp.float32
BF16 = jnp.bfloat16

D_MODEL = 1024
HEAD_DIM = 64
RWKV_DIM = 384
CONV_DIM = 256
ATTN_DIM = 384
LORA_W = 128
RWKV_IN = 3 * RWKV_DIM + LORA_W
CONV_IN = 3 * CONV_DIM
ATTN_IN = 3 * ATTN_DIM
VRES_LORA = 16
VRES_PAD = 128
CONV_WIDTH = 3
DILATIONS = (1, 4, 16)
ATTN_BLOCK = 128
DIL_SPAN = ATTN_BLOCK * 16
ROPE_THETA = 10000.0
D_FF = 3584
N_EXPERTS = 8
NORM_EPS = 1e-6
GN_EPS = 64e-5

LANES = 128
RWKV_CHUNK = 64
PAIR = 2 * HEAD_DIM
VMEM_LIMIT = 56 * 1024 * 1024


def _params(*sem):
    return pltpu.CompilerParams(dimension_semantics=sem, vmem_limit_bytes=VMEM_LIMIT)


def _bdot(a, b):
    return jnp.dot(a.astype(BF16), b.astype(BF16), preferred_element_type=F32)


def _bdot_nt(a, b):
    return lax.dot_general(a.astype(BF16), b.astype(BF16), (((1,), (1,)), ((), ())),
                           preferred_element_type=F32)


def _shift_rows(x, carry_row, n=1):
    row = lax.broadcasted_iota(jnp.int32, (x.shape[0], 1), 0)
    out = pltpu.roll(x, n, 0)
    for i in range(n):
        out = jnp.where(row == i, carry_row[i:i + 1, :], out)
    return out


def _inproj_body(x_ref, g_ref, w_ref, cw_ref, cgain_ref, cseg_ref, cos_ref, sin_ref, qg_ref, kg_ref,
                 aseg_ref, *rest, has_vres, tiles_per_seq):
    if has_vres:
        prw_ref, pv_ref, yc_ref, q_ref, k_ref, v_ref, carry_ref = rest
    else:
        prw_ref, yc_ref, q_ref, k_ref, v_ref, carry_ref = rest
    o_conv, o_attn, o_end = RWKV_IN, RWKV_IN + CONV_IN, RWKV_IN + CONV_IN + ATTN_IN
    first_tile = lax.rem(pl.program_id(0), tiles_per_seq) == 0
    n_split = 2
    rows_h = x_ref.shape[0] // n_split
    hs = []
    for c in range(n_split):
        x = x_ref[pl.ds(c * rows_h, rows_h), :]
        hs.append((x * lax.rsqrt(jnp.mean(x * x, axis=-1, keepdims=True) + NORM_EPS)
                   * g_ref[...]).astype(BF16))
    ps = [jnp.dot(h, w_ref[...], preferred_element_type=F32) for h in hs]
    for c, p in enumerate(ps):
        rows = pl.ds(c * rows_h, rows_h)
        prw_ref[rows, :] = p[:, :o_conv]
        if has_vres:
            pv_ref[rows, :] = p[:, o_end:]
        _conv_tile(p[:, o_conv:o_attn], first_tile if c == 0 else False, cw_ref, cgain_ref, cseg_ref,
                   yc_ref, carry_ref, rows)
        _attn_prep_tile(p[:, o_attn:o_end], cos_ref, sin_ref, qg_ref, kg_ref, aseg_ref,
                        q_ref, k_ref, v_ref, rows)


def _inproj(x, gain, w, conv, attn, has_vres, tm):
    b, s, d = x.shape
    t = b * s
    n = w.shape[1]
    tps = s // tm
    conv_w, conv_gain, conv_seg = conv
    cos, sin, qg, kg, attn_seg = attn
    full = lambda arr: pl.BlockSpec(arr.shape, lambda i: (0, 0))
    row = lambda wd: pl.BlockSpec((tm, wd), lambda i: (i, 0))
    tab = pl.BlockSpec((tm, ATTN_DIM), lambda i: (lax.rem(i, tps), 0))
    qkv = pl.BlockSpec((None, ATTN_DIM // PAIR, tm, PAIR), lambda i: (i // tps, 0, lax.rem(i, tps), 0))
    qkv_shape = jax.ShapeDtypeStruct((b, ATTN_DIM // PAIR, s, PAIR), F32)
    out_specs = [row(RWKV_IN)] + ([row(VRES_PAD)] if has_vres else []) + [row(CONV_DIM)] + [qkv] * 3
    out_shape = ([jax.ShapeDtypeStruct((t, RWKV_IN), F32)]
                 + ([jax.ShapeDtypeStruct((t, VRES_PAD), F32)] if has_vres else [])
                 + [jax.ShapeDtypeStruct((t, CONV_DIM), BF16)] + [qkv_shape] * 3)
    return pl.pallas_call(
        functools.partial(_inproj_body, has_vres=has_vres, tiles_per_seq=tps),
        grid=(t // tm,),
        in_specs=[row(d), full(gain), full(w), full(conv_w), full(conv_gain), full(conv_seg),
                  tab, tab, full(qg), full(kg), full(attn_seg)],
        out_specs=out_specs, out_shape=out_shape,
        scratch_shapes=[pltpu.VMEM((CONV_WIDTH - 1, CONV_DIM), F32)],
        compiler_params=_params("arbitrary"),
        name="inproj",
    )(x.reshape(t, d), gain, w, conv_w, conv_gain, conv_seg, cos, sin, qg, kg, attn_seg)


def _rwkv_body(*refs, has_vres):
    if has_vres:
        (p_ref, pv_ref, vf_ref, mu_ref, vmu_ref, wl_ref, v2_ref, vec_ref, seg_ref, tri_ref,
         cm_ref, y_ref, prev_ref, pprev_ref, h_ref) = refs
    else:
        (p_ref, mu_ref, wl_ref, vec_ref, seg_ref, tri_ref,
         cm_ref, y_ref, vout_ref, prev_ref, h_ref) = refs
    @pl.when(pl.program_id(0) == 0)
    def _():
        prev_ref[...] = jnp.zeros_like(prev_ref)
        h_ref[...] = jnp.zeros_like(h_ref)
        if has_vres:
            pprev_ref[...] = jnp.zeros_like(pprev_ref)

    nb, nt = p_ref.shape[0], p_ref.shape[1]

    def token_shift(src_ref, carry_ref, mix):
        out = []
        for bi in range(nb):
            cur = src_ref[bi]
            prev = _shift_rows(cur, carry_ref[bi:bi + 1, :])
            carry_ref[bi:bi + 1, :] = cur[nt - 1:nt, :]
            out.append(cur + (prev - cur) * mix)
        return jnp.concatenate(out, axis=0)

    x = token_shift(p_ref, prev_ref, mu_ref[...])
    r = x[:, 0:RWKV_DIM]
    k = x[:, RWKV_DIM:2 * RWKV_DIM]
    v = x[:, 2 * RWKV_DIM:3 * RWKV_DIM]
    z = x[:, 3 * RWKV_DIM:]
    lane = lax.broadcasted_iota(jnp.int32, (1, LANES), 1)
    zt = jnp.where(lane < 32, jnp.tanh(z), jnp.where(lane < 64, z, jax.nn.sigmoid(z)))
    lo = _bdot(zt, wl_ref[...])
    w0, a0, k_k, k_a, r_k, ln_w, ln_b, v0 = (vec_ref[i:i + 1, :] for i in range(8))
    seg = seg_ref[...]

    zz = -(w0 + lo[:, 0:RWKV_DIM])
    softplus = jnp.maximum(zz, 0.0) + jnp.log(1.0 + jnp.exp(-jnp.abs(zz)))
    lw = -jnp.exp(-softplus - 0.5)
    a = jax.nn.sigmoid(a0 + lo[:, RWKV_DIM:2 * RWKV_DIM])
    g = lo[:, 2 * RWKV_DIM:]
    if has_vres:
        xv = token_shift(pv_ref, pprev_ref, vmu_ref[...])
        v_first = vf_ref[...].reshape(nb * nt, RWKV_DIM)
        v = v + (v_first - v) * jax.nn.sigmoid(v0 + _bdot(xv, v2_ref[...]))
    else:
        vout_ref[...] = v.reshape(nb, nt, RWKV_DIM)
    kk = k * k_k
    kk = kk * jnp.minimum(lax.rsqrt(_bdot(kk * kk, seg)), 1e12)
    kmod = k * (1.0 + (a - 1.0) * k_a)
    bonus = _bdot(r * kmod * r_k, seg) * v

    tri = tri_ref[...]
    lw_hi = lw.astype(BF16)
    lw_lo = (lw - lw_hi.astype(F32)).astype(BF16)
    cum = (jnp.dot(tri, lw_hi, preferred_element_type=F32)
           + jnp.dot(tri, lw_lo, preferred_element_type=F32))
    pinv = jnp.exp(-cum)
    r_t = r * jnp.exp(cum)
    a_t = -kk * jnp.exp(cum - lw)
    b_t = kk * a * pinv
    k_t = kmod * pinv
    p_ends = [jnp.exp(cum[bi * nt + nt - 1:(bi + 1) * nt, :]) for bi in range(nb)]
    p_end_rows = jnp.concatenate([jnp.broadcast_to(pe, (nt, RWKV_DIM)) for pe in p_ends], axis=0)
    bh_t = b_t * p_end_rows
    kh_t = k_t * p_end_rows

    strict, incl, blk8, eye = (cm_ref[i] for i in range(4))
    m0 = (lane < HEAD_DIM).astype(F32)
    m1 = 1.0 - m0

    def stack(t, rows, sl):
        ts = t[rows, sl]
        return jnp.concatenate([ts * m0, ts * m1], axis=0)

    npair = RWKV_DIM // PAIR
    chains = [(bi, j) for bi in range(nb) for j in range(npair)]
    each = lambda fn, *cols: [fn(*args) for args in zip(*cols)]
    cat0 = lambda *ts: jnp.concatenate(ts, axis=0)
    cat1 = lambda *ts: jnp.concatenate(ts, axis=1)

    def stacks(t):
        return [stack(t, slice(bi * nt, (bi + 1) * nt), slice(j * PAIR, (j + 1) * PAIR))
                for bi, j in chains]

    a_st, r_st, b_st, k_st, v_st, bh_st, kh_st = (stacks(t) for t in (a_t, r_t, b_t, k_t, v, bh_t, kh_t))
    gram = each(lambda a_, r_, b_, k_: _bdot_nt(cat0(a_, r_), cat0(b_, k_)), a_st, r_st, b_st, k_st)
    n2 = 2 * nt
    a_ab = [gm[:n2, :n2] * strict for gm in gram]
    a_ak = [gm[:n2, n2:] * strict for gm in gram]
    a_rb = [gm[n2:, :n2] * incl for gm in gram]
    a_rk = [gm[n2:, n2:] * incl for gm in gram]
    dg = [m * blk8 for m in a_ab]
    off = each(lambda m, d_: m - d_, a_ab, dg)
    d2 = each(_bdot, dg, dg)
    tdiag = [eye + d_ for d_ in dg]
    tdiag = each(lambda t_, d_: t_ + _bdot(t_, d_), tdiag, d2)
    d4 = each(_bdot, d2, d2)
    tdiag = each(lambda t_, d_: t_ + _bdot(t_, d_), tdiag, d4)
    e1 = each(_bdot, tdiag, off)
    e2 = each(_bdot, e1, e1)
    yb = [eye + e_ for e_ in e1]
    yb = each(lambda y_, e_: y_ + _bdot(y_, e_), yb, e2)
    e4 = each(_bdot, e2, e2)
    yb = each(lambda y_, e_: y_ + _bdot(y_, e_), yb, e4)
    tinv = each(_bdot, yb, tdiag)

    hs = [h_ref[bi, j] for bi, j in chains]
    xs = each(lambda a_, ak_, h_, v_: _bdot(cat1(a_, ak_), cat0(h_, v_)), a_st, a_ak, hs, v_st)
    us = each(_bdot, tinv, xs)
    y_st = each(lambda r_, rb_, rk_, h_, u_, v_: _bdot(cat1(r_, rb_, rk_), cat0(h_, u_, v_)),
                r_st, a_rb, a_rk, hs, us, v_st)
    pe_col = [jnp.sum(eye * p_ends[bi][:, j * PAIR:(j + 1) * PAIR], axis=1, keepdims=True)
              for bi, j in chains]
    h_new = each(lambda pc_, h_, bh_, kh_, u_, v_: pc_ * h_ + _bdot(cat0(bh_, kh_).T, cat0(u_, v_)),
                 pe_col, hs, bh_st, kh_st, us, v_st)
    for (bi, j), hn in zip(chains, h_new):
        h_ref[bi, j] = hn
    y_pair = [ys_[:nt] + ys_[nt:] for ys_ in y_st]
    y = cat0(*[cat1(*y_pair[bi * npair:(bi + 1) * npair]) for bi in range(nb)])
    inv_n = 1.0 / HEAD_DIM
    mean = _bdot(y, seg) * inv_n
    dy = y - mean
    var = _bdot(dy * dy, seg) * inv_n
    yn = dy * lax.rsqrt(var + GN_EPS) * ln_w + ln_b
    y_ref[...] = ((yn + bonus) * g).reshape(nb, nt, RWKV_DIM).astype(y_ref.dtype)


def _rwkv(p_rwkv, mu, wl, vecs, consts, vres=None):
    b, s, _ = p_rwkv.shape
    c = RWKV_CHUNK
    seg, tri, cm = consts
    tile = lambda wd: pl.BlockSpec((b, c, wd), lambda j: (0, j, 0))
    full = lambda arr: pl.BlockSpec(arr.shape, lambda j: (0,) * arr.ndim)
    y_shape = jax.ShapeDtypeStruct((b, s, RWKV_DIM), BF16)
    scratch = [pltpu.VMEM((b, RWKV_IN), F32)]
    if vres is None:
        ins = [p_rwkv, mu, wl, vecs, seg, tri, cm]
        in_specs = [tile(RWKV_IN)] + [full(t) for t in ins[1:]]
        out_shape = [y_shape, jax.ShapeDtypeStruct((b, s, RWKV_DIM), F32)]
        out_specs = [tile(RWKV_DIM), tile(RWKV_DIM)]
    else:
        p_vres, v_first, vmu, v2 = vres
        ins = [p_rwkv, p_vres, v_first, mu, vmu, wl, v2, vecs, seg, tri, cm]
        in_specs = [tile(RWKV_IN), tile(VRES_PAD), tile(RWKV_DIM)] + [full(t) for t in ins[3:]]
        out_shape = [y_shape]
        out_specs = [tile(RWKV_DIM)]
        scratch.append(pltpu.VMEM((b, VRES_PAD), F32))
    scratch.append(pltpu.VMEM((b, RWKV_DIM // PAIR, PAIR, PAIR), F32))
    return pl.pallas_call(
        functools.partial(_rwkv_body, has_vres=vres is not None),
        grid=(s // c,),
        in_specs=in_specs, out_specs=out_specs, out_shape=out_shape,
        scratch_shapes=scratch,
        compiler_params=_params("arbitrary"),
        name="rwkv7",
    )(*ins)


def _conv_tile(p, first_tile, w_ref, gain_ref, seg_ref, o_ref, carry_ref, rows):
    if first_tile is not False:
        @pl.when(first_tile)
        def _():
            carry_ref[...] = jnp.zeros_like(carry_ref)

    nt = p.shape[0]
    u = p[:, CONV_DIM:2 * CONV_DIM] * p[:, 2 * CONV_DIM:]
    carry = carry_ref[...]
    u1 = _shift_rows(u, carry[1:2, :], 1)
    u2 = _shift_rows(u, carry, 2)
    carry_ref[...] = u[nt - 2:nt, :]
    w = w_ref[...]
    y = p[:, 0:CONV_DIM] * (w[0:1, :] * u2 + w[1:2, :] * u1 + w[2:3, :] * u)
    ms = _bdot(y * y, seg_ref[...]) * (1.0 / HEAD_DIM)
    o_ref[rows, :] = (y * lax.rsqrt(ms + NORM_EPS) * gain_ref[...]).astype(o_ref.dtype)


def _attn_prep_tile(p, cos_ref, sin_ref, qg_ref, kg_ref, seg_ref, q_ref, k_ref, v_ref, rows):
    seg = seg_ref[...]
    cos, sin = cos_ref[rows, :], sin_ref[rows, :]
    lane = lax.broadcasted_iota(jnp.int32, (1, LANES), 1)
    first = (lane & (HEAD_DIM - 1)) < HEAD_DIM // 2
    for part, g_ref, o_ref, scale in ((0, qg_ref, q_ref, HEAD_DIM ** -0.5), (1, kg_ref, k_ref, 1.0)):
        x = p[:, part * ATTN_DIM:(part + 1) * ATTN_DIM]
        ms = _bdot(x * x, seg) * (1.0 / HEAD_DIM)
        xn = x * lax.rsqrt(ms + NORM_EPS) * g_ref[...]
        rot = []
        for j in range(ATTN_DIM // LANES):
            xs = xn[:, j * LANES:(j + 1) * LANES]
            rot.append(jnp.where(first, pltpu.roll(xs, LANES - HEAD_DIM // 2, 1),
                                 pltpu.roll(xs, HEAD_DIM // 2, 1)))
        out = (xn * cos + jnp.concatenate(rot, axis=1) * sin) * scale
        for j in range(ATTN_DIM // PAIR):
            o_ref[j, rows, :] = out[:, j * PAIR:(j + 1) * PAIR]
    for j in range(ATTN_DIM // PAIR):
        v_ref[j, rows, :] = p[:, 2 * ATTN_DIM + j * PAIR:2 * ATTN_DIM + (j + 1) * PAIR]


def _pair_major_spec(rows, row_index):
    return pl.BlockSpec((None, ATTN_DIM // PAIR, rows, PAIR), lambda i, j: (i, 0, row_index(j), 0))


def _attn_body(q_ref, k_ref, v_ref, kprev_ref, vprev_ref, o_ref, l_ref, *, dil):
    span = pl.program_id(1)
    blk = ATTN_BLOCK
    npair = ATTN_DIM // PAIR
    qi = lax.broadcasted_iota(jnp.int32, (blk, 2 * blk), 0)
    kj = lax.broadcasted_iota(jnp.int32, (blk, 2 * blk), 1)
    dist = qi + blk - kj
    band = (dist >= 0) & (dist <= ATTN_BLOCK)
    lane = lax.broadcasted_iota(jnp.int32, (1, LANES), 1)
    head0 = lane < HEAD_DIM
    m0 = head0.astype(F32)
    heads = [(j, hm) for j in range(npair) for hm in (m0, 1.0 - m0)]

    def block_attn(q, kcat, vcat, block_index):
        mask = band & (kj + block_index * (2 * blk) >= blk)
        s_all = [_bdot_nt(q[j] * hm, kcat[j]) for j, hm in heads]
        s_all = [jnp.where(mask, s, -jnp.inf) for s in s_all]
        m_all = [jnp.max(s, axis=-1, keepdims=True) for s in s_all]
        pe_all = [jnp.exp(s - m) for s, m in zip(s_all, m_all)]
        den_all = [jnp.sum(pe, axis=-1, keepdims=True) for pe in pe_all]
        o_all = [_bdot(pe, vcat[j]) * (1.0 / den) for pe, den, (j, _) in zip(pe_all, den_all, heads)]
        lse_all = [m + jnp.log(den) for m, den in zip(m_all, den_all)]
        o_pair = [jnp.where(head0, o_all[2 * j], o_all[2 * j + 1]) for j in range(npair)]
        l_pair = [jnp.where(head0, lse_all[2 * j], lse_all[2 * j + 1]) for j in range(npair)]
        return o_pair, l_pair

    if dil == 1:
        nq = q_ref.shape[1] // blk
        kext = [jnp.concatenate([kprev_ref[j], k_ref[j]], axis=0) for j in range(npair)]
        vext = [jnp.concatenate([vprev_ref[j], v_ref[j]], axis=0) for j in range(npair)]
        for qb in range(nq):
            rows = slice(qb * blk, (qb + 1) * blk)
            keys = slice(qb * blk, (qb + 2) * blk)
            o_pair, l_pair = block_attn([q_ref[j, rows, :] for j in range(npair)],
                                        [kx[keys] for kx in kext], [vx[keys] for vx in vext],
                                        span * nq + qb)
            for j in range(npair):
                o_ref[j, rows, :] = o_pair[j]
                l_ref[j, rows, :] = l_pair[j]
    else:
        def one_class(r, carry):
            rows = pl.ds(r, blk, stride=dil)
            cat = lambda prev_ref, ref: [jnp.concatenate([prev_ref[j, rows, :], ref[j, rows, :]], axis=0)
                                         for j in range(npair)]
            o_pair, l_pair = block_attn([q_ref[j, rows, :] for j in range(npair)],
                                        cat(kprev_ref, k_ref), cat(vprev_ref, v_ref), span)
            for j in range(npair):
                o_ref[j, rows, :] = o_pair[j]
                l_ref[j, rows, :] = l_pair[j]
            return carry
        lax.fori_loop(0, dil, one_class, 0)


def _attn(q, k, v, dil):
    b, _, s, _ = q.shape
    if dil == 1:
        nq = 4
        span = ATTN_BLOCK * nq
        prev = _pair_major_spec(ATTN_BLOCK, lambda m: jnp.maximum(m * nq - 1, 0))
    else:
        span = ATTN_BLOCK * dil
        prev = _pair_major_spec(span, lambda m: jnp.maximum(m - 1, 0))
    spec = _pair_major_spec(span, lambda m: m)
    shp = jax.ShapeDtypeStruct(q.shape, F32)
    return pl.pallas_call(
        functools.partial(_attn_body, dil=dil),
        grid=(b, s // span),
        in_specs=[spec, spec, spec, prev, prev],
        out_specs=[spec, spec],
        out_shape=[shp, shp],
        compiler_params=_params("arbitrary", "arbitrary"),
        name=f"attn_d{dil}",
    )(q, k, v, k, v)


def _attn_combine_body(*refs):
    n = len(DILATIONS)
    o_refs, l_refs = refs[:n], refs[n:2 * n]
    gain_ref, seg_ref, y_ref = refs[2 * n:]
    for j in range(ATTN_DIM // PAIR):
        sl = slice(j * PAIR, (j + 1) * PAIR)
        ls = [r[j] for r in l_refs]
        m = functools.reduce(jnp.maximum, ls)
        es = [jnp.exp(l - m) for l in ls]
        den = functools.reduce(jnp.add, es)
        o = functools.reduce(jnp.add, [e * r[j] for e, r in zip(es, o_refs)]) / den
        ms = _bdot(o * o, seg_ref[...]) * (1.0 / HEAD_DIM)
        y_ref[:, sl] = (o * lax.rsqrt(ms + NORM_EPS) * gain_ref[:, sl]).astype(y_ref.dtype)


def _attn_combine(os_, ls_, gain, seg, tm):
    b, _, s, _ = os_[0].shape
    tile = _pair_major_spec(tm, lambda j: j)
    full = lambda arr: pl.BlockSpec(arr.shape, lambda i, j: (0, 0))
    return pl.pallas_call(
        _attn_combine_body,
        grid=(b, s // tm),
        in_specs=[tile] * (2 * len(os_)) + [full(gain), full(seg)],
        out_specs=pl.BlockSpec((None, tm, ATTN_DIM), lambda i, j: (i, j, 0)),
        out_shape=jax.ShapeDtypeStruct((b, s, ATTN_DIM), BF16),
        compiler_params=_params("arbitrary", "arbitrary"),
        name="attn_combine",
    )(*os_, *ls_, gain, seg)


INFO_E0, INFO_E1, INFO_R0, INFO_R1, INFO_G0, INFO_G1 = range(6)


def _mix_residual(x_ref, yr_ref, yc_ref, ya_ref, wo_ref, g_ref):
    o1, o2 = RWKV_DIM, RWKV_DIM + CONV_DIM
    acc = jnp.dot(yr_ref[...], wo_ref[0:o1, :], preferred_element_type=F32)
    acc += jnp.dot(yc_ref[...], wo_ref[o1:o2, :], preferred_element_type=F32)
    acc += jnp.dot(ya_ref[...], wo_ref[o2:, :], preferred_element_type=F32)
    xn = x_ref[...] + acc
    h = xn * lax.rsqrt(jnp.mean(xn * xn, axis=-1, keepdims=True) + NORM_EPS) * g_ref[...]
    return xn, h


def _outproj_body(*refs, has_router):
    if has_router:
        (x_ref, yr_ref, yc_ref, ya_ref, wo_ref, g_ref, rwh_ref, rwl_ref, tri_ref,
         xn_ref, h_ref, info_ref, cnt_ref) = refs
    else:
        x_ref, yr_ref, yc_ref, ya_ref, wo_ref, g_ref, xn_ref, h_ref = refs
    xn, h = _mix_residual(x_ref, yr_ref, yc_ref, ya_ref, wo_ref, g_ref)
    xn_ref[...] = xn
    h_ref[...] = h.astype(h_ref.dtype)
    if not has_router:
        return

    h_hi = h.astype(BF16)
    h_lo = (h - h_hi.astype(F32)).astype(BF16)
    rwh = rwh_ref[...]
    logits = (jnp.dot(h_hi, rwh, preferred_element_type=F32)
              + jnp.dot(h_lo, rwh, preferred_element_type=F32)
              + jnp.dot(h_hi, rwl_ref[...], preferred_element_type=F32))
    lane = lax.broadcasted_iota(jnp.int32, (1, LANES), 1)
    logits = jnp.where(lane < N_EXPERTS, logits, -jnp.inf)

    n_grp = 4
    rows_g = logits.shape[0] // n_grp
    lg = [logits[g * rows_g:(g + 1) * rows_g] for g in range(n_grp)]

    def top(ls):
        ms = [jnp.max(l, axis=-1, keepdims=True) for l in ls]
        idx = [jnp.min(jnp.where(l == m, lane, LANES), axis=-1, keepdims=True) for l, m in zip(ls, ms)]
        return ms, idx, [lane == ix for ix in idx]

    m1, i1, oh1 = top(lg)
    m2, i2, oh2 = top([jnp.where(o, -jnp.inf, l) for o, l in zip(oh1, lg)])
    cat = lambda parts: jnp.concatenate(parts, axis=0)
    m1, i1, oh1, m2, i2, oh2 = (cat(v) for v in (m1, i1, oh1, m2, i2, oh2))
    zexp = jnp.exp(m2 - m1)
    g0 = 1.0 / (1.0 + zexp)
    g1 = zexp * g0
    oh = oh1.astype(F32) + oh2.astype(F32)
    tot = jnp.dot(tri_ref[...], oh.astype(BF16), preferred_element_type=F32)
    r0 = jnp.sum(jnp.where(oh1, tot, 0.0), axis=-1, keepdims=True)
    r1 = jnp.sum(jnp.where(oh2, tot, 0.0), axis=-1, keepdims=True)
    cnt_ref[...] = jnp.broadcast_to(jnp.sum(oh, axis=0, keepdims=True), cnt_ref.shape)
    info = jnp.zeros(logits.shape, F32)
    for ln, val in ((INFO_E0, i1.astype(F32)), (INFO_E1, i2.astype(F32)), (INFO_R0, r0),
                    (INFO_R1, r1), (INFO_G0, g0), (INFO_G1, g1)):
        info = jnp.where(lane == ln, val, info)
    info_ref[...] = info


def _outproj(x2d, yr, yc, ya, wo, gain, tm, router=None):
    t, d = x2d.shape
    row = lambda wd: pl.BlockSpec((tm, wd), lambda i: (i, 0))
    full = lambda arr: pl.BlockSpec(arr.shape, lambda i: (0, 0))
    ins = [x2d, yr, yc, ya, wo, gain]
    in_specs = [row(d), row(RWKV_DIM), row(CONV_DIM), row(ATTN_DIM), full(wo), full(gain)]
    out_shape = [jax.ShapeDtypeStruct((t, d), F32)]
    out_specs = [row(d)]
    scratch = []
    if router is None:
        out_shape.append(jax.ShapeDtypeStruct((t, d), BF16))
        out_specs.append(row(d))
    else:
        ins += list(router)
        in_specs += [full(a) for a in router]
        out_shape += [jax.ShapeDtypeStruct((t, d), BF16), jax.ShapeDtypeStruct((t, LANES), F32),
                      jax.ShapeDtypeStruct((t // tm * 8, LANES), F32)]
        out_specs += [row(d), row(LANES), pl.BlockSpec((8, LANES), lambda i: (i, 0))]
    return pl.pallas_call(
        functools.partial(_outproj_body, has_router=router is not None),
        grid=(t // tm,),
        in_specs=in_specs, out_specs=out_specs, out_shape=out_shape,
        scratch_shapes=scratch,
        compiler_params=_params("arbitrary"),
        name="outproj",
    )(*ins)


def _ffn_body(te_ref, nu_ref, *refs, from_mixers):
    if from_mixers:
        (xres_ref, yr_ref, yc_ref, ya_ref, wo_ref, g_ref, wg_ref, wu_ref, wd_ref, o_ref,
         acc_ref, xn_ref, h_ref) = refs
    else:
        h_ref, wg_ref, wu_ref, wd_ref, o_ref, acc_ref = refs
    i, f = pl.program_id(0), pl.program_id(1)

    @pl.when(f == 0)
    def _():
        acc_ref[...] = jnp.zeros_like(acc_ref)
        if from_mixers:
            xn, h = _mix_residual(xres_ref, yr_ref, yc_ref, ya_ref, wo_ref, g_ref)
            xn_ref[...] = xn
            h_ref[...] = h

    @pl.when(i < nu_ref[0])
    def _():
        x = h_ref[...].astype(BF16)
        hg = jnp.dot(x, wg_ref[...], preferred_element_type=F32)
        hu = jnp.dot(x, wu_ref[...], preferred_element_type=F32)
        act = hg * jax.nn.sigmoid(hg) * hu
        acc_ref[...] += jnp.dot(act.astype(BF16), wd_ref[...], preferred_element_type=F32)

    @pl.when(f == pl.num_programs(1) - 1)
    def _():
        out = acc_ref[...]
        if from_mixers:
            out = out + xn_ref[...]
        o_ref[...] = out.astype(o_ref.dtype)


def _ffn(x, wg, wu, wd, tile_expert, n_used, tm, tf, mixers=None):
    ff = wg.shape[-1]
    row = lambda wd_: pl.BlockSpec((tm, wd_), lambda i, f, te, nu: (i, 0))
    full = lambda arr: pl.BlockSpec(arr.shape, lambda i, f, te, nu: (0, 0))
    weights = [pl.BlockSpec((None, wg.shape[1], tf), lambda i, f, te, nu: (te[i], 0, f)),
               pl.BlockSpec((None, wg.shape[1], tf), lambda i, f, te, nu: (te[i], 0, f)),
               pl.BlockSpec((None, tf, wg.shape[1]), lambda i, f, te, nu: (te[i], f, 0))]
    if mixers is None:
        n, d = x.shape
        ins = [x, wg, wu, wd]
        in_specs = [row(d)] + weights
        scratch = [pltpu.VMEM((tm, d), F32)]
    else:
        x_res, yr, yc, ya, wo, gain = mixers
        n, d = x_res.shape
        ins = [x_res, yr, yc, ya, wo, gain, wg, wu, wd]
        in_specs = [row(d), row(RWKV_DIM), row(CONV_DIM), row(ATTN_DIM), full(wo), full(gain)] + weights
        scratch = [pltpu.VMEM((tm, d), F32) for _ in range(3)]
    return pl.pallas_call(
        functools.partial(_ffn_body, from_mixers=mixers is not None),
        grid_spec=pltpu.PrefetchScalarGridSpec(
            num_scalar_prefetch=2,
            grid=(n // tm, ff // tf),
            in_specs=in_specs,
            out_specs=row(d),
            scratch_shapes=scratch),
        out_shape=jax.ShapeDtypeStruct((n, d), F32),
        compiler_params=_params("arbitrary", "arbitrary"),
        name="swiglu",
    )(tile_expert, n_used, *ins)


SEG_ALIGN = 16


def _stage_positions(info, loffv_ref):
    lane_f = lax.broadcasted_iota(jnp.int32, (1, LANES), 1).astype(F32)
    pos = []
    for e_ln, r_ln in ((INFO_E0, INFO_R0), (INFO_E1, INFO_R1)):
        onehot = lane_f == info[:, e_ln:e_ln + 1]
        seg_start = jnp.sum(jnp.where(onehot, loffv_ref[0:1, :], 0.0), axis=-1, keepdims=True)
        pos.append(seg_start + info[:, r_ln:r_ln + 1])
    return pos


def _segment_copies(tile, tm, base_ref, loff_ref, seg_ref, make_copy, start):
    for e in range(N_EXPERTS):
        seg = seg_ref[tile * N_EXPERTS + e]
        size = tm
        while size >= SEG_ALIGN:
            off = seg & ~(2 * size - 1)

            @pl.when((seg & size) != 0)
            def _():
                cp = make_copy(pl.multiple_of(base_ref[tile * N_EXPERTS + e] + off, SEG_ALIGN),
                               pl.multiple_of(loff_ref[tile * N_EXPERTS + e] + off, SEG_ALIGN), size)
                if start:
                    cp.start()
                else:
                    cp.wait()
            size //= 2


def _dispatch_body(base_ref, loff_ref, seg_ref, h_ref, info_ref, loffv_ref, xs_in, xs_out,
                   stage_ref, sem):
    del xs_in
    i = pl.program_id(0)
    tm = h_ref.shape[0]
    n_stage = stage_ref.shape[0]
    lane = lax.broadcasted_iota(jnp.int32, (1, LANES), 1)
    pos = _stage_positions(info_ref[...], loffv_ref)
    pos_t = jnp.transpose(jnp.where(lane == 0, pos[0], jnp.where(lane == 1, pos[1], -1.0)))
    rows = lax.broadcasted_iota(jnp.int32, (n_stage, 1), 0).astype(F32)
    perm = ((rows == pos_t[0:1, :]) | (rows == pos_t[1:2, :])).astype(BF16)
    stage_ref[...] = jnp.dot(perm, h_ref[...].astype(BF16),
                             preferred_element_type=F32).astype(stage_ref.dtype)
    make_copy = lambda row, srow, size: pltpu.make_async_copy(
        stage_ref.at[pl.ds(srow, size)], xs_out.at[pl.ds(row, size)], sem)
    for start in (True, False):
        _segment_copies(i, tm, base_ref, loff_ref, seg_ref, make_copy, start)


def _moe_specs(tm):
    row = lambda wd: pl.BlockSpec((tm, wd), lambda i, *_: (i, 0))
    vec = pl.BlockSpec((8, LANES), lambda i, *_: (i, 0))
    lanes8 = lambda a: jnp.repeat(jnp.pad(a.astype(F32), ((0, 0), (0, LANES - N_EXPERTS))), 8, axis=0)
    flat = lambda a: a.reshape(-1).astype(jnp.int32)
    return row, vec, lanes8, flat


def _dispatch(h2, info, base, loff, seg, n_rows, tm):
    t, d = h2.shape
    n_stage = 2 * tm + N_EXPERTS * SEG_ALIGN
    row, vec, lanes8, flat = _moe_specs(tm)
    any_spec = pl.BlockSpec(memory_space=pl.ANY)
    return pl.pallas_call(
        _dispatch_body,
        grid_spec=pltpu.PrefetchScalarGridSpec(
            num_scalar_prefetch=3,
            grid=(t // tm,),
            in_specs=[row(d), row(LANES), vec, any_spec],
            out_specs=any_spec,
            scratch_shapes=[pltpu.VMEM((n_stage, d), BF16), pltpu.SemaphoreType.DMA]),
        out_shape=jax.ShapeDtypeStruct((n_rows, d), BF16),
        input_output_aliases={6: 0},
        compiler_params=_params("arbitrary"),
        name="moe_dispatch",
    )(flat(base), flat(loff), flat(seg), h2, info, lanes8(loff), jnp.zeros((n_rows, d), BF16))


def _combine_body(base_ref, loff_ref, seg_ref, x_ref, info_ref, loffv_ref, ys_hbm, o_ref,
                  stage_ref, sem):
    i = pl.program_id(0)
    tm = x_ref.shape[0]
    n_stage = stage_ref.shape[1]
    slot = lax.rem(i, 2)

    def copies(tile, sl, start):
        make_copy = lambda row, srow, size: pltpu.make_async_copy(
            ys_hbm.at[pl.ds(row, size)], stage_ref.at[sl, pl.ds(srow, size)], sem.at[sl])
        _segment_copies(tile, tm, base_ref, loff_ref, seg_ref, make_copy, start)

    @pl.when(i == 0)
    def _():
        stage_ref[...] = jnp.zeros_like(stage_ref)
        copies(0, 0, True)

    @pl.when(i + 1 < pl.num_programs(0))
    def _():
        copies(i + 1, 1 - slot, True)

    info = info_ref[...]
    pos = _stage_positions(info, loffv_ref)
    col = lax.broadcasted_iota(jnp.int32, (1, n_stage), 1).astype(F32)
    sel = (jnp.where(col == pos[0], info[:, INFO_G0:INFO_G0 + 1], 0.0)
           + jnp.where(col == pos[1], info[:, INFO_G1:INFO_G1 + 1], 0.0))
    sel_hi = sel.astype(BF16)
    sel_lo = (sel - sel_hi.astype(F32)).astype(BF16)
    copies(i, slot, False)
    ys_tile = stage_ref[slot].astype(BF16)
    o_ref[...] = (x_ref[...] + jnp.dot(sel_hi, ys_tile, preferred_element_type=F32)
                  + jnp.dot(sel_lo, ys_tile, preferred_element_type=F32))


def _combine(xn, info, ys, base, loff, seg, tm):
    t, d = xn.shape
    n_stage = 2 * tm + N_EXPERTS * SEG_ALIGN
    row, vec, lanes8, flat = _moe_specs(tm)
    return pl.pallas_call(
        _combine_body,
        grid_spec=pltpu.PrefetchScalarGridSpec(
            num_scalar_prefetch=3,
            grid=(t // tm,),
            in_specs=[row(d), row(LANES), vec, pl.BlockSpec(memory_space=pl.ANY)],
            out_specs=row(d),
            scratch_shapes=[pltpu.VMEM((2, n_stage, d), F32), pltpu.SemaphoreType.DMA((2,))]),
        out_shape=jax.ShapeDtypeStruct((t, d), F32),
        compiler_params=_params("arbitrary"),
        name="moe_combine",
    )(flat(base), flat(loff), flat(seg), xn, info, lanes8(loff), ys)


def _seg_matrix(n):
    i = jnp.arange(n) // HEAD_DIM
    return (i[:, None] == i[None, :]).astype(BF16)


def _rwkv_consts(nb):
    c = RWKV_CHUNK
    t = jnp.arange(nb * c)
    tri = ((t[:, None] >= t[None, :]) & (t[:, None] // c == t[None, :] // c)).astype(BF16)
    i = jnp.arange(2 * c)
    same = (i[:, None] // c) == (i[None, :] // c)
    ti, tj = (i % c)[:, None], (i % c)[None, :]
    strict = same & (ti > tj)
    incl = same & (ti >= tj)
    blk8 = (i[:, None] // 8) == (i[None, :] // 8)
    eye = i[:, None] == i[None, :]
    cm = jnp.stack([strict, incl, blk8, eye]).astype(F32)
    return _seg_matrix(RWKV_DIM), tri, cm


def _rope_tables(s):
    half = HEAD_DIM // 2
    inv_freq = ROPE_THETA ** (-jnp.arange(half, dtype=F32) * 2.0 / HEAD_DIM)
    ang = jnp.arange(s, dtype=F32)[:, None] * inv_freq[None, :]
    cos, sin = jnp.cos(ang), jnp.sin(ang)
    reps = ATTN_DIM // HEAD_DIM
    cos_t = jnp.tile(jnp.concatenate([cos, cos], axis=1), (1, reps))
    sin_t = jnp.tile(jnp.concatenate([-sin, sin], axis=1), (1, reps))
    return cos_t, sin_t


def _row_tile(n, want):
    return want if n % want == 0 else n


def kernel(x, mix_norm, w_in, tshift_mu, vres_w_in, vres_mu, vres_v0, vres_v2, decay_w0, decay_w2, iclr_a0, iclr_a2, gate_g2, k_k, k_a, r_k, ln_x_w, ln_x_b, conv_w, conv_out_norm, q_norm, k_norm, attn_out_norm, w_o, ffn_norm, dense_wg, dense_wu, dense_wd, router, moe_wg, moe_wu, moe_wd):
    b, s, d = x.shape
    assert d == D_MODEL and s % DIL_SPAN == 0
    t = b * s
    depth = w_in.shape[0]
    tm = _row_tile(s, 512)
    seg384 = _seg_matrix(RWKV_DIM)
    seg256 = _seg_matrix(CONV_DIM)
    rwkv_consts = _rwkv_consts(b)
    cos_t, sin_t = _rope_tables(s)
    zeros_r = jnp.zeros((RWKV_DIM,), F32)

    x2d = x.reshape(t, d)
    v_first = None
    for l in range(depth):
        w_comb = w_in[l]
        if l > 0:
            w_comb = jnp.concatenate(
                [w_comb, jnp.pad(vres_w_in[l - 1], ((0, 0), (0, VRES_PAD - VRES_LORA)))], axis=1)
        reps = ATTN_DIM // HEAD_DIM
        outs = _inproj(x2d.reshape(b, s, d), mix_norm[l][None, :], w_comb.astype(BF16),
                       (conv_w[l], conv_out_norm[l].reshape(1, -1), seg256),
                       (cos_t, sin_t, jnp.tile(q_norm[l], reps)[None, :],
                        jnp.tile(k_norm[l], reps)[None, :], seg384), l > 0, tm)
        p_rwkv = outs[0].reshape(b, s, RWKV_IN)
        y_conv, q, k, v = outs[-4:]

        wl = jnp.zeros((LORA_W, 3 * RWKV_DIM), F32)
        wl = wl.at[0:32, 0:RWKV_DIM].set(decay_w2[l])
        wl = wl.at[32:64, RWKV_DIM:2 * RWKV_DIM].set(iclr_a2[l])
        wl = wl.at[64:128, 2 * RWKV_DIM:].set(gate_g2[l]).astype(BF16)
        vecs = jnp.stack([decay_w0[l], iclr_a0[l], k_k[l], k_a[l], r_k[l].reshape(-1), ln_x_w[l],
                          ln_x_b[l], vres_v0[l - 1] if l > 0 else zeros_r])
        mu = tshift_mu[l][None, :]
        if l == 0:
            y_rwkv, v_first = _rwkv(p_rwkv, mu, wl, vecs, rwkv_consts)
        else:
            vmu = jnp.pad(vres_mu[l - 1], (0, VRES_PAD - VRES_LORA))[None, :]
            v2 = jnp.pad(vres_v2[l - 1], ((0, VRES_PAD - VRES_LORA), (0, 0))).astype(BF16)
            (y_rwkv,) = _rwkv(p_rwkv, mu, wl, vecs, rwkv_consts,
                              vres=(outs[1].reshape(b, s, VRES_PAD), v_first, vmu, v2))

        res = [_attn(q, k, v, dil) for dil in DILATIONS]
        y_attn = _attn_combine([r[0] for r in res], [r[1] for r in res],
                               attn_out_norm[l].reshape(1, -1), _seg_matrix(PAIR), tm)

        flat = lambda a: a.reshape(t, -1)
        wo = w_o[l].astype(BF16)
        gain = ffn_norm[l][None, :]
        i = l // 2
        if l % 2 == 0:
            tmf = _row_tile(t, 1024)
            n_tiles = t // tmf
            x2d = _ffn(None, dense_wg[i][None].astype(BF16), dense_wu[i][None].astype(BF16),
                       dense_wd[i][None].astype(BF16), jnp.zeros((n_tiles,), jnp.int32),
                       jnp.full((1,), n_tiles, jnp.int32), tmf, 512,
                       mixers=(x2d, flat(y_rwkv), flat(y_conv), flat(y_attn), wo, gain))
        else:
            rw = jnp.pad(router[i], ((0, 0), (0, LANES - N_EXPERTS)))
            rw_hi = rw.astype(BF16)
            rw_lo = (rw - rw_hi.astype(F32)).astype(BF16)
            rt = jnp.arange(tm)
            tri = (rt[:, None] > rt[None, :]).astype(BF16)
            xn, h2, info, cnt = _outproj(x2d, flat(y_rwkv), flat(y_conv), flat(y_attn), wo, gain,
                                         tm, router=(rw_hi, rw_lo, tri))
            tme = 1024
            n_tt = t // tm
            cnt_te = cnt.reshape(n_tt, 8, LANES)[:, 0, :N_EXPERTS].astype(jnp.int32)
            seg = (cnt_te + SEG_ALIGN - 1) // SEG_ALIGN * SEG_ALIGN
            run = jnp.cumsum(seg, axis=0) - seg
            padded = (jnp.sum(seg, axis=0) + tme - 1) // tme * tme
            ends = jnp.cumsum(padded)
            base = (ends - padded)[None, :] + run
            loff = jnp.cumsum(seg, axis=1) - seg
            n_rows = pl.cdiv(2 * t + n_tt * N_EXPERTS * (SEG_ALIGN - 1) + N_EXPERTS * tme, tme) * tme
            n_tiles = n_rows // tme
            tile_start = jnp.arange(n_tiles, dtype=jnp.int32) * tme
            tile_expert = jnp.minimum(jnp.sum(ends[None, :] <= tile_start[:, None], axis=1),
                                      N_EXPERTS - 1).astype(jnp.int32)
            n_used = (ends[-1] // tme).astype(jnp.int32).reshape(1)
            xs = _dispatch(h2, info, base, loff, seg, n_rows, tm)
            ys = _ffn(xs, moe_wg[i].astype(BF16), moe_wu[i].astype(BF16), moe_wd[i].astype(BF16),
                      tile_expert, n_used, tme, 512)
            x2d = _combine(xn, info, ys, base, loff, seg, tm)
    return x2d.reshape(b, s, d)
```

```python
import functools

import jax
import jax.numpy as jnp
from jax import lax
from jax.experimental import pallas as pl
from jax.experimental.pallas import tpu as pltpu

F32 = jnp.float32
BF16 = jnp.bfloat16

D_MODEL = 1024
HEAD_DIM = 64
RWKV_DIM = 384
CONV_DIM = 256
ATTN_DIM = 384
LORA_W = 128
RWKV_IN = 3 * RWKV_DIM + LORA_W
CONV_IN = 3 * CONV_DIM
ATTN_IN = 3 * ATTN_DIM
VRES_LORA = 16
VRES_PAD = 128
CONV_WIDTH = 3
DILATIONS = (1, 4, 16)
ATTN_BLOCK = 128
DIL_SPAN = ATTN_BLOCK * 16
ROPE_THETA = 10000.0
D_FF = 3584
N_EXPERTS = 8
NORM_EPS = 1e-6
GN_EPS = 64e-5

LANES = 128
RWKV_CHUNK = 64
PAIR = 2 * HEAD_DIM
VMEM_LIMIT = 56 * 1024 * 1024
ROW_TILE = 512
FFN_ROW_TILE = 1024
FFN_COL_TILE = 512


def _params(*sem):
    return pltpu.CompilerParams(dimension_semantics=sem, vmem_limit_bytes=VMEM_LIMIT)


def _bdot(a, b):
    return jnp.dot(a.astype(BF16), b.astype(BF16), preferred_element_type=F32)


def _bdot_nt(a, b):
    return lax.dot_general(a.astype(BF16), b.astype(BF16), (((1,), (1,)), ((), ())),
                           preferred_element_type=F32)


def _shift_rows(x, carry_row, n=1):
    row = lax.broadcasted_iota(jnp.int32, (x.shape[0], 1), 0)
    out = pltpu.roll(x, n, 0)
    for i in range(n):
        out = jnp.where(row == i, carry_row[i:i + 1, :], out)
    return out


def _inproj_body(x_ref, g_ref, w_ref, cw_ref, cgain_ref, cseg_ref, cos_ref, sin_ref, qg_ref, kg_ref,
                 aseg_ref, *rest, has_vres, tiles_per_seq):
    if has_vres:
        prw_ref, pv_ref, yc_ref, q_ref, k_ref, v_ref, carry_ref = rest
    else:
        prw_ref, yc_ref, q_ref, k_ref, v_ref, carry_ref = rest
    o_conv, o_attn, o_end = RWKV_IN, RWKV_IN + CONV_IN, RWKV_IN + CONV_IN + ATTN_IN
    first_tile = lax.rem(pl.program_id(0), tiles_per_seq) == 0
    n_split = 2
    rows_h = x_ref.shape[0] // n_split
    hs = []
    for c in range(n_split):
        x = x_ref[pl.ds(c * rows_h, rows_h), :]
        hs.append((x * lax.rsqrt(jnp.mean(x * x, axis=-1, keepdims=True) + NORM_EPS)
                   * g_ref[...]).astype(BF16))
    ps = [jnp.dot(h, w_ref[...], preferred_element_type=F32) for h in hs]
    for c, p in enumerate(ps):
        rows = pl.ds(c * rows_h, rows_h)
        prw_ref[rows, :] = p[:, :o_conv]
        if has_vres:
            pv_ref[rows, :] = p[:, o_end:]
        _conv_tile(p[:, o_conv:o_attn], first_tile if c == 0 else False, cw_ref, cgain_ref, cseg_ref,
                   yc_ref, carry_ref, rows)
        _attn_prep_tile(p[:, o_attn:o_end], cos_ref, sin_ref, qg_ref, kg_ref, aseg_ref,
                        q_ref, k_ref, v_ref, rows)


def _inproj(x, gain, w, conv, attn, has_vres, tm):
    b, s, d = x.shape
    t = b * s
    n = w.shape[1]
    tps = s // tm
    conv_w, conv_gain, conv_seg = conv
    cos, sin, qg, kg, attn_seg = attn
    full = lambda arr: pl.BlockSpec(arr.shape, lambda i: (0, 0))
    row = lambda wd: pl.BlockSpec((tm, wd), lambda i: (i, 0))
    tab = pl.BlockSpec((tm, ATTN_DIM), lambda i: (lax.rem(i, tps), 0))
    qkv = pl.BlockSpec((None, ATTN_DIM // PAIR, tm, PAIR), lambda i: (i // tps, 0, lax.rem(i, tps), 0))
    qkv_shape = jax.ShapeDtypeStruct((b, ATTN_DIM // PAIR, s, PAIR), F32)
    out_specs = [row(RWKV_IN)] + ([row(VRES_PAD)] if has_vres else []) + [row(CONV_DIM)] + [qkv] * 3
    out_shape = ([jax.ShapeDtypeStruct((t, RWKV_IN), F32)]
                 + ([jax.ShapeDtypeStruct((t, VRES_PAD), F32)] if has_vres else [])
                 + [jax.ShapeDtypeStruct((t, CONV_DIM), BF16)] + [qkv_shape] * 3)
    return pl.pallas_call(
        functools.partial(_inproj_body, has_vres=has_vres, tiles_per_seq=tps),
        grid=(t // tm,),
        in_specs=[row(d), full(gain), full(w), full(conv_w), full(conv_gain), full(conv_seg),
                  tab, tab, full(qg), full(kg), full(attn_seg)],
        out_specs=out_specs, out_shape=out_shape,
        scratch_shapes=[pltpu.VMEM((CONV_WIDTH - 1, CONV_DIM), F32)],
        compiler_params=_params("arbitrary"),
        name="inproj",
    )(x.reshape(t, d), gain, w, conv_w, conv_gain, conv_seg, cos, sin, qg, kg, attn_seg)


def _rwkv_body(*refs, has_vres):
    if has_vres:
        (p_ref, pv_ref, vf_ref, mu_ref, vmu_ref, wl_ref, v2_ref, vec_ref, seg_ref, tri_ref,
         cm_ref, y_ref, prev_ref, pprev_ref, h_ref) = refs
    else:
        (p_ref, mu_ref, wl_ref, vec_ref, seg_ref, tri_ref,
         cm_ref, y_ref, vout_ref, prev_ref, h_ref) = refs
    @pl.when(pl.program_id(0) == 0)
    def _():
        prev_ref[...] = jnp.zeros_like(prev_ref)
        h_ref[...] = jnp.zeros_like(h_ref)
        if has_vres:
            pprev_ref[...] = jnp.zeros_like(pprev_ref)

    nb, nt = p_ref.shape[0], p_ref.shape[1]

    def token_shift(src_ref, carry_ref, mix):
        out = []
        for bi in range(nb):
            cur = src_ref[bi]
            prev = _shift_rows(cur, carry_ref[bi:bi + 1, :])
            carry_ref[bi:bi + 1, :] = cur[nt - 1:nt, :]
            out.append(cur + (prev - cur) * mix)
        return jnp.concatenate(out, axis=0)

    x = token_shift(p_ref, prev_ref, mu_ref[...])
    r = x[:, 0:RWKV_DIM]
    k = x[:, RWKV_DIM:2 * RWKV_DIM]
    v = x[:, 2 * RWKV_DIM:3 * RWKV_DIM]
    z = x[:, 3 * RWKV_DIM:]
    lane = lax.broadcasted_iota(jnp.int32, (1, LANES), 1)
    zt = jnp.where(lane < 32, jnp.tanh(z), jnp.where(lane < 64, z, jax.nn.sigmoid(z)))
    lo = _bdot(zt, wl_ref[...])
    w0, a0, k_k, k_a, r_k, ln_w, ln_b, v0 = (vec_ref[i:i + 1, :] for i in range(8))
    seg = seg_ref[...]

    zz = -(w0 + lo[:, 0:RWKV_DIM])
    softplus = jnp.maximum(zz, 0.0) + jnp.log(1.0 + jnp.exp(-jnp.abs(zz)))
    lw = -jnp.exp(-softplus - 0.5)
    a = jax.nn.sigmoid(a0 + lo[:, RWKV_DIM:2 * RWKV_DIM])
    g = lo[:, 2 * RWKV_DIM:]
    if has_vres:
        xv = token_shift(pv_ref, pprev_ref, vmu_ref[...])
        v_first = vf_ref[...].reshape(nb * nt, RWKV_DIM)
        v = v + (v_first - v) * jax.nn.sigmoid(v0 + _bdot(xv, v2_ref[...]))
    else:
        vout_ref[...] = v.reshape(nb, nt, RWKV_DIM)
    kk = k * k_k
    kk = kk * jnp.minimum(lax.rsqrt(_bdot(kk * kk, seg)), 1e12)
    kmod = k * (1.0 + (a - 1.0) * k_a)
    bonus = _bdot(r * kmod * r_k, seg) * v

    tri = tri_ref[...]
    lw_hi = lw.astype(BF16)
    lw_lo = (lw - lw_hi.astype(F32)).astype(BF16)
    cum = (jnp.dot(tri, lw_hi, preferred_element_type=F32)
           + jnp.dot(tri, lw_lo, preferred_element_type=F32))
    pinv = jnp.exp(-cum)
    r_t = r * jnp.exp(cum)
    a_t = -kk * jnp.exp(cum - lw)
    b_t = kk * a * pinv
    k_t = kmod * pinv
    p_ends = [jnp.exp(cum[bi * nt + nt - 1:(bi + 1) * nt, :]) for bi in range(nb)]
    p_end_rows = jnp.concatenate([jnp.broadcast_to(pe, (nt, RWKV_DIM)) for pe in p_ends], axis=0)
    bh_t = b_t * p_end_rows
    kh_t = k_t * p_end_rows

    strict, incl, blk8, eye = (cm_ref[i] for i in range(4))
    m0 = (lane < HEAD_DIM).astype(F32)
    m1 = 1.0 - m0

    def stack(t, rows, sl):
        ts = t[rows, sl]
        return jnp.concatenate([ts * m0, ts * m1], axis=0)

    npair = RWKV_DIM // PAIR
    chains = [(bi, j) for bi in range(nb) for j in range(npair)]
    each = lambda fn, *cols: [fn(*args) for args in zip(*cols)]
    cat0 = lambda *ts: jnp.concatenate(ts, axis=0)
    cat1 = lambda *ts: jnp.concatenate(ts, axis=1)

    def stacks(t):
        return [stack(t, slice(bi * nt, (bi + 1) * nt), slice(j * PAIR, (j + 1) * PAIR))
                for bi, j in chains]

    a_st, r_st, b_st, k_st, v_st, bh_st, kh_st = (stacks(t) for t in (a_t, r_t, b_t, k_t, v, bh_t, kh_t))
    gram = each(lambda a_, r_, b_, k_: _bdot_nt(cat0(a_, r_), cat0(b_, k_)), a_st, r_st, b_st, k_st)
    n2 = 2 * nt
    a_ab = [gm[:n2, :n2] * strict for gm in gram]
    a_ak = [gm[:n2, n2:] * strict for gm in gram]
    a_rb = [gm[n2:, :n2] * incl for gm in gram]
    a_rk = [gm[n2:, n2:] * incl for gm in gram]
    dg = [m * blk8 for m in a_ab]
    off = each(lambda m, d_: m - d_, a_ab, dg)
    d2 = each(_bdot, dg, dg)
    tdiag = [eye + d_ for d_ in dg]
    tdiag = each(lambda t_, d_: t_ + _bdot(t_, d_), tdiag, d2)
    d4 = each(_bdot, d2, d2)
    tdiag = each(lambda t_, d_: t_ + _bdot(t_, d_), tdiag, d4)
    e1 = each(_bdot, tdiag, off)
    e2 = each(_bdot, e1, e1)
    yb = [eye + e_ for e_ in e1]
    yb = each(lambda y_, e_: y_ + _bdot(y_, e_), yb, e2)
    e4 = each(_bdot, e2, e2)
    yb = each(lambda y_, e_: y_ + _bdot(y_, e_), yb, e4)
    tinv = each(_bdot, yb, tdiag)

    hs = [h_ref[bi, j] for bi, j in chains]
    xs = each(lambda a_, ak_, h_, v_: _bdot(cat1(a_, ak_), cat0(h_, v_)), a_st, a_ak, hs, v_st)
    us = each(_bdot, tinv, xs)
    y_st = each(lambda r_, rb_, rk_, h_, u_, v_: _bdot(cat1(r_, rb_, rk_), cat0(h_, u_, v_)),
                r_st, a_rb, a_rk, hs, us, v_st)
    pe_col = [jnp.sum(eye * p_ends[bi][:, j * PAIR:(j + 1) * PAIR], axis=1, keepdims=True)
              for bi, j in chains]
    h_new = each(lambda pc_, h_, bh_, kh_, u_, v_: pc_ * h_ + _bdot(cat0(bh_, kh_).T, cat0(u_, v_)),
                 pe_col, hs, bh_st, kh_st, us, v_st)
    for (bi, j), hn in zip(chains, h_new):
        h_ref[bi, j] = hn
    y_pair = [ys_[:nt] + ys_[nt:] for ys_ in y_st]
    y = cat0(*[cat1(*y_pair[bi * npair:(bi + 1) * npair]) for bi in range(nb)])
    inv_n = 1.0 / HEAD_DIM
    mean = _bdot(y, seg) * inv_n
    dy = y - mean
    var = _bdot(dy * dy, seg) * inv_n
    yn = dy * lax.rsqrt(var + GN_EPS) * ln_w + ln_b
    y_ref[...] = ((yn + bonus) * g).reshape(nb, nt, RWKV_DIM).astype(y_ref.dtype)


def _rwkv(p_rwkv, mu, wl, vecs, consts, vres=None):
    b, s, _ = p_rwkv.shape
    c = RWKV_CHUNK
    seg, tri, cm = consts
    tile = lambda wd: pl.BlockSpec((b, c, wd), lambda j: (0, j, 0))
    full = lambda arr: pl.BlockSpec(arr.shape, lambda j: (0,) * arr.ndim)
    y_shape = jax.ShapeDtypeStruct((b, s, RWKV_DIM), BF16)
    scratch = [pltpu.VMEM((b, RWKV_IN), F32)]
    if vres is None:
        ins = [p_rwkv, mu, wl, vecs, seg, tri, cm]
        in_specs = [tile(RWKV_IN)] + [full(t) for t in ins[1:]]
        out_shape = [y_shape, jax.ShapeDtypeStruct((b, s, RWKV_DIM), F32)]
        out_specs = [tile(RWKV_DIM), tile(RWKV_DIM)]
    else:
        p_vres, v_first, vmu, v2 = vres
        ins = [p_rwkv, p_vres, v_first, mu, vmu, wl, v2, vecs, seg, tri, cm]
        in_specs = [tile(RWKV_IN), tile(VRES_PAD), tile(RWKV_DIM)] + [full(t) for t in ins[3:]]
        out_shape = [y_shape]
        out_specs = [tile(RWKV_DIM)]
        scratch.append(pltpu.VMEM((b, VRES_PAD), F32))
    scratch.append(pltpu.VMEM((b, RWKV_DIM // PAIR, PAIR, PAIR), F32))
    return pl.pallas_call(
        functools.partial(_rwkv_body, has_vres=vres is not None),
        grid=(s // c,),
        in_specs=in_specs, out_specs=out_specs, out_shape=out_shape,
        scratch_shapes=scratch,
        compiler_params=_params("arbitrary"),
        name="rwkv7",
    )(*ins)


def _conv_tile(p, first_tile, w_ref, gain_ref, seg_ref, o_ref, carry_ref, rows):
    if first_tile is not False:
        @pl.when(first_tile)
        def _():
            carry_ref[...] = jnp.zeros_like(carry_ref)

    nt = p.shape[0]
    u = p[:, CONV_DIM:2 * CONV_DIM] * p[:, 2 * CONV_DIM:]
    carry = carry_ref[...]
    u1 = _shift_rows(u, carry[1:2, :], 1)
    u2 = _shift_rows(u, carry, 2)
    carry_ref[...] = u[nt - 2:nt, :]
    w = w_ref[...]
    y = p[:, 0:CONV_DIM] * (w[0:1, :] * u2 + w[1:2, :] * u1 + w[2:3, :] * u)
    ms = _bdot(y * y, seg_ref[...]) * (1.0 / HEAD_DIM)
    o_ref[rows, :] = (y * lax.rsqrt(ms + NORM_EPS) * gain_ref[...]).astype(o_ref.dtype)


def _attn_prep_tile(p, cos_ref, sin_ref, qg_ref, kg_ref, seg_ref, q_ref, k_ref, v_ref, rows):
    seg = seg_ref[...]
    cos, sin = cos_ref[rows, :], sin_ref[rows, :]
    lane = lax.broadcasted_iota(jnp.int32, (1, LANES), 1)
    first = (lane & (HEAD_DIM - 1)) < HEAD_DIM // 2
    for part, g_ref, o_ref, scale in ((0, qg_ref, q_ref, HEAD_DIM ** -0.5), (1, kg_ref, k_ref, 1.0)):
        x = p[:, part * ATTN_DIM:(part + 1) * ATTN_DIM]
        ms = _bdot(x * x, seg) * (1.0 / HEAD_DIM)
        xn = x * lax.rsqrt(ms + NORM_EPS) * g_ref[...]
        rot = []
        for j in range(ATTN_DIM // LANES):
            xs = xn[:, j * LANES:(j + 1) * LANES]
            rot.append(jnp.where(first, pltpu.roll(xs, LANES - HEAD_DIM // 2, 1),
                                 pltpu.roll(xs, HEAD_DIM // 2, 1)))
        out = (xn * cos + jnp.concatenate(rot, axis=1) * sin) * scale
        for j in range(ATTN_DIM // PAIR):
            o_ref[j, rows, :] = out[:, j * PAIR:(j + 1) * PAIR]
    for j in range(ATTN_DIM // PAIR):
        v_ref[j, rows, :] = p[:, 2 * ATTN_DIM + j * PAIR:2 * ATTN_DIM + (j + 1) * PAIR]


def _pair_major_spec(rows, row_index):
    return pl.BlockSpec((None, ATTN_DIM // PAIR, rows, PAIR), lambda i, j: (i, 0, row_index(j), 0))


def _attn_body(q_ref, k_ref, v_ref, kprev_ref, vprev_ref, o_ref, l_ref, *, dil):
    span = pl.program_id(1)
    blk = ATTN_BLOCK
    npair = ATTN_DIM // PAIR
    qi = lax.broadcasted_iota(jnp.int32, (blk, 2 * blk), 0)
    kj = lax.broadcasted_iota(jnp.int32, (blk, 2 * blk), 1)
    dist = qi + blk - kj
    band = (dist >= 0) & (dist <= ATTN_BLOCK)
    lane = lax.broadcasted_iota(jnp.int32, (1, LANES), 1)
    head0 = lane < HEAD_DIM
    m0 = head0.astype(F32)
    heads = [(j, hm) for j in range(npair) for hm in (m0, 1.0 - m0)]

    def block_attn(q, kcat, vcat, block_index):
        mask = band & (kj + block_index * (2 * blk) >= blk)
        s_all = [_bdot_nt(q[j] * hm, kcat[j]) for j, hm in heads]
        s_all = [jnp.where(mask, s, -jnp.inf) for s in s_all]
        m_all = [jnp.max(s, axis=-1, keepdims=True) for s in s_all]
        pe_all = [jnp.exp(s - m) for s, m in zip(s_all, m_all)]
        den_all = [jnp.sum(pe, axis=-1, keepdims=True) for pe in pe_all]
        o_all = [_bdot(pe, vcat[j]) * (1.0 / den) for pe, den, (j, _) in zip(pe_all, den_all, heads)]
        lse_all = [m + jnp.log(den) for m, den in zip(m_all, den_all)]
        o_pair = [jnp.where(head0, o_all[2 * j], o_all[2 * j + 1]) for j in range(npair)]
        l_pair = [jnp.where(head0, lse_all[2 * j], lse_all[2 * j + 1]) for j in range(npair)]
        return o_pair, l_pair

    if dil == 1:
        nq = q_ref.shape[1] // blk
        kext = [jnp.concatenate([kprev_ref[j], k_ref[j]], axis=0) for j in range(npair)]
        vext = [jnp.concatenate([vprev_ref[j], v_ref[j]], axis=0) for j in range(npair)]
        for qb in range(nq):
            rows = slice(qb * blk, (qb + 1) * blk)
            keys = slice(qb * blk, (qb + 2) * blk)
            o_pair, l_pair = block_attn([q_ref[j, rows, :] for j in range(npair)],
                                        [kx[keys] for kx in kext], [vx[keys] for vx in vext],
                                        span * nq + qb)
            for j in range(npair):
                o_ref[j, rows, :] = o_pair[j]
                l_ref[j, rows, :] = l_pair[j]
    else:
        def one_class(r, carry):
            rows = pl.ds(r, blk, stride=dil)
            cat = lambda prev_ref, ref: [jnp.concatenate([prev_ref[j, rows, :], ref[j, rows, :]], axis=0)
                                         for j in range(npair)]
            o_pair, l_pair = block_attn([q_ref[j, rows, :] for j in range(npair)],
                                        cat(kprev_ref, k_ref), cat(vprev_ref, v_ref), span)
            for j in range(npair):
                o_ref[j, rows, :] = o_pair[j]
                l_ref[j, rows, :] = l_pair[j]
            return carry
        lax.fori_loop(0, dil, one_class, 0)


def _attn(q, k, v, dil):
    b, _, s, _ = q.shape
    if dil == 1:
        nq = 4
        span = ATTN_BLOCK * nq
        prev = _pair_major_spec(ATTN_BLOCK, lambda m: jnp.maximum(m * nq - 1, 0))
    else:
        span = ATTN_BLOCK * dil
        prev = _pair_major_spec(span, lambda m: jnp.maximum(m - 1, 0))
    spec = _pair_major_spec(span, lambda m: m)
    shp = jax.ShapeDtypeStruct(q.shape, F32)
    return pl.pallas_call(
        functools.partial(_attn_body, dil=dil),
        grid=(b, s // span),
        in_specs=[spec, spec, spec, prev, prev],
        out_specs=[spec, spec],
        out_shape=[shp, shp],
        compiler_params=_params("arbitrary", "arbitrary"),
        name=f"attn_d{dil}",
    )(q, k, v, k, v)


def _attn_combine_body(*refs):
    n = len(DILATIONS)
    o_refs, l_refs = refs[:n], refs[n:2 * n]
    gain_ref, seg_ref, y_ref = refs[2 * n:]
    for j in range(ATTN_DIM // PAIR):
        sl = slice(j * PAIR, (j + 1) * PAIR)
        ls = [r[j] for r in l_refs]
        m = functools.reduce(jnp.maximum, ls)
        es = [jnp.exp(l - m) for l in ls]
        den = functools.reduce(jnp.add, es)
        o = functools.reduce(jnp.add, [e * r[j] for e, r in zip(es, o_refs)]) / den
        ms = _bdot(o * o, seg_ref[...]) * (1.0 / HEAD_DIM)
        y_ref[:, sl] = (o * lax.rsqrt(ms + NORM_EPS) * gain_ref[:, sl]).astype(y_ref.dtype)


def _attn_combine(os_, ls_, gain, seg, tm):
    b, _, s, _ = os_[0].shape
    tile = _pair_major_spec(tm, lambda j: j)
    full = lambda arr: pl.BlockSpec(arr.shape, lambda i, j: (0, 0))
    return pl.pallas_call(
        _attn_combine_body,
        grid=(b, s // tm),
        in_specs=[tile] * (2 * len(os_)) + [full(gain), full(seg)],
        out_specs=pl.BlockSpec((None, tm, ATTN_DIM), lambda i, j: (i, j, 0)),
        out_shape=jax.ShapeDtypeStruct((b, s, ATTN_DIM), BF16),
        compiler_params=_params("arbitrary", "arbitrary"),
        name="attn_combine",
    )(*os_, *ls_, gain, seg)


INFO_E0, INFO_E1, INFO_R0, INFO_R1, INFO_G0, INFO_G1 = range(6)


def _mix_residual(x_ref, yr_ref, yc_ref, ya_ref, wo_ref, g_ref):
    o1, o2 = RWKV_DIM, RWKV_DIM + CONV_DIM
    acc = jnp.dot(yr_ref[...], wo_ref[0:o1, :], preferred_element_type=F32)
    acc += jnp.dot(yc_ref[...], wo_ref[o1:o2, :], preferred_element_type=F32)
    acc += jnp.dot(ya_ref[...], wo_ref[o2:, :], preferred_element_type=F32)
    xn = x_ref[...] + acc
    h = xn * lax.rsqrt(jnp.mean(xn * xn, axis=-1, keepdims=True) + NORM_EPS) * g_ref[...]
    return xn, h


def _outproj_body(x_ref, yr_ref, yc_ref, ya_ref, wo_ref, g_ref, rwh_ref, rwl_ref, tri_ref,
                  xn_ref, h_ref, info_ref, cnt_ref):
    xn, h = _mix_residual(x_ref, yr_ref, yc_ref, ya_ref, wo_ref, g_ref)
    xn_ref[...] = xn
    h_ref[...] = h.astype(h_ref.dtype)

    h_hi = h.astype(BF16)
    h_lo = (h - h_hi.astype(F32)).astype(BF16)
    rwh = rwh_ref[...]
    logits = (jnp.dot(h_hi, rwh, preferred_element_type=F32)
              + jnp.dot(h_lo, rwh, preferred_element_type=F32)
              + jnp.dot(h_hi, rwl_ref[...], preferred_element_type=F32))
    lane = lax.broadcasted_iota(jnp.int32, (1, LANES), 1)
    logits = jnp.where(lane < N_EXPERTS, logits, -jnp.inf)

    def top(lg):
        m = jnp.max(lg, axis=-1, keepdims=True)
        idx = jnp.min(jnp.where(lg == m, lane, LANES), axis=-1, keepdims=True)
        return m, idx, lane == idx

    m1, i1, oh1 = top(logits)
    m2, i2, oh2 = top(jnp.where(oh1, -jnp.inf, logits))
    zexp = jnp.exp(m2 - m1)
    g0 = 1.0 / (1.0 + zexp)
    g1 = zexp * g0
    oh = oh1.astype(F32) + oh2.astype(F32)
    tot = jnp.dot(tri_ref[...], oh.astype(BF16), preferred_element_type=F32)
    r0 = jnp.sum(jnp.where(oh1, tot, 0.0), axis=-1, keepdims=True)
    r1 = jnp.sum(jnp.where(oh2, tot, 0.0), axis=-1, keepdims=True)
    cnt_ref[...] = jnp.broadcast_to(jnp.sum(oh, axis=0, keepdims=True), cnt_ref.shape)
    info = jnp.zeros(logits.shape, F32)
    for ln, val in ((INFO_E0, i1.astype(F32)), (INFO_E1, i2.astype(F32)), (INFO_R0, r0),
                    (INFO_R1, r1), (INFO_G0, g0), (INFO_G1, g1)):
        info = jnp.where(lane == ln, val, info)
    info_ref[...] = info


def _outproj(x2d, yr, yc, ya, wo, gain, router, tm):
    t, d = x2d.shape
    row = lambda wd: pl.BlockSpec((tm, wd), lambda i: (i, 0))
    full = lambda arr: pl.BlockSpec(arr.shape, lambda i: (0, 0))
    return pl.pallas_call(
        _outproj_body,
        grid=(t // tm,),
        in_specs=[row(d), row(RWKV_DIM), row(CONV_DIM), row(ATTN_DIM), full(wo), full(gain)]
                 + [full(a) for a in router],
        out_specs=[row(d), row(d), row(LANES), pl.BlockSpec((8, LANES), lambda i: (i, 0))],
        out_shape=[jax.ShapeDtypeStruct((t, d), F32), jax.ShapeDtypeStruct((t, d), BF16),
                   jax.ShapeDtypeStruct((t, LANES), F32),
                   jax.ShapeDtypeStruct((t // tm * 8, LANES), F32)],
        compiler_params=_params("arbitrary"),
        name="outproj",
    )(x2d, yr, yc, ya, wo, gain, *router)


def _ffn_body(te_ref, nu_ref, *refs, from_mixers):
    if from_mixers:
        (xres_ref, yr_ref, yc_ref, ya_ref, wo_ref, g_ref, wg_ref, wu_ref, wd_ref, o_ref,
         acc_ref, xn_ref, h_ref) = refs
    else:
        h_ref, wg_ref, wu_ref, wd_ref, o_ref, acc_ref = refs
    i, f = pl.program_id(0), pl.program_id(1)

    @pl.when(f == 0)
    def _():
        acc_ref[...] = jnp.zeros_like(acc_ref)
        if from_mixers:
            xn, h = _mix_residual(xres_ref, yr_ref, yc_ref, ya_ref, wo_ref, g_ref)
            xn_ref[...] = xn
            h_ref[...] = h

    @pl.when(i < nu_ref[0])
    def _():
        x = h_ref[...].astype(BF16)
        hg = jnp.dot(x, wg_ref[...], preferred_element_type=F32)
        hu = jnp.dot(x, wu_ref[...], preferred_element_type=F32)
        act = hg * jax.nn.sigmoid(hg) * hu
        acc_ref[...] += jnp.dot(act.astype(BF16), wd_ref[...], preferred_element_type=F32)

    @pl.when(f == pl.num_programs(1) - 1)
    def _():
        out = acc_ref[...]
        if from_mixers:
            out = out + xn_ref[...]
        o_ref[...] = out.astype(o_ref.dtype)


def _ffn(x, wg, wu, wd, tile_expert, n_used, tm, tf, mixers=None):
    ff = wg.shape[-1]
    row = lambda wd_: pl.BlockSpec((tm, wd_), lambda i, f, te, nu: (i, 0))
    full = lambda arr: pl.BlockSpec(arr.shape, lambda i, f, te, nu: (0, 0))
    weights = [pl.BlockSpec((None, wg.shape[1], tf), lambda i, f, te, nu: (te[i], 0, f)),
               pl.BlockSpec((None, wg.shape[1], tf), lambda i, f, te, nu: (te[i], 0, f)),
               pl.BlockSpec((None, tf, wg.shape[1]), lambda i, f, te, nu: (te[i], f, 0))]
    if mixers is None:
        n, d = x.shape
        ins = [x, wg, wu, wd]
        in_specs = [row(d)] + weights
        scratch = [pltpu.VMEM((tm, d), F32)]
    else:
        x_res, yr, yc, ya, wo, gain = mixers
        n, d = x_res.shape
        ins = [x_res, yr, yc, ya, wo, gain, wg, wu, wd]
        in_specs = [row(d), row(RWKV_DIM), row(CONV_DIM), row(ATTN_DIM), full(wo), full(gain)] + weights
        scratch = [pltpu.VMEM((tm, d), F32) for _ in range(3)]
    return pl.pallas_call(
        functools.partial(_ffn_body, from_mixers=mixers is not None),
        grid_spec=pltpu.PrefetchScalarGridSpec(
            num_scalar_prefetch=2,
            grid=(n // tm, ff // tf),
            in_specs=in_specs,
            out_specs=row(d),
            scratch_shapes=scratch),
        out_shape=jax.ShapeDtypeStruct((n, d), F32),
        compiler_params=_params("arbitrary", "arbitrary"),
        name="swiglu",
    )(tile_expert, n_used, *ins)


SEG_ALIGN = 16


def _stage_positions(info, loffv_ref):
    lane_f = lax.broadcasted_iota(jnp.int32, (1, LANES), 1).astype(F32)
    pos = []
    for e_ln, r_ln in ((INFO_E0, INFO_R0), (INFO_E1, INFO_R1)):
        onehot = lane_f == info[:, e_ln:e_ln + 1]
        seg_start = jnp.sum(jnp.where(onehot, loffv_ref[0:1, :], 0.0), axis=-1, keepdims=True)
        pos.append(seg_start + info[:, r_ln:r_ln + 1])
    return pos


def _segment_copies(tile, tm, base_ref, loff_ref, seg_ref, make_copy, start):
    for e in range(N_EXPERTS):
        seg = seg_ref[tile * N_EXPERTS + e]
        size = tm
        while size >= SEG_ALIGN:
            off = seg & ~(2 * size - 1)

            @pl.when((seg & size) != 0)
            def _():
                cp = make_copy(pl.multiple_of(base_ref[tile * N_EXPERTS + e] + off, SEG_ALIGN),
                               pl.multiple_of(loff_ref[tile * N_EXPERTS + e] + off, SEG_ALIGN), size)
                if start:
                    cp.start()
                else:
                    cp.wait()
            size //= 2


def _dispatch_body(base_ref, loff_ref, seg_ref, h_ref, info_ref, loffv_ref, xs_in, xs_out,
                   stage_ref, sem):
    del xs_in
    i = pl.program_id(0)
    tm = h_ref.shape[0]
    n_stage = stage_ref.shape[0]
    lane = lax.broadcasted_iota(jnp.int32, (1, LANES), 1)
    pos = _stage_positions(info_ref[...], loffv_ref)
    pos_t = jnp.transpose(jnp.where(lane == 0, pos[0], jnp.where(lane == 1, pos[1], -1.0)))
    rows = lax.broadcasted_iota(jnp.int32, (n_stage, 1), 0).astype(F32)
    perm = ((rows == pos_t[0:1, :]) | (rows == pos_t[1:2, :])).astype(BF16)
    stage_ref[...] = jnp.dot(perm, h_ref[...].astype(BF16),
                             preferred_element_type=F32).astype(stage_ref.dtype)
    make_copy = lambda row, srow, size: pltpu.make_async_copy(
        stage_ref.at[pl.ds(srow, size)], xs_out.at[pl.ds(row, size)], sem)
    for start in (True, False):
        _segment_copies(i, tm, base_ref, loff_ref, seg_ref, make_copy, start)


def _moe_specs(tm):
    row = lambda wd: pl.BlockSpec((tm, wd), lambda i, *_: (i, 0))
    vec = pl.BlockSpec((8, LANES), lambda i, *_: (i, 0))
    lanes8 = lambda a: jnp.repeat(jnp.pad(a.astype(F32), ((0, 0), (0, LANES - N_EXPERTS))), 8, axis=0)
    flat = lambda a: a.reshape(-1).astype(jnp.int32)
    return row, vec, lanes8, flat


def _dispatch(h2, info, base, loff, seg, n_rows, tm):
    t, d = h2.shape
    n_stage = 2 * tm + N_EXPERTS * SEG_ALIGN
    row, vec, lanes8, flat = _moe_specs(tm)
    any_spec = pl.BlockSpec(memory_space=pl.ANY)
    return pl.pallas_call(
        _dispatch_body,
        grid_spec=pltpu.PrefetchScalarGridSpec(
            num_scalar_prefetch=3,
            grid=(t // tm,),
            in_specs=[row(d), row(LANES), vec, any_spec],
            out_specs=any_spec,
            scratch_shapes=[pltpu.VMEM((n_stage, d), BF16), pltpu.SemaphoreType.DMA]),
        out_shape=jax.ShapeDtypeStruct((n_rows, d), BF16),
        input_output_aliases={6: 0},
        compiler_params=_params("arbitrary"),
        name="moe_dispatch",
    )(flat(base), flat(loff), flat(seg), h2, info, lanes8(loff), jnp.zeros((n_rows, d), BF16))


def _combine_body(base_ref, loff_ref, seg_ref, x_ref, info_ref, loffv_ref, ys_hbm, o_ref,
                  stage_ref, sem):
    i = pl.program_id(0)
    tm = x_ref.shape[0]
    n_stage = stage_ref.shape[1]
    slot = lax.rem(i, 2)

    def copies(tile, sl, start):
        make_copy = lambda row, srow, size: pltpu.make_async_copy(
            ys_hbm.at[pl.ds(row, size)], stage_ref.at[sl, pl.ds(srow, size)], sem.at[sl])
        _segment_copies(tile, tm, base_ref, loff_ref, seg_ref, make_copy, start)

    @pl.when(i == 0)
    def _():
        stage_ref[...] = jnp.zeros_like(stage_ref)
        copies(0, 0, True)

    @pl.when(i + 1 < pl.num_programs(0))
    def _():
        copies(i + 1, 1 - slot, True)

    info = info_ref[...]
    pos = _stage_positions(info, loffv_ref)
    col = lax.broadcasted_iota(jnp.int32, (1, n_stage), 1).astype(F32)
    sel = (jnp.where(col == pos[0], info[:, INFO_G0:INFO_G0 + 1], 0.0)
           + jnp.where(col == pos[1], info[:, INFO_G1:INFO_G1 + 1], 0.0))
    sel_hi = sel.astype(BF16)
    sel_lo = (sel - sel_hi.astype(F32)).astype(BF16)
    copies(i, slot, False)
    ys_tile = stage_ref[slot].astype(BF16)
    o_ref[...] = (x_ref[...] + jnp.dot(sel_hi, ys_tile, preferred_element_type=F32)
                  + jnp.dot(sel_lo, ys_tile, preferred_element_type=F32))


def _combine(xn, info, ys, base, loff, seg, tm):
    t, d = xn.shape
    n_stage = 2 * tm + N_EXPERTS * SEG_ALIGN
    row, vec, lanes8, flat = _moe_specs(tm)
    return pl.pallas_call(
        _combine_body,
        grid_spec=pltpu.PrefetchScalarGridSpec(
            num_scalar_prefetch=3,
            grid=(t // tm,),
            in_specs=[row(d), row(LANES), vec, pl.BlockSpec(memory_space=pl.ANY)],
            out_specs=row(d),
            scratch_shapes=[pltpu.VMEM((2, n_stage, d), F32), pltpu.SemaphoreType.DMA((2,))]),
        out_shape=jax.ShapeDtypeStruct((t, d), F32),
        compiler_params=_params("arbitrary"),
        name="moe_combine",
    )(flat(base), flat(loff), flat(seg), xn, info, lanes8(loff), ys)


def _seg_matrix(n):
    i = jnp.arange(n) // HEAD_DIM
    return (i[:, None] == i[None, :]).astype(BF16)


def _rwkv_consts(nb):
    c = RWKV_CHUNK
    t = jnp.arange(nb * c)
    tri = ((t[:, None] >= t[None, :]) & (t[:, None] // c == t[None, :] // c)).astype(BF16)
    i = jnp.arange(2 * c)
    same = (i[:, None] // c) == (i[None, :] // c)
    ti, tj = (i % c)[:, None], (i % c)[None, :]
    strict = same & (ti > tj)
    incl = same & (ti >= tj)
    blk8 = (i[:, None] // 8) == (i[None, :] // 8)
    eye = i[:, None] == i[None, :]
    cm = jnp.stack([strict, incl, blk8, eye]).astype(F32)
    return _seg_matrix(RWKV_DIM), tri, cm


def _rope_tables(s):
    half = HEAD_DIM // 2
    inv_freq = ROPE_THETA ** (-jnp.arange(half, dtype=F32) * 2.0 / HEAD_DIM)
    ang = jnp.arange(s, dtype=F32)[:, None] * inv_freq[None, :]
    cos, sin = jnp.cos(ang), jnp.sin(ang)
    reps = ATTN_DIM // HEAD_DIM
    cos_t = jnp.tile(jnp.concatenate([cos, cos], axis=1), (1, reps))
    sin_t = jnp.tile(jnp.concatenate([-sin, sin], axis=1), (1, reps))
    return cos_t, sin_t


def _row_tile(n, want):
    return want if n % want == 0 else n


def kernel(x, mix_norm, w_in, tshift_mu, vres_w_in, vres_mu, vres_v0, vres_v2, decay_w0, decay_w2, iclr_a0, iclr_a2, gate_g2, k_k, k_a, r_k, ln_x_w, ln_x_b, conv_w, conv_out_norm, q_norm, k_norm, attn_out_norm, w_o, ffn_norm, dense_wg, dense_wu, dense_wd, router, moe_wg, moe_wu, moe_wd):
    b, s, d = x.shape
    assert d == D_MODEL and s % DIL_SPAN == 0
    t = b * s
    depth = w_in.shape[0]
    tm = _row_tile(s, ROW_TILE)
    seg384 = _seg_matrix(RWKV_DIM)
    seg256 = _seg_matrix(CONV_DIM)
    rwkv_consts = _rwkv_consts(b)
    cos_t, sin_t = _rope_tables(s)
    zeros_r = jnp.zeros((RWKV_DIM,), F32)

    x2d = x.reshape(t, d)
    v_first = None
    for l in range(depth):
        w_comb = w_in[l]
        if l > 0:
            w_comb = jnp.concatenate(
                [w_comb, jnp.pad(vres_w_in[l - 1], ((0, 0), (0, VRES_PAD - VRES_LORA)))], axis=1)
        reps = ATTN_DIM // HEAD_DIM
        outs = _inproj(x2d.reshape(b, s, d), mix_norm[l][None, :], w_comb.astype(BF16),
                       (conv_w[l], conv_out_norm[l].reshape(1, -1), seg256),
                       (cos_t, sin_t, jnp.tile(q_norm[l], reps)[None, :],
                        jnp.tile(k_norm[l], reps)[None, :], seg384), l > 0, tm)
        p_rwkv = outs[0].reshape(b, s, RWKV_IN)
        y_conv, q, k, v = outs[-4:]

        wl = jnp.zeros((LORA_W, 3 * RWKV_DIM), F32)
        wl = wl.at[0:32, 0:RWKV_DIM].set(decay_w2[l])
        wl = wl.at[32:64, RWKV_DIM:2 * RWKV_DIM].set(iclr_a2[l])
        wl = wl.at[64:128, 2 * RWKV_DIM:].set(gate_g2[l]).astype(BF16)
        vecs = jnp.stack([decay_w0[l], iclr_a0[l], k_k[l], k_a[l], r_k[l].reshape(-1), ln_x_w[l],
                          ln_x_b[l], vres_v0[l - 1] if l > 0 else zeros_r])
        mu = tshift_mu[l][None, :]
        if l == 0:
            y_rwkv, v_first = _rwkv(p_rwkv, mu, wl, vecs, rwkv_consts)
        else:
            vmu = jnp.pad(vres_mu[l - 1], (0, VRES_PAD - VRES_LORA))[None, :]
            v2 = jnp.pad(vres_v2[l - 1], ((0, VRES_PAD - VRES_LORA), (0, 0))).astype(BF16)
            (y_rwkv,) = _rwkv(p_rwkv, mu, wl, vecs, rwkv_consts,
                              vres=(outs[1].reshape(b, s, VRES_PAD), v_first, vmu, v2))

        res = [_attn(q, k, v, dil) for dil in DILATIONS]
        y_attn = _attn_combine([r[0] for r in res], [r[1] for r in res],
                               attn_out_norm[l].reshape(1, -1), _seg_matrix(PAIR), tm)

        flat = lambda a: a.reshape(t, -1)
        wo = w_o[l].astype(BF16)
        gain = ffn_norm[l][None, :]
        i = l // 2
        if l % 2 == 0:
            tmf = _row_tile(t, FFN_ROW_TILE)
            n_tiles = t // tmf
            x2d = _ffn(None, dense_wg[i][None].astype(BF16), dense_wu[i][None].astype(BF16),
                       dense_wd[i][None].astype(BF16), jnp.zeros((n_tiles,), jnp.int32),
                       jnp.full((1,), n_tiles, jnp.int32), tmf, FFN_COL_TILE,
                       mixers=(x2d, flat(y_rwkv), flat(y_conv), flat(y_attn), wo, gain))
        else:
            rw = jnp.pad(router[i], ((0, 0), (0, LANES - N_EXPERTS)))
            rw_hi = rw.astype(BF16)
            rw_lo = (rw - rw_hi.astype(F32)).astype(BF16)
            rt = jnp.arange(tm)
            tri = (rt[:, None] > rt[None, :]).astype(BF16)
            xn, h2, info, cnt = _outproj(x2d, flat(y_rwkv), flat(y_conv), flat(y_attn), wo, gain,
                                         (rw_hi, rw_lo, tri), tm)
            tme = FFN_ROW_TILE
            n_tt = t // tm
            cnt_te = cnt.reshape(n_tt, 8, LANES)[:, 0, :N_EXPERTS].astype(jnp.int32)
            seg = (cnt_te + SEG_ALIGN - 1) // SEG_ALIGN * SEG_ALIGN
            run = jnp.cumsum(seg, axis=0) - seg
            padded = (jnp.sum(seg, axis=0) + tme - 1) // tme * tme
            ends = jnp.cumsum(padded)
            base = (ends - padded)[None, :] + run
            loff = jnp.cumsum(seg, axis=1) - seg
            n_rows = pl.cdiv(2 * t + n_tt * N_EXPERTS * (SEG_ALIGN - 1) + N_EXPERTS * tme, tme) * tme
            n_tiles = n_rows // tme
            tile_start = jnp.arange(n_tiles, dtype=jnp.int32) * tme
            tile_expert = jnp.minimum(jnp.sum(ends[None, :] <= tile_start[:, None], axis=1),
                                      N_EXPERTS - 1).astype(jnp.int32)
            n_used = (ends[-1] // tme).astype(jnp.int32).reshape(1)
            xs = _dispatch(h2, info, base, loff, seg, n_rows, tm)
            ys = _ffn(xs, moe_wg[i].astype(BF16), moe_wu[i].astype(BF16), moe_wd[i].astype(BF16),
                      tile_expert, n_used, tme, FFN_COL_TILE)
            x2d = _combine(xn, info, ys, base, loff, seg, tm)
    return x2d.reshape(b, s, d)
```

```python
import functools

import jax
import jax.numpy as jnp
from jax import lax
from jax.experimental import pallas as pl
from jax.experimental.pallas import tpu as pltpu

F32 = jnp.float32
BF16 = jnp.bfloat16

D_MODEL = 1024
HEAD_DIM = 64
RWKV_DIM = 384
CONV_DIM = 256
ATTN_DIM = 384
LORA_W = 128
RWKV_IN = 3 * RWKV_DIM + LORA_W
CONV_IN = 3 * CONV_DIM
ATTN_IN = 3 * ATTN_DIM
VRES_LORA = 16
VRES_PAD = 128
CONV_WIDTH = 3
DILATIONS = (1, 4, 16)
ATTN_BLOCK = 128
DIL_SPAN = ATTN_BLOCK * 16
ROPE_THETA = 10000.0
D_FF = 3584
N_EXPERTS = 8
NORM_EPS = 1e-6
GN_EPS = 64e-5

LANES = 128
RWKV_CHUNK = 64
PAIR = 2 * HEAD_DIM
VMEM_LIMIT = 56 * 1024 * 1024
ROW_TILE = 512
FFN_ROW_TILE = 1024
FFN_COL_TILE = 512


def _params(*sem):
    return pltpu.CompilerParams(dimension_semantics=sem, vmem_limit_bytes=VMEM_LIMIT)


def _bdot(a, b):
    return jnp.dot(a.astype(BF16), b.astype(BF16), preferred_element_type=F32)


def _bdot_nt(a, b):
    return lax.dot_general(a.astype(BF16), b.astype(BF16), (((1,), (1,)), ((), ())),
                           preferred_element_type=F32)


def _shift_rows(x, carry_row, n=1):
    row = lax.broadcasted_iota(jnp.int32, (x.shape[0], 1), 0)
    out = pltpu.roll(x, n, 0)
    for i in range(n):
        out = jnp.where(row == i, carry_row[i:i + 1, :], out)
    return out


def _inproj_body(x_ref, g_ref, w_ref, cw_ref, cgain_ref, cseg_ref, cos_ref, sin_ref, qg_ref, kg_ref,
                 aseg_ref, *rest, has_vres, tiles_per_seq):
    if has_vres:
        prw_ref, pv_ref, yc_ref, q_ref, k_ref, v_ref, carry_ref = rest
    else:
        prw_ref, yc_ref, q_ref, k_ref, v_ref, carry_ref = rest
    o_conv, o_attn, o_end = RWKV_IN, RWKV_IN + CONV_IN, RWKV_IN + CONV_IN + ATTN_IN
    first_tile = lax.rem(pl.program_id(0), tiles_per_seq) == 0
    n_split = 2
    rows_h = x_ref.shape[0] // n_split
    hs = []
    for c in range(n_split):
        x = x_ref[pl.ds(c * rows_h, rows_h), :]
        hs.append((x * lax.rsqrt(jnp.mean(x * x, axis=-1, keepdims=True) + NORM_EPS)
                   * g_ref[...]).astype(BF16))
    ps = [jnp.dot(h, w_ref[...], preferred_element_type=F32) for h in hs]
    for c, p in enumerate(ps):
        rows = pl.ds(c * rows_h, rows_h)
        prw_ref[rows, :] = p[:, :o_conv]
        if has_vres:
            pv_ref[rows, :] = p[:, o_end:]
        _conv_tile(p[:, o_conv:o_attn], first_tile if c == 0 else False, cw_ref, cgain_ref, cseg_ref,
                   yc_ref, carry_ref, rows)
        _attn_prep_tile(p[:, o_attn:o_end], cos_ref, sin_ref, qg_ref, kg_ref, aseg_ref,
                        q_ref, k_ref, v_ref, rows)


def _inproj(x, gain, w, conv, attn, has_vres, tm):
    b, s, d = x.shape
    t = b * s
    n = w.shape[1]
    tps = s // tm
    conv_w, conv_gain, conv_seg = conv
    cos, sin, qg, kg, attn_seg = attn
    full = lambda arr: pl.BlockSpec(arr.shape, lambda i: (0, 0))
    row = lambda wd: pl.BlockSpec((tm, wd), lambda i: (i, 0))
    tab = pl.BlockSpec((tm, ATTN_DIM), lambda i: (lax.rem(i, tps), 0))
    qkv = pl.BlockSpec((None, ATTN_DIM // PAIR, tm, PAIR), lambda i: (i // tps, 0, lax.rem(i, tps), 0))
    qkv_shape = jax.ShapeDtypeStruct((b, ATTN_DIM // PAIR, s, PAIR), F32)
    out_specs = [row(RWKV_IN)] + ([row(VRES_PAD)] if has_vres else []) + [row(CONV_DIM)] + [qkv] * 3
    out_shape = ([jax.ShapeDtypeStruct((t, RWKV_IN), F32)]
                 + ([jax.ShapeDtypeStruct((t, VRES_PAD), F32)] if has_vres else [])
                 + [jax.ShapeDtypeStruct((t, CONV_DIM), BF16)] + [qkv_shape] * 3)
    return pl.pallas_call(
        functools.partial(_inproj_body, has_vres=has_vres, tiles_per_seq=tps),
        grid=(t // tm,),
        in_specs=[row(d), full(gain), full(w), full(conv_w), full(conv_gain), full(conv_seg),
                  tab, tab, full(qg), full(kg), full(attn_seg)],
        out_specs=out_specs, out_shape=out_shape,
        scratch_shapes=[pltpu.VMEM((CONV_WIDTH - 1, CONV_DIM), F32)],
        compiler_params=_params("arbitrary"),
        name="inproj",
    )(x.reshape(t, d), gain, w, conv_w, conv_gain, conv_seg, cos, sin, qg, kg, attn_seg)


def _rwkv_body(*refs, has_vres):
    if has_vres:
        (p_ref, pv_ref, vf_ref, mu_ref, vmu_ref, wl_ref, v2_ref, vec_ref, seg_ref, tri_ref,
         cm_ref, y_ref, prev_ref, pprev_ref, h_ref) = refs
    else:
        (p_ref, mu_ref, wl_ref, vec_ref, seg_ref, tri_ref,
         cm_ref, y_ref, vout_ref, prev_ref, h_ref) = refs
    @pl.when(pl.program_id(0) == 0)
    def _():
        prev_ref[...] = jnp.zeros_like(prev_ref)
        h_ref[...] = jnp.zeros_like(h_ref)
        if has_vres:
            pprev_ref[...] = jnp.zeros_like(pprev_ref)

    nb, nt = p_ref.shape[0], p_ref.shape[1]

    def token_shift(src_ref, carry_ref, mix):
        out = []
        for bi in range(nb):
            cur = src_ref[bi]
            prev = _shift_rows(cur, carry_ref[bi:bi + 1, :])
            carry_ref[bi:bi + 1, :] = cur[nt - 1:nt, :]
            out.append(cur + (prev - cur) * mix)
        return jnp.concatenate(out, axis=0)

    x = token_shift(p_ref, prev_ref, mu_ref[...])
    r = x[:, 0:RWKV_DIM]
    k = x[:, RWKV_DIM:2 * RWKV_DIM]
    v = x[:, 2 * RWKV_DIM:3 * RWKV_DIM]
    z = x[:, 3 * RWKV_DIM:]
    lane = lax.broadcasted_iota(jnp.int32, (1, LANES), 1)
    zt = jnp.where(lane < 32, jnp.tanh(z), jnp.where(lane < 64, z, jax.nn.sigmoid(z)))
    lo = _bdot(zt, wl_ref[...])
    w0, a0, k_k, k_a, r_k, ln_w, ln_b, v0 = (vec_ref[i:i + 1, :] for i in range(8))
    seg = seg_ref[...]

    zz = -(w0 + lo[:, 0:RWKV_DIM])
    softplus = jnp.maximum(zz, 0.0) + jnp.log(1.0 + jnp.exp(-jnp.abs(zz)))
    lw = -jnp.exp(-softplus - 0.5)
    a = jax.nn.sigmoid(a0 + lo[:, RWKV_DIM:2 * RWKV_DIM])
    g = lo[:, 2 * RWKV_DIM:]
    if has_vres:
        xv = token_shift(pv_ref, pprev_ref, vmu_ref[...])
        v_first = vf_ref[...].reshape(nb * nt, RWKV_DIM)
        v = v + (v_first - v) * jax.nn.sigmoid(v0 + _bdot(xv, v2_ref[...]))
    else:
        vout_ref[...] = v.reshape(nb, nt, RWKV_DIM)
    kk = k * k_k
    kk = kk * jnp.minimum(lax.rsqrt(_bdot(kk * kk, seg)), 1e12)
    kmod = k * (1.0 + (a - 1.0) * k_a)
    bonus = _bdot(r * kmod * r_k, seg) * v

    tri = tri_ref[...]
    lw_hi = lw.astype(BF16)
    lw_lo = (lw - lw_hi.astype(F32)).astype(BF16)
    cum = (jnp.dot(tri, lw_hi, preferred_element_type=F32)
           + jnp.dot(tri, lw_lo, preferred_element_type=F32))
    pinv = jnp.exp(-cum)
    r_t = r * jnp.exp(cum)
    a_t = -kk * jnp.exp(cum - lw)
    b_t = kk * a * pinv
    k_t = kmod * pinv
    p_ends = [jnp.exp(cum[bi * nt + nt - 1:(bi + 1) * nt, :]) for bi in range(nb)]
    p_end_rows = jnp.concatenate([jnp.broadcast_to(pe, (nt, RWKV_DIM)) for pe in p_ends], axis=0)
    bh_t = b_t * p_end_rows
    kh_t = k_t * p_end_rows

    strict, incl, blk8, eye = (cm_ref[i] for i in range(4))
    m0 = (lane < HEAD_DIM).astype(F32)
    m1 = 1.0 - m0

    def stack(t, rows, sl):
        ts = t[rows, sl]
        return jnp.concatenate([ts * m0, ts * m1], axis=0)

    npair = RWKV_DIM // PAIR
    chains = [(bi, j) for bi in range(nb) for j in range(npair)]
    each = lambda fn, *cols: [fn(*args) for args in zip(*cols)]
    cat0 = lambda *ts: jnp.concatenate(ts, axis=0)
    cat1 = lambda *ts: jnp.concatenate(ts, axis=1)

    def stacks(t):
        return [stack(t, slice(bi * nt, (bi + 1) * nt), slice(j * PAIR, (j + 1) * PAIR))
                for bi, j in chains]

    a_st, r_st, b_st, k_st, v_st, bh_st, kh_st = (stacks(t) for t in (a_t, r_t, b_t, k_t, v, bh_t, kh_t))
    gram = each(lambda a_, r_, b_, k_: _bdot_nt(cat0(a_, r_), cat0(b_, k_)), a_st, r_st, b_st, k_st)
    n2 = 2 * nt
    a_ab = [gm[:n2, :n2] * strict for gm in gram]
    a_ak = [gm[:n2, n2:] * strict for gm in gram]
    a_rb = [gm[n2:, :n2] * incl for gm in gram]
    a_rk = [gm[n2:, n2:] * incl for gm in gram]
    dg = [m * blk8 for m in a_ab]
    off = each(lambda m, d_: m - d_, a_ab, dg)
    d2 = each(_bdot, dg, dg)
    tdiag = [eye + d_ for d_ in dg]
    tdiag = each(lambda t_, d_: t_ + _bdot(t_, d_), tdiag, d2)
    d4 = each(_bdot, d2, d2)
    tdiag = each(lambda t_, d_: t_ + _bdot(t_, d_), tdiag, d4)
    e1 = each(_bdot, tdiag, off)
    e2 = each(_bdot, e1, e1)
    yb = [eye + e_ for e_ in e1]
    yb = each(lambda y_, e_: y_ + _bdot(y_, e_), yb, e2)
    e4 = each(_bdot, e2, e2)
    yb = each(lambda y_, e_: y_ + _bdot(y_, e_), yb, e4)
    tinv = each(_bdot, yb, tdiag)

    hs = [h_ref[bi, j] for bi, j in chains]
    xs = each(lambda a_, ak_, h_, v_: _bdot(cat1(a_, ak_), cat0(h_, v_)), a_st, a_ak, hs, v_st)
    us = each(_bdot, tinv, xs)
    y_st = each(lambda r_, rb_, rk_, h_, u_, v_: _bdot(cat1(r_, rb_, rk_), cat0(h_, u_, v_)),
                r_st, a_rb, a_rk, hs, us, v_st)
    pe_col = [jnp.sum(eye * p_ends[bi][:, j * PAIR:(j + 1) * PAIR], axis=1, keepdims=True)
              for bi, j in chains]
    h_new = each(lambda pc_, h_, bh_, kh_, u_, v_: pc_ * h_ + _bdot(cat0(bh_, kh_).T, cat0(u_, v_)),
                 pe_col, hs, bh_st, kh_st, us, v_st)
    for (bi, j), hn in zip(chains, h_new):
        h_ref[bi, j] = hn
    y_pair = [ys_[:nt] + ys_[nt:] for ys_ in y_st]
    y = cat0(*[cat1(*y_pair[bi * npair:(bi + 1) * npair]) for bi in range(nb)])
    inv_n = 1.0 / HEAD_DIM
    mean = _bdot(y, seg) * inv_n
    dy = y - mean
    var = _bdot(dy * dy, seg) * inv_n
    yn = dy * lax.rsqrt(var + GN_EPS) * ln_w + ln_b
    y_ref[...] = ((yn + bonus) * g).reshape(nb, nt, RWKV_DIM).astype(y_ref.dtype)


def _rwkv(p_rwkv, mu, wl, vecs, consts, vres=None):
    b, s, _ = p_rwkv.shape
    c = RWKV_CHUNK
    seg, tri, cm = consts
    tile = lambda wd: pl.BlockSpec((b, c, wd), lambda j: (0, j, 0))
    full = lambda arr: pl.BlockSpec(arr.shape, lambda j: (0,) * arr.ndim)
    y_shape = jax.ShapeDtypeStruct((b, s, RWKV_DIM), BF16)
    scratch = [pltpu.VMEM((b, RWKV_IN), F32)]
    if vres is None:
        ins = [p_rwkv, mu, wl, vecs, seg, tri, cm]
        in_specs = [tile(RWKV_IN)] + [full(t) for t in ins[1:]]
        out_shape = [y_shape, jax.ShapeDtypeStruct((b, s, RWKV_DIM), F32)]
        out_specs = [tile(RWKV_DIM), tile(RWKV_DIM)]
    else:
        p_vres, v_first, vmu, v2 = vres
        ins = [p_rwkv, p_vres, v_first, mu, vmu, wl, v2, vecs, seg, tri, cm]
        in_specs = [tile(RWKV_IN), tile(VRES_PAD), tile(RWKV_DIM)] + [full(t) for t in ins[3:]]
        out_shape = [y_shape]
        out_specs = [tile(RWKV_DIM)]
        scratch.append(pltpu.VMEM((b, VRES_PAD), F32))
    scratch.append(pltpu.VMEM((b, RWKV_DIM // PAIR, PAIR, PAIR), F32))
    return pl.pallas_call(
        functools.partial(_rwkv_body, has_vres=vres is not None),
        grid=(s // c,),
        in_specs=in_specs, out_specs=out_specs, out_shape=out_shape,
        scratch_shapes=scratch,
        compiler_params=_params("arbitrary"),
        name="rwkv7",
    )(*ins)


def _conv_tile(p, first_tile, w_ref, gain_ref, seg_ref, o_ref, carry_ref, rows):
    if first_tile is not False:
        @pl.when(first_tile)
        def _():
            carry_ref[...] = jnp.zeros_like(carry_ref)

    nt = p.shape[0]
    u = p[:, CONV_DIM:2 * CONV_DIM] * p[:, 2 * CONV_DIM:]
    carry = carry_ref[...]
    u1 = _shift_rows(u, carry[1:2, :], 1)
    u2 = _shift_rows(u, carry, 2)
    carry_ref[...] = u[nt - 2:nt, :]
    w = w_ref[...]
    y = p[:, 0:CONV_DIM] * (w[0:1, :] * u2 + w[1:2, :] * u1 + w[2:3, :] * u)
    ms = _bdot(y * y, seg_ref[...]) * (1.0 / HEAD_DIM)
    o_ref[rows, :] = (y * lax.rsqrt(ms + NORM_EPS) * gain_ref[...]).astype(o_ref.dtype)


def _attn_prep_tile(p, cos_ref, sin_ref, qg_ref, kg_ref, seg_ref, q_ref, k_ref, v_ref, rows):
    seg = seg_ref[...]
    cos, sin = cos_ref[rows, :], sin_ref[rows, :]
    lane = lax.broadcasted_iota(jnp.int32, (1, LANES), 1)
    first = (lane & (HEAD_DIM - 1)) < HEAD_DIM // 2
    for part, g_ref, o_ref, scale in ((0, qg_ref, q_ref, HEAD_DIM ** -0.5), (1, kg_ref, k_ref, 1.0)):
        x = p[:, part * ATTN_DIM:(part + 1) * ATTN_DIM]
        ms = _bdot(x * x, seg) * (1.0 / HEAD_DIM)
        xn = x * lax.rsqrt(ms + NORM_EPS) * g_ref[...]
        rot = []
        for j in range(ATTN_DIM // LANES):
            xs = xn[:, j * LANES:(j + 1) * LANES]
            rot.append(jnp.where(first, pltpu.roll(xs, LANES - HEAD_DIM // 2, 1),
                                 pltpu.roll(xs, HEAD_DIM // 2, 1)))
        out = (xn * cos + jnp.concatenate(rot, axis=1) * sin) * scale
        for j in range(ATTN_DIM // PAIR):
            o_ref[j, rows, :] = out[:, j * PAIR:(j + 1) * PAIR]
    for j in range(ATTN_DIM // PAIR):
        v_ref[j, rows, :] = p[:, 2 * ATTN_DIM + j * PAIR:2 * ATTN_DIM + (j + 1) * PAIR]


def _pair_major_spec(rows, row_index):
    return pl.BlockSpec((None, ATTN_DIM // PAIR, rows, PAIR), lambda i, j: (i, 0, row_index(j), 0))


def _attn_body(q_ref, k_ref, v_ref, kprev_ref, vprev_ref, o_ref, l_ref, *, dil):
    span = pl.program_id(1)
    blk = ATTN_BLOCK
    npair = ATTN_DIM // PAIR
    qi = lax.broadcasted_iota(jnp.int32, (blk, 2 * blk), 0)
    kj = lax.broadcasted_iota(jnp.int32, (blk, 2 * blk), 1)
    dist = qi + blk - kj
    band = (dist >= 0) & (dist <= ATTN_BLOCK)
    lane = lax.broadcasted_iota(jnp.int32, (1, LANES), 1)
    head0 = lane < HEAD_DIM
    m0 = head0.astype(F32)
    heads = [(j, hm) for j in range(npair) for hm in (m0, 1.0 - m0)]

    def block_attn(q, kcat, vcat, block_index):
        mask = band & (kj + block_index * (2 * blk) >= blk)
        s_all = [_bdot_nt(q[j] * hm, kcat[j]) for j, hm in heads]
        s_all = [jnp.where(mask, s, -jnp.inf) for s in s_all]
        m_all = [jnp.max(s, axis=-1, keepdims=True) for s in s_all]
        pe_all = [jnp.exp(s - m) for s, m in zip(s_all, m_all)]
        den_all = [jnp.sum(pe, axis=-1, keepdims=True) for pe in pe_all]
        o_all = [_bdot(pe, vcat[j]) * (1.0 / den) for pe, den, (j, _) in zip(pe_all, den_all, heads)]
        lse_all = [m + jnp.log(den) for m, den in zip(m_all, den_all)]
        o_pair = [jnp.where(head0, o_all[2 * j], o_all[2 * j + 1]) for j in range(npair)]
        l_pair = [jnp.where(head0, lse_all[2 * j], lse_all[2 * j + 1]) for j in range(npair)]
        return o_pair, l_pair

    if dil == 1:
        nq = q_ref.shape[1] // blk
        kext = [jnp.concatenate([kprev_ref[j], k_ref[j]], axis=0) for j in range(npair)]
        vext = [jnp.concatenate([vprev_ref[j], v_ref[j]], axis=0) for j in range(npair)]
        for qb in range(nq):
            rows = slice(qb * blk, (qb + 1) * blk)
            keys = slice(qb * blk, (qb + 2) * blk)
            o_pair, l_pair = block_attn([q_ref[j, rows, :] for j in range(npair)],
                                        [kx[keys] for kx in kext], [vx[keys] for vx in vext],
                                        span * nq + qb)
            for j in range(npair):
                o_ref[j, rows, :] = o_pair[j]
                l_ref[j, rows, :] = l_pair[j]
    else:
        def one_class(r, carry):
            rows = pl.ds(r, blk, stride=dil)
            cat = lambda prev_ref, ref: [jnp.concatenate([prev_ref[j, rows, :], ref[j, rows, :]], axis=0)
                                         for j in range(npair)]
            o_pair, l_pair = block_attn([q_ref[j, rows, :] for j in range(npair)],
                                        cat(kprev_ref, k_ref), cat(vprev_ref, v_ref), span)
            for j in range(npair):
                o_ref[j, rows, :] = o_pair[j]
                l_ref[j, rows, :] = l_pair[j]
            return carry
        lax.fori_loop(0, dil, one_class, 0)


def _attn(q, k, v, dil):
    b, _, s, _ = q.shape
    if dil == 1:
        nq = 4
        span = ATTN_BLOCK * nq
        prev = _pair_major_spec(ATTN_BLOCK, lambda m: jnp.maximum(m * nq - 1, 0))
    else:
        span = ATTN_BLOCK * dil
        prev = _pair_major_spec(span, lambda m: jnp.maximum(m - 1, 0))
    spec = _pair_major_spec(span, lambda m: m)
    shp = jax.ShapeDtypeStruct(q.shape, F32)
    return pl.pallas_call(
        functools.partial(_attn_body, dil=dil),
        grid=(b, s // span),
        in_specs=[spec, spec, spec, prev, prev],
        out_specs=[spec, spec],
        out_shape=[shp, shp],
        compiler_params=_params("arbitrary", "arbitrary"),
        name=f"attn_d{dil}",
    )(q, k, v, k, v)


def _attn_combine_tile(o_refs, l_refs, gain_ref, seg_ref):
    pieces = []
    for j in range(ATTN_DIM // PAIR):
        ls = [r[j] for r in l_refs]
        m = functools.reduce(jnp.maximum, ls)
        es = [jnp.exp(l - m) for l in ls]
        den = functools.reduce(jnp.add, es)
        o = functools.reduce(jnp.add, [e * r[j] for e, r in zip(es, o_refs)]) / den
        ms = _bdot(o * o, seg_ref[...]) * (1.0 / HEAD_DIM)
        gain = gain_ref[:, j * PAIR:(j + 1) * PAIR]
        pieces.append((o * lax.rsqrt(ms + NORM_EPS) * gain).astype(BF16))
    return pieces


def _attn_combine_body(*refs):
    n = len(DILATIONS)
    gain_ref, seg_ref, y_ref = refs[2 * n:]
    for j, piece in enumerate(_attn_combine_tile(refs[:n], refs[n:2 * n], gain_ref, seg_ref)):
        y_ref[:, j * PAIR:(j + 1) * PAIR] = piece


def _attn_combine(os_, ls_, gain, seg, tm):
    b, _, s, _ = os_[0].shape
    tile = _pair_major_spec(tm, lambda j: j)
    full = lambda arr: pl.BlockSpec(arr.shape, lambda i, j: (0, 0))
    return pl.pallas_call(
        _attn_combine_body,
        grid=(b, s // tm),
        in_specs=[tile] * (2 * len(os_)) + [full(gain), full(seg)],
        out_specs=pl.BlockSpec((None, tm, ATTN_DIM), lambda i, j: (i, j, 0)),
        out_shape=jax.ShapeDtypeStruct((b, s, ATTN_DIM), BF16),
        compiler_params=_params("arbitrary", "arbitrary"),
        name="attn_combine",
    )(*os_, *ls_, gain, seg)


INFO_E0, INFO_E1, INFO_R0, INFO_R1, INFO_G0, INFO_G1 = range(6)


def _mix_residual(x_ref, yr_ref, yc_ref, ya, wo_ref, g_ref):
    o1, o2 = RWKV_DIM, RWKV_DIM + CONV_DIM
    acc = jnp.dot(yr_ref[...], wo_ref[0:o1, :], preferred_element_type=F32)
    acc += jnp.dot(yc_ref[...], wo_ref[o1:o2, :], preferred_element_type=F32)
    if isinstance(ya, list):
        for j, piece in enumerate(ya):
            acc += jnp.dot(piece, wo_ref[o2 + j * PAIR:o2 + (j + 1) * PAIR, :],
                           preferred_element_type=F32)
    else:
        acc += jnp.dot(ya[...], wo_ref[o2:, :], preferred_element_type=F32)
    xn = x_ref[...] + acc
    h = xn * lax.rsqrt(jnp.mean(xn * xn, axis=-1, keepdims=True) + NORM_EPS) * g_ref[...]
    return xn, h


def _outproj_body(x_ref, yr_ref, yc_ref, *refs):
    n = len(DILATIONS)
    (again_ref, aseg_ref, wo_ref, g_ref, rwh_ref, rwl_ref, tri_ref,
     xn_ref, h_ref, info_ref, cnt_ref) = refs[2 * n:]
    ya = _attn_combine_tile(refs[:n], refs[n:2 * n], again_ref, aseg_ref)
    xn, h = _mix_residual(x_ref, yr_ref, yc_ref, ya, wo_ref, g_ref)
    xn_ref[...] = xn
    h_ref[...] = h.astype(h_ref.dtype)

    h_hi = h.astype(BF16)
    h_lo = (h - h_hi.astype(F32)).astype(BF16)
    rwh = rwh_ref[...]
    logits = (jnp.dot(h_hi, rwh, preferred_element_type=F32)
              + jnp.dot(h_lo, rwh, preferred_element_type=F32)
              + jnp.dot(h_hi, rwl_ref[...], preferred_element_type=F32))
    lane = lax.broadcasted_iota(jnp.int32, (1, LANES), 1)
    logits = jnp.where(lane < N_EXPERTS, logits, -jnp.inf)

    def top(lg):
        m = jnp.max(lg, axis=-1, keepdims=True)
        idx = jnp.min(jnp.where(lg == m, lane, LANES), axis=-1, keepdims=True)
        return m, idx, lane == idx

    m1, i1, oh1 = top(logits)
    m2, i2, oh2 = top(jnp.where(oh1, -jnp.inf, logits))
    zexp = jnp.exp(m2 - m1)
    g0 = 1.0 / (1.0 + zexp)
    g1 = zexp * g0
    oh = oh1.astype(F32) + oh2.astype(F32)
    tot = jnp.dot(tri_ref[...], oh.astype(BF16), preferred_element_type=F32)
    r0 = jnp.sum(jnp.where(oh1, tot, 0.0), axis=-1, keepdims=True)
    r1 = jnp.sum(jnp.where(oh2, tot, 0.0), axis=-1, keepdims=True)
    cnt_ref[...] = jnp.broadcast_to(jnp.sum(oh, axis=0, keepdims=True), cnt_ref.shape)
    info = jnp.zeros(logits.shape, F32)
    for ln, val in ((INFO_E0, i1.astype(F32)), (INFO_E1, i2.astype(F32)), (INFO_R0, r0),
                    (INFO_R1, r1), (INFO_G0, g0), (INFO_G1, g1)):
        info = jnp.where(lane == ln, val, info)
    info_ref[...] = info


def _outproj(x2d, yr, yc, attn, wo, gain, router, tm):
    t, d = x2d.shape
    os_, ls_, attn_gain, attn_seg = attn
    tps = os_[0].shape[2] // tm
    row = lambda wd: pl.BlockSpec((tm, wd), lambda i: (i, 0))
    full = lambda arr: pl.BlockSpec(arr.shape, lambda i: (0, 0))
    pairs = pl.BlockSpec((None, ATTN_DIM // PAIR, tm, PAIR), lambda i: (i // tps, 0, lax.rem(i, tps), 0))
    return pl.pallas_call(
        _outproj_body,
        grid=(t // tm,),
        in_specs=[row(d), row(RWKV_DIM), row(CONV_DIM)] + [pairs] * (2 * len(os_))
                 + [full(attn_gain), full(attn_seg), full(wo), full(gain)]
                 + [full(a) for a in router],
        out_specs=[row(d), row(d), row(LANES), pl.BlockSpec((8, LANES), lambda i: (i, 0))],
        out_shape=[jax.ShapeDtypeStruct((t, d), F32), jax.ShapeDtypeStruct((t, d), BF16),
                   jax.ShapeDtypeStruct((t, LANES), F32),
                   jax.ShapeDtypeStruct((t // tm * 8, LANES), F32)],
        compiler_params=_params("arbitrary"),
        name="outproj",
    )(x2d, yr, yc, *os_, *ls_, attn_gain, attn_seg, wo, gain, *router)


def _ffn_body(te_ref, nu_ref, *refs, from_mixers):
    if from_mixers:
        (xres_ref, yr_ref, yc_ref, ya_ref, wo_ref, g_ref, wg_ref, wu_ref, wd_ref, o_ref,
         acc_ref, xn_ref, h_ref) = refs
    else:
        h_ref, wg_ref, wu_ref, wd_ref, o_ref, acc_ref = refs
    i, f = pl.program_id(0), pl.program_id(1)

    @pl.when(f == 0)
    def _():
        acc_ref[...] = jnp.zeros_like(acc_ref)
        if from_mixers:
            xn, h = _mix_residual(xres_ref, yr_ref, yc_ref, ya_ref, wo_ref, g_ref)
            xn_ref[...] = xn
            h_ref[...] = h

    @pl.when(i < nu_ref[0])
    def _():
        x = h_ref[...].astype(BF16)
        hg = jnp.dot(x, wg_ref[...], preferred_element_type=F32)
        hu = jnp.dot(x, wu_ref[...], preferred_element_type=F32)
        act = hg * jax.nn.sigmoid(hg) * hu
        acc_ref[...] += jnp.dot(act.astype(BF16), wd_ref[...], preferred_element_type=F32)

    @pl.when(f == pl.num_programs(1) - 1)
    def _():
        out = acc_ref[...]
        if from_mixers:
            out = out + xn_ref[...]
        o_ref[...] = out.astype(o_ref.dtype)


def _ffn(x, wg, wu, wd, tile_expert, n_used, tm, tf, mixers=None):
    ff = wg.shape[-1]
    row = lambda wd_: pl.BlockSpec((tm, wd_), lambda i, f, te, nu: (i, 0))
    full = lambda arr: pl.BlockSpec(arr.shape, lambda i, f, te, nu: (0, 0))
    weights = [pl.BlockSpec((None, wg.shape[1], tf), lambda i, f, te, nu: (te[i], 0, f)),
               pl.BlockSpec((None, wg.shape[1], tf), lambda i, f, te, nu: (te[i], 0, f)),
               pl.BlockSpec((None, tf, wg.shape[1]), lambda i, f, te, nu: (te[i], f, 0))]
    if mixers is None:
        n, d = x.shape
        ins = [x, wg, wu, wd]
        in_specs = [row(d)] + weights
        scratch = [pltpu.VMEM((tm, d), F32)]
    else:
        x_res, yr, yc, ya, wo, gain = mixers
        n, d = x_res.shape
        ins = [x_res, yr, yc, ya, wo, gain, wg, wu, wd]
        in_specs = [row(d), row(RWKV_DIM), row(CONV_DIM), row(ATTN_DIM), full(wo), full(gain)] + weights
        scratch = [pltpu.VMEM((tm, d), F32) for _ in range(3)]
    return pl.pallas_call(
        functools.partial(_ffn_body, from_mixers=mixers is not None),
        grid_spec=pltpu.PrefetchScalarGridSpec(
            num_scalar_prefetch=2,
            grid=(n // tm, ff // tf),
            in_specs=in_specs,
            out_specs=row(d),
            scratch_shapes=scratch),
        out_shape=jax.ShapeDtypeStruct((n, d), F32),
        compiler_params=_params("arbitrary", "arbitrary"),
        name="swiglu",
    )(tile_expert, n_used, *ins)


SEG_ALIGN = 16


def _stage_positions(info, loffv_ref):
    lane_f = lax.broadcasted_iota(jnp.int32, (1, LANES), 1).astype(F32)
    pos = []
    for e_ln, r_ln in ((INFO_E0, INFO_R0), (INFO_E1, INFO_R1)):
        onehot = lane_f == info[:, e_ln:e_ln + 1]
        seg_start = jnp.sum(jnp.where(onehot, loffv_ref[0:1, :], 0.0), axis=-1, keepdims=True)
        pos.append(seg_start + info[:, r_ln:r_ln + 1])
    return pos


def _segment_copies(tile, tm, base_ref, loff_ref, seg_ref, make_copy, start):
    for e in range(N_EXPERTS):
        seg = seg_ref[tile * N_EXPERTS + e]
        size = tm
        while size >= SEG_ALIGN:
            off = seg & ~(2 * size - 1)

            @pl.when((seg & size) != 0)
            def _():
                cp = make_copy(pl.multiple_of(base_ref[tile * N_EXPERTS + e] + off, SEG_ALIGN),
                               pl.multiple_of(loff_ref[tile * N_EXPERTS + e] + off, SEG_ALIGN), size)
                if start:
                    cp.start()
                else:
                    cp.wait()
            size //= 2


def _dispatch_body(base_ref, loff_ref, seg_ref, h_ref, info_ref, loffv_ref, xs_in, xs_out,
                   stage_ref, sem):
    del xs_in
    i = pl.program_id(0)
    tm = h_ref.shape[0]
    n_stage = stage_ref.shape[0]
    lane = lax.broadcasted_iota(jnp.int32, (1, LANES), 1)
    pos = _stage_positions(info_ref[...], loffv_ref)
    pos_t = jnp.transpose(jnp.where(lane == 0, pos[0], jnp.where(lane == 1, pos[1], -1.0)))
    rows = lax.broadcasted_iota(jnp.int32, (n_stage, 1), 0).astype(F32)
    perm = ((rows == pos_t[0:1, :]) | (rows == pos_t[1:2, :])).astype(BF16)
    stage_ref[...] = jnp.dot(perm, h_ref[...].astype(BF16),
                             preferred_element_type=F32).astype(stage_ref.dtype)
    make_copy = lambda row, srow, size: pltpu.make_async_copy(
        stage_ref.at[pl.ds(srow, size)], xs_out.at[pl.ds(row, size)], sem)
    for start in (True, False):
        _segment_copies(i, tm, base_ref, loff_ref, seg_ref, make_copy, start)


def _moe_specs(tm):
    row = lambda wd: pl.BlockSpec((tm, wd), lambda i, *_: (i, 0))
    vec = pl.BlockSpec((8, LANES), lambda i, *_: (i, 0))
    lanes8 = lambda a: jnp.repeat(jnp.pad(a.astype(F32), ((0, 0), (0, LANES - N_EXPERTS))), 8, axis=0)
    flat = lambda a: a.reshape(-1).astype(jnp.int32)
    return row, vec, lanes8, flat


def _dispatch(h2, info, base, loff, seg, n_rows, tm):
    t, d = h2.shape
    n_stage = 2 * tm + N_EXPERTS * SEG_ALIGN
    row, vec, lanes8, flat = _moe_specs(tm)
    any_spec = pl.BlockSpec(memory_space=pl.ANY)
    return pl.pallas_call(
        _dispatch_body,
        grid_spec=pltpu.PrefetchScalarGridSpec(
            num_scalar_prefetch=3,
            grid=(t // tm,),
            in_specs=[row(d), row(LANES), vec, any_spec],
            out_specs=any_spec,
            scratch_shapes=[pltpu.VMEM((n_stage, d), BF16), pltpu.SemaphoreType.DMA]),
        out_shape=jax.ShapeDtypeStruct((n_rows, d), BF16),
        input_output_aliases={6: 0},
        compiler_params=_params("arbitrary"),
        name="moe_dispatch",
    )(flat(base), flat(loff), flat(seg), h2, info, lanes8(loff), jnp.zeros((n_rows, d), BF16))


def _combine_body(base_ref, loff_ref, seg_ref, x_ref, info_ref, loffv_ref, ys_hbm, o_ref,
                  stage_ref, sem):
    i = pl.program_id(0)
    tm = x_ref.shape[0]
    n_stage = stage_ref.shape[1]
    slot = lax.rem(i, 2)

    def copies(tile, sl, start):
        make_copy = lambda row, srow, size: pltpu.make_async_copy(
            ys_hbm.at[pl.ds(row, size)], stage_ref.at[sl, pl.ds(srow, size)], sem.at[sl])
        _segment_copies(tile, tm, base_ref, loff_ref, seg_ref, make_copy, start)

    @pl.when(i == 0)
    def _():
        stage_ref[...] = jnp.zeros_like(stage_ref)
        copies(0, 0, True)

    @pl.when(i + 1 < pl.num_programs(0))
    def _():
        copies(i + 1, 1 - slot, True)

    info = info_ref[...]
    pos = _stage_positions(info, loffv_ref)
    col = lax.broadcasted_iota(jnp.int32, (1, n_stage), 1).astype(F32)
    sel = (jnp.where(col == pos[0], info[:, INFO_G0:INFO_G0 + 1], 0.0)
           + jnp.where(col == pos[1], info[:, INFO_G1:INFO_G1 + 1], 0.0))
    sel_hi = sel.astype(BF16)
    sel_lo = (sel - sel_hi.astype(F32)).astype(BF16)
    copies(i, slot, False)
    ys_tile = stage_ref[slot].astype(BF16)
    o_ref[...] = (x_ref[...] + jnp.dot(sel_hi, ys_tile, preferred_element_type=F32)
                  + jnp.dot(sel_lo, ys_tile, preferred_element_type=F32))


def _combine(xn, info, ys, base, loff, seg, tm):
    t, d = xn.shape
    n_stage = 2 * tm + N_EXPERTS * SEG_ALIGN
    row, vec, lanes8, flat = _moe_specs(tm)
    return pl.pallas_call(
        _combine_body,
        grid_spec=pltpu.PrefetchScalarGridSpec(
            num_scalar_prefetch=3,
            grid=(t // tm,),
            in_specs=[row(d), row(LANES), vec, pl.BlockSpec(memory_space=pl.ANY)],
            out_specs=row(d),
            scratch_shapes=[pltpu.VMEM((2, n_stage, d), F32), pltpu.SemaphoreType.DMA((2,))]),
        out_shape=jax.ShapeDtypeStruct((t, d), F32),
        compiler_params=_params("arbitrary"),
        name="moe_combine",
    )(flat(base), flat(loff), flat(seg), xn, info, lanes8(loff), ys)


def _seg_matrix(n):
    i = jnp.arange(n) // HEAD_DIM
    return (i[:, None] == i[None, :]).astype(BF16)


def _rwkv_consts(nb):
    c = RWKV_CHUNK
    t = jnp.arange(nb * c)
    tri = ((t[:, None] >= t[None, :]) & (t[:, None] // c == t[None, :] // c)).astype(BF16)
    i = jnp.arange(2 * c)
    same = (i[:, None] // c) == (i[None, :] // c)
    ti, tj = (i % c)[:, None], (i % c)[None, :]
    strict = same & (ti > tj)
    incl = same & (ti >= tj)
    blk8 = (i[:, None] // 8) == (i[None, :] // 8)
    eye = i[:, None] == i[None, :]
    cm = jnp.stack([strict, incl, blk8, eye]).astype(F32)
    return _seg_matrix(RWKV_DIM), tri, cm


def _rope_tables(s):
    half = HEAD_DIM // 2
    inv_freq = ROPE_THETA ** (-jnp.arange(half, dtype=F32) * 2.0 / HEAD_DIM)
    ang = jnp.arange(s, dtype=F32)[:, None] * inv_freq[None, :]
    cos, sin = jnp.cos(ang), jnp.sin(ang)
    reps = ATTN_DIM // HEAD_DIM
    cos_t = jnp.tile(jnp.concatenate([cos, cos], axis=1), (1, reps))
    sin_t = jnp.tile(jnp.concatenate([-sin, sin], axis=1), (1, reps))
    return cos_t, sin_t


def _row_tile(n, want):
    return want if n % want == 0 else n


def kernel(x, mix_norm, w_in, tshift_mu, vres_w_in, vres_mu, vres_v0, vres_v2, decay_w0, decay_w2, iclr_a0, iclr_a2, gate_g2, k_k, k_a, r_k, ln_x_w, ln_x_b, conv_w, conv_out_norm, q_norm, k_norm, attn_out_norm, w_o, ffn_norm, dense_wg, dense_wu, dense_wd, router, moe_wg, moe_wu, moe_wd):
    b, s, d = x.shape
    assert d == D_MODEL and s % DIL_SPAN == 0
    t = b * s
    depth = w_in.shape[0]
    tm = _row_tile(s, ROW_TILE)
    seg384 = _seg_matrix(RWKV_DIM)
    seg256 = _seg_matrix(CONV_DIM)
    rwkv_consts = _rwkv_consts(b)
    cos_t, sin_t = _rope_tables(s)
    zeros_r = jnp.zeros((RWKV_DIM,), F32)

    x2d = x.reshape(t, d)
    v_first = None
    for l in range(depth):
        w_comb = w_in[l]
        if l > 0:
            w_comb = jnp.concatenate(
                [w_comb, jnp.pad(vres_w_in[l - 1], ((0, 0), (0, VRES_PAD - VRES_LORA)))], axis=1)
        reps = ATTN_DIM // HEAD_DIM
        outs = _inproj(x2d.reshape(b, s, d), mix_norm[l][None, :], w_comb.astype(BF16),
                       (conv_w[l], conv_out_norm[l].reshape(1, -1), seg256),
                       (cos_t, sin_t, jnp.tile(q_norm[l], reps)[None, :],
                        jnp.tile(k_norm[l], reps)[None, :], seg384), l > 0, tm)
        p_rwkv = outs[0].reshape(b, s, RWKV_IN)
        y_conv, q, k, v = outs[-4:]

        wl = jnp.zeros((LORA_W, 3 * RWKV_DIM), F32)
        wl = wl.at[0:32, 0:RWKV_DIM].set(decay_w2[l])
        wl = wl.at[32:64, RWKV_DIM:2 * RWKV_DIM].set(iclr_a2[l])
        wl = wl.at[64:128, 2 * RWKV_DIM:].set(gate_g2[l]).astype(BF16)
        vecs = jnp.stack([decay_w0[l], iclr_a0[l], k_k[l], k_a[l], r_k[l].reshape(-1), ln_x_w[l],
                          ln_x_b[l], vres_v0[l - 1] if l > 0 else zeros_r])
        mu = tshift_mu[l][None, :]
        if l == 0:
            y_rwkv, v_first = _rwkv(p_rwkv, mu, wl, vecs, rwkv_consts)
        else:
            vmu = jnp.pad(vres_mu[l - 1], (0, VRES_PAD - VRES_LORA))[None, :]
            v2 = jnp.pad(vres_v2[l - 1], ((0, VRES_PAD - VRES_LORA), (0, 0))).astype(BF16)
            (y_rwkv,) = _rwkv(p_rwkv, mu, wl, vecs, rwkv_consts,
                              vres=(outs[1].reshape(b, s, VRES_PAD), v_first, vmu, v2))

        res = [_attn(q, k, v, dil) for dil in DILATIONS]
        attn = ([r[0] for r in res], [r[1] for r in res], attn_out_norm[l].reshape(1, -1),
                _seg_matrix(PAIR))

        flat = lambda a: a.reshape(t, -1)
        wo = w_o[l].astype(BF16)
        gain = ffn_norm[l][None, :]
        i = l // 2
        if l % 2 == 0:
            tmf = _row_tile(t, FFN_ROW_TILE)
            n_tiles = t // tmf
            x2d = _ffn(None, dense_wg[i][None].astype(BF16), dense_wu[i][None].astype(BF16),
                       dense_wd[i][None].astype(BF16), jnp.zeros((n_tiles,), jnp.int32),
                       jnp.full((1,), n_tiles, jnp.int32), tmf, FFN_COL_TILE,
                       mixers=(x2d, flat(y_rwkv), flat(y_conv), flat(_attn_combine(*attn, tm)), wo, gain))
        else:
            rw = jnp.pad(router[i], ((0, 0), (0, LANES - N_EXPERTS)))
            rw_hi = rw.astype(BF16)
            rw_lo = (rw - rw_hi.astype(F32)).astype(BF16)
            rt = jnp.arange(tm)
            tri = (rt[:, None] > rt[None, :]).astype(BF16)
            xn, h2, info, cnt = _outproj(x2d, flat(y_rwkv), flat(y_conv), attn, wo, gain,
                                         (rw_hi, rw_lo, tri), tm)
            tme = FFN_ROW_TILE
            n_tt = t // tm
            cnt_te = cnt.reshape(n_tt, 8, LANES)[:, 0, :N_EXPERTS].astype(jnp.int32)
            seg = (cnt_te + SEG_ALIGN - 1) // SEG_ALIGN * SEG_ALIGN
            run = jnp.cumsum(seg, axis=0) - seg
            padded = (jnp.sum(seg, axis=0) + tme - 1) // tme * tme
            ends = jnp.cumsum(padded)
            base = (ends - padded)[None, :] + run
            loff = jnp.cumsum(seg, axis=1) - seg
            n_rows = pl.cdiv(2 * t + n_tt * N_EXPERTS * (SEG_ALIGN - 1) + N_EXPERTS * tme, tme) * tme
            n_tiles = n_rows // tme
            tile_start = jnp.arange(n_tiles, dtype=jnp.int32) * tme
            tile_expert = jnp.minimum(jnp.sum(ends[None, :] <= tile_start[:, None], axis=1),
                                      N_EXPERTS - 1).astype(jnp.int32)
            n_used = (ends[-1] // tme).astype(jnp.int32).reshape(1)
            xs = _dispatch(h2, info, base, loff, seg, n_rows, tm)
            ys = _ffn(xs, moe_wg[i].astype(BF16), moe_wu[i].astype(BF16), moe_wd[i].astype(BF16),
                      tile_expert, n_used, tme, FFN_COL_TILE)
            x2d = _combine(xn, info, ys, base, loff, seg, tm)
    return x2d.reshape(b, s, d)
```
